```python
import jax, jax.numpy as jnp
from jax import lax
import numpy as np

D_MODEL = 2048
BATCH = 4
SEQ = 4096
DEPTH = 4

N_EVEN = (DEPTH + 1) // 2
N_ODD = DEPTH // 2
HEAD_DIM = 128
H_HGRN = D_MODEL // 2 // HEAD_DIM
H_RET = D_MODEL // 2 // HEAD_DIM
D_HGRN = H_HGRN * HEAD_DIM
D_RET = H_RET * HEAD_DIM
H_FOX = D_MODEL // HEAD_DIM
D_FOX = H_FOX * HEAD_DIM
EVEN_IN = 4 * D_HGRN + 4 * D_RET
ODD_IN = 4 * D_FOX + H_FOX
D_FF = 128 * ((8 * D_MODEL // 3 + 127) // 128)
CONV_WIDTH = 3
CHUNK = 64
Q_BLOCK = 128
ROPE_BASE = 10000.0
RET_DECAY_BASE = 5.0
FOX_FORGET_BIAS = 3.0
EPS = 1e-6
LB_FLOOR = 1e-30
MASK_VALUE = -1e30

kernel_name = 'hybrid_hgrn2_retnet_fox_convffn'


def rmsnorm(x, w):
    xf = x.astype(jnp.float32)
    y = xf * lax.rsqrt(jnp.mean(xf * xf, axis=-1, keepdims=True) + EPS)
    return (y * w.astype(jnp.float32)).astype(x.dtype)


def head_rmsnorm(o, w):
    y = o * lax.rsqrt(jnp.mean(o * o, axis=-1, keepdims=True) + EPS)
    return y * w.astype(jnp.float32).reshape(o.shape[2:])


def head_layernorm(o, w):
    mu = jnp.mean(o, axis=-1, keepdims=True)
    c = o - mu
    y = c * lax.rsqrt(jnp.mean(c * c, axis=-1, keepdims=True) + EPS)
    return y * w.astype(jnp.float32).reshape(o.shape[2:])


def rotary(x, positions):
    d = x.shape[-1]
    inv = ROPE_BASE ** (-jnp.arange(0, d, 2, dtype=jnp.float32) / d)
    ang = positions.astype(jnp.float32)[:, None] * inv[None, :]
    cos = jnp.cos(ang)[None, :, None, :]
    sin = jnp.sin(ang)[None, :, None, :]
    x1, x2 = x[..., : d // 2], x[..., d // 2:]
    return jnp.concatenate([x1 * cos - x2 * sin, x2 * cos + x1 * sin], axis=-1)


def hgrn2_chunkwise(q, k, v, log_f):
    B, S, H, dk = q.shape
    dv = v.shape[-1]
    n = S // CHUNK

    def to_chunks(t):
        return t.reshape(B, n, CHUNK, H, t.shape[-1]).transpose(1, 0, 3, 2, 4)

    causal = jnp.tril(jnp.ones((CHUNK, CHUNK), dtype=bool))[None, None, :, :, None]

    def step(state, xs):
        qc, kc, vc, gc = xs
        b = jnp.cumsum(gc, axis=2)
        diff = b[:, :, :, None, :] - b[:, :, None, :, :]
        decay = jnp.where(causal, jnp.exp(jnp.minimum(diff, 0.0)), 0.0)
        scores = jnp.einsum('bhtd,bhsd,bhtsd->bhts', qc, kc, decay)
        o = (jnp.einsum('bhts,bhsv->bhtv', scores, vc)
             + jnp.einsum('bhtd,bhdv->bhtv', qc * jnp.exp(b), state))
        b_last = b[:, :, -1:, :]
        state = (jnp.exp(b_last)[:, :, 0, :, None] * state
                 + jnp.einsum('bhsd,bhsv->bhdv', kc * jnp.exp(b_last - b), vc))
        return state, o

    state0 = jnp.zeros((B, H, dk, dv), q.dtype)
    _, o = lax.scan(step, state0, (to_chunks(q), to_chunks(k), to_chunks(v), to_chunks(log_f)))
    return o.transpose(1, 0, 3, 2, 4).reshape(B, S, H, dv)


def retention_chunkwise(q, k, v, log_gamma):
    B, S, H, dk = q.shape
    dv = v.shape[-1]
    n = S // CHUNK

    def to_chunks(t):
        return t.reshape(B, n, CHUNK, H, t.shape[-1]).transpose(0, 3, 1, 2, 4)

    qc, kc, vc = to_chunks(q), to_chunks(k), to_chunks(v)
    pos = jnp.arange(CHUNK, dtype=jnp.float32)
    rel = pos[:, None] - pos[None, :]
    intra = jnp.where(rel >= 0, jnp.exp(log_gamma[:, None, None] * jnp.maximum(rel, 0.0)), 0.0)
    scores = jnp.einsum('bhntd,bhnsd->bhnts', qc, kc) * intra[None, :, None]
    o = jnp.einsum('bhnts,bhnsv->bhntv', scores, vc)
    w_end = jnp.exp(log_gamma[:, None] * (CHUNK - 1 - pos)[None, :])
    kv = jnp.einsum('bhnsd,hs,bhnsv->nbhdv', kc, w_end, vc)
    decay_chunk = jnp.exp(log_gamma * CHUNK)[None, :, None, None]

    def step(state, kv_n):
        return decay_chunk * state + kv_n, state

    _, prev = lax.scan(step, jnp.zeros_like(kv[0]), kv)
    w_start = jnp.exp(log_gamma[:, None] * (pos + 1.0)[None, :])
    o = o + jnp.einsum('bhntd,ht,nbhdv->bhntv', qc, w_start, prev)
    return o.transpose(0, 2, 3, 1, 4).reshape(B, S, H, dv)


def even_mixer(h, w_in, lb, hgrn_norm_w, ret_norm_w, w_out, positions):
    B, S, _ = h.shape
    f32 = jnp.float32
    proj = h @ w_in
    cuts = [D_HGRN, 2 * D_HGRN, 3 * D_HGRN, 4 * D_HGRN,
            4 * D_HGRN + D_RET, 4 * D_HGRN + 2 * D_RET, 4 * D_HGRN + 3 * D_RET]
    a_q, a_f, a_i, a_g, r_q, r_k, r_v, r_g = jnp.split(proj, cuts, axis=-1)

    def heads(t, n_heads):
        return t.reshape(B, S, n_heads, HEAD_DIM).astype(f32)

    z = heads(a_f, H_HGRN)
    lbh = lb.reshape(H_HGRN, HEAD_DIM).astype(f32)
    log_lb = jnp.log(jnp.maximum(lbh, LB_FLOOR))
    log_f = jnp.logaddexp(log_lb, jnp.log1p(-lbh) + jax.nn.log_sigmoid(z))
    k_in = (1.0 - lbh) * jax.nn.sigmoid(-z)
    o_a = hgrn2_chunkwise(jax.nn.silu(heads(a_q, H_HGRN)), k_in, heads(a_i, H_HGRN), log_f)
    o_a = head_rmsnorm(o_a, hgrn_norm_w).reshape(B, S, D_HGRN) * jax.nn.silu(a_g.astype(f32))

    log_gamma = jnp.log1p(-jnp.exp2(-RET_DECAY_BASE - jnp.arange(H_RET, dtype=f32)))
    rq = rotary(heads(r_q, H_RET), positions)
    rk = rotary(heads(r_k, H_RET), positions) * (HEAD_DIM ** -0.5)
    o_b = retention_chunkwise(rq, rk, heads(r_v, H_RET), log_gamma)
    o_b = head_layernorm(o_b, ret_norm_w).reshape(B, S, D_RET) * jax.nn.silu(r_g.astype(f32))

    o = jnp.concatenate([o_a, o_b], axis=-1).astype(h.dtype)
    return o @ w_out


def fox_attention(q, k, v, c):
    S = q.shape[2]
    outs = []
    for i in range(S // Q_BLOCK):
        t0, t1 = i * Q_BLOCK, (i + 1) * Q_BLOCK
        logits = jnp.einsum('bhtd,bhsd->bhts', q[:, :, t0:t1], k[:, :, :t1],
                            preferred_element_type=jnp.float32)
        logits = logits + c[:, :, t0:t1, None] - c[:, :, None, :t1]
        mask = (t0 + jnp.arange(Q_BLOCK))[:, None] >= jnp.arange(t1)[None, :]
        p = jax.nn.softmax(jnp.where(mask, logits, MASK_VALUE), axis=-1)
        outs.append(jnp.einsum('bhts,bhsv->bhtv', p.astype(v.dtype), v[:, :, :t1]))
    return jnp.concatenate(outs, axis=2)


def fox_mixer(h, w_in, b_f, w_out):
    B, S, _ = h.shape
    proj = h @ w_in
    q, k, v, g, f = jnp.split(proj, [D_FOX, 2 * D_FOX, 3 * D_FOX, 4 * D_FOX], axis=-1)
    log_f = jax.nn.log_sigmoid((f + b_f).astype(jnp.float32))
    c = jnp.cumsum(log_f, axis=1).transpose(0, 2, 1)

    def heads(t):
        return t.reshape(B, S, H_FOX, HEAD_DIM).transpose(0, 2, 1, 3)

    o = fox_attention(heads(q) * (HEAD_DIM ** -0.5), heads(k), heads(v), c)
    o = o.transpose(0, 2, 1, 3).reshape(B, S, D_FOX) * jax.nn.sigmoid(g)
    return o @ w_out


def conv_ffn(h, w_up, conv_w, w_down):
    S = h.shape[1]
    u = h @ w_up
    up = jnp.pad(u, ((0, 0), (CONV_WIDTH - 1, 0), (0, 0)))
    uc = sum(conv_w[j] * up[:, j:j + S] for j in range(CONV_WIDTH))
    gate, val = jnp.split(uc, 2, axis=-1)
    return (jax.nn.silu(gate) * val) @ w_down


def setup_inputs(seed: int = 0) -> dict:
    key = jax.random.key(seed)
    ks = jax.random.split(key, 15)
    f32 = jnp.float32

    def nrm(k, shape, scale):
        return jax.random.normal(k, shape, f32) * scale

    x = nrm(ks[0], (BATCH, SEQ, D_MODEL), 1.0)
    attn_norm_w = 1.0 + nrm(ks[1], (DEPTH, D_MODEL), 0.02)
    ffn_norm_w = 1.0 + nrm(ks[2], (DEPTH, D_MODEL), 0.02)
    final_norm_w = 1.0 + nrm(ks[3], (D_MODEL,), 0.02)
    even_w_in = nrm(ks[4], (N_EVEN, D_MODEL, EVEN_IN), D_MODEL ** -0.5)
    hgrn_lb_logits = nrm(ks[5], (N_EVEN, D_HGRN), 0.5)
    hgrn_norm_w = 1.0 + nrm(ks[6], (N_EVEN, D_HGRN), 0.02)
    ret_norm_w = 1.0 + nrm(ks[7], (N_EVEN, D_RET), 0.02)
    even_w_out = nrm(ks[8], (N_EVEN, D_HGRN + D_RET, D_MODEL), (D_HGRN + D_RET) ** -0.5)
    odd_w_in = nrm(ks[9], (N_ODD, D_MODEL, ODD_IN), D_MODEL ** -0.5)
    odd_b_f = FOX_FORGET_BIAS + nrm(ks[10], (N_ODD, H_FOX), 0.5)
    odd_w_out = nrm(ks[11], (N_ODD, D_FOX, D_MODEL), D_FOX ** -0.5)
    ffn_w_up = nrm(ks[12], (DEPTH, D_MODEL, 2 * D_FF), D_MODEL ** -0.5)
    conv_center = jnp.zeros((CONV_WIDTH, 2 * D_FF), f32).at[CONV_WIDTH - 1].set(1.0)
    ffn_conv_w = conv_center[None] + nrm(ks[13], (DEPTH, CONV_WIDTH, 2 * D_FF), 0.2)
    ffn_w_down = nrm(ks[14], (DEPTH, D_FF, D_MODEL), D_FF ** -0.5)
    return {'x': x, 'attn_norm_w': attn_norm_w, 'ffn_norm_w': ffn_norm_w,
            'final_norm_w': final_norm_w, 'even_w_in': even_w_in,
            'hgrn_lb_logits': hgrn_lb_logits, 'hgrn_norm_w': hgrn_norm_w,
            'ret_norm_w': ret_norm_w, 'even_w_out': even_w_out, 'odd_w_in': odd_w_in,
            'odd_b_f': odd_b_f, 'odd_w_out': odd_w_out, 'ffn_w_up': ffn_w_up,
            'ffn_conv_w': ffn_conv_w, 'ffn_w_down': ffn_w_down}


def reference(x, attn_norm_w, ffn_norm_w, final_norm_w, even_w_in, hgrn_lb_logits,
              hgrn_norm_w, ret_norm_w, even_w_out, odd_w_in, odd_b_f, odd_w_out,
              ffn_w_up, ffn_conv_w, ffn_w_down):
    S = x.shape[1]
    positions = jnp.arange(S)
    sm = jax.nn.softmax(hgrn_lb_logits.astype(jnp.float32), axis=0)
    lower_bounds = jnp.cumsum(sm, axis=0) - sm[0:1]
    for layer in range(DEPTH):
        j = layer // 2
        h = rmsnorm(x, attn_norm_w[layer])
        if layer % 2 == 0:
            x = x + even_mixer(h, even_w_in[j], lower_bounds[j], hgrn_norm_w[j],
                               ret_norm_w[j], even_w_out[j], positions)
        else:
            x = x + fox_mixer(h, odd_w_in[j], odd_b_f[j], odd_w_out[j])
        h = rmsnorm(x, ffn_norm_w[layer])
        x = x + conv_ffn(h, ffn_w_up[layer], ffn_conv_w[layer], ffn_w_down[layer])
    return rmsnorm(x, final_norm_w)
```

```python
import functools
import math

import jax
import jax.numpy as jnp
from jax import lax
from jax.experimental import pallas as pl
from jax.experimental.pallas import tpu as pltpu

F32 = jnp.float32
BF16 = jnp.bfloat16

HEAD_DIM = 128
ROPE_BASE = 10000.0
RET_DECAY_BASE = 5.0
EPS = 1e-6
LB_FLOOR = 1e-30
MASK_VALUE = -1e30
CONV_WIDTH = 3

LANES = 128
SUBLANES = 8
VMEM_LIMIT_BYTES = 48 * 1024 * 1024

NORM_ROWS = 256
MM_ROWS = 1024
MM_COLS = 512
FFN_COLS = 512
MIX_ROWS = 512
HGRN_CHUNK = 64
HGRN_SUB = 16
RET_CHUNK = 128
FOX_BLOCK = 512
SCAN_ROWS = 256

NT_DIMS = (((1,), (1,)), ((), ()))
TN_DIMS = (((0,), (0,)), ((), ()))


def _params(*semantics):
    return pltpu.CompilerParams(dimension_semantics=semantics, vmem_limit_bytes=VMEM_LIMIT_BYTES)


def _sigmoid(x):
    return 1.0 / (1.0 + jnp.exp(-x))


def _silu(x):
    return x * _sigmoid(x)


def _log_sigmoid(x):
    return jnp.minimum(x, 0.0) - jnp.log1p(jnp.exp(-jnp.abs(x)))


def _lower_tri(n):
    rows = lax.broadcasted_iota(jnp.int32, (n, n), 0)
    cols = lax.broadcasted_iota(jnp.int32, (n, n), 1)
    return (rows >= cols).astype(F32)


def _rmsnorm_kernel(x_ref, w_ref, o_ref):
    x = x_ref[...]
    ms = jnp.mean(x * x, axis=-1, keepdims=True)
    o_ref[...] = (x * lax.rsqrt(ms + EPS) * w_ref[...]).astype(o_ref.dtype)


def _rmsnorm(x, w, out_dtype):
    t, d = x.shape
    return pl.pallas_call(
        _rmsnorm_kernel,
        grid=(t // NORM_ROWS,),
        in_specs=[pl.BlockSpec((NORM_ROWS, d), lambda i: (i, 0)),
                  pl.BlockSpec((1, d), lambda i: (0, 0))],
        out_specs=pl.BlockSpec((NORM_ROWS, d), lambda i: (i, 0)),
        out_shape=jax.ShapeDtypeStruct((t, d), out_dtype),
        compiler_params=_params("parallel"),
        name="rmsnorm",
    )(x, w.reshape(1, d))


def _matmul_kernel(x_ref, w_ref, o_ref):
    o_ref[...] = jnp.dot(x_ref[...], w_ref[...], preferred_element_type=F32).astype(o_ref.dtype)


def _matmul_residual_kernel(x_ref, w_ref, r_ref, o_ref):
    o_ref[...] = r_ref[...] + jnp.dot(x_ref[...], w_ref[...], preferred_element_type=F32)


def _matmul(x, w, out_dtype, residual=None, rows=MM_ROWS, cols=MM_COLS, name="matmul"):
    t, k = x.shape
    n = w.shape[1]
    rows, cols = min(rows, t), min(cols, n)
    assert t % rows == 0 and n % cols == 0
    in_specs = [pl.BlockSpec((rows, k), lambda j, i: (i, 0)),
                pl.BlockSpec((k, cols), lambda j, i: (0, j))]
    args = [x, w]
    body = _matmul_kernel
    if residual is not None:
        in_specs.append(pl.BlockSpec((rows, cols), lambda j, i: (i, j)))
        args.append(residual)
        body = _matmul_residual_kernel
    return pl.pallas_call(
        body,
        grid=(n // cols, t // rows),
        in_specs=in_specs,
        out_specs=pl.BlockSpec((rows, cols), lambda j, i: (i, j)),
        out_shape=jax.ShapeDtypeStruct((t, n), out_dtype),
        compiler_params=_params("parallel", "parallel"),
        name=name,
    )(*args)


def _hgrn_kernel(q_ref, f_ref, i_ref, g_ref, lbl_ref, nw_ref, o_ref, state_ref, *, layer_j):
    C, L = HGRN_CHUNK, HGRN_SUB

    @pl.when(pl.program_id(2) == 0)
    def _():
        state_ref[...] = jnp.zeros_like(state_ref)

    lbl = lbl_ref[...]
    e = jnp.exp(lbl - jnp.max(lbl, axis=0, keepdims=True))
    sm = e / jnp.sum(e, axis=0, keepdims=True)
    lb = jnp.zeros((1, HEAD_DIM), F32)
    for r in range(1, layer_j + 1):
        lb = lb + sm[r:r + 1, :]
    log_lb = jnp.log(jnp.maximum(lb, LB_FLOOR))
    log_1m_lb = jnp.log1p(-lb)
    norm_w = nw_ref[...]

    tri = _lower_tri(C)
    ones_dk = jnp.ones((HEAD_DIM, C), BF16)
    row64 = lax.broadcasted_iota(jnp.int32, (C, HEAD_DIM), 0)
    sub_row = lax.broadcasted_iota(jnp.int32, (L, C), 0)
    sub_col = lax.broadcasted_iota(jnp.int32, (L, C), 1)

    def chunk(c, carry):
        r0 = pl.multiple_of(c * C, C)
        rows = pl.ds(r0, C)
        z = f_ref[0, rows, :]
        q = _silu(q_ref[0, rows, :])
        v = i_ref[0, rows, :].astype(BF16)

        cand = log_1m_lb + _log_sigmoid(z)
        log_f = jnp.maximum(log_lb, cand) + jnp.log1p(jnp.exp(-jnp.abs(log_lb - cand)))
        k = (1.0 - lb) * _sigmoid(-z)
        b = jnp.dot(tri, log_f, precision=lax.Precision.HIGHEST, preferred_element_type=F32)
        b_last = b[C - 1:C, :]

        state = state_ref[...]
        o = lax.dot_general((q * jnp.exp(b)).astype(BF16), state.astype(BF16), NT_DIMS,
                            preferred_element_type=F32)

        a_rows = []
        for i in range(C // L):
            lo = i * L
            qi, bi = q[lo:lo + L, :], b[lo:lo + L, :]
            if i == 0:
                a_i = jnp.zeros((L, C), F32)
            else:
                edge = b[lo - 1:lo, :]
                q_edge = (qi * jnp.exp(bi - edge)).astype(BF16)
                k_edge = jnp.where(row64 < lo, k * jnp.exp(jnp.minimum(edge - b, 0.0)), 0.0).astype(BF16)
                a_i = lax.dot_general(q_edge, k_edge, NT_DIMS, preferred_element_type=F32)
            pair = [(qi * k[lo + s:lo + s + 1, :]
                     * jnp.exp(jnp.minimum(bi - b[lo + s:lo + s + 1, :], 0.0))).astype(BF16)
                    for s in range(L)]
            pair_sum = jnp.dot(jnp.concatenate(pair, axis=0), ones_dk, preferred_element_type=F32)
            for s in range(L):
                hit = (sub_col == lo + s) & (sub_row >= s)
                a_i = jnp.where(hit, pair_sum[s * L:(s + 1) * L, :], a_i)
            a_rows.append(a_i)
        scores = jnp.concatenate(a_rows, axis=0).astype(BF16)
        o = o + jnp.dot(scores, v, preferred_element_type=F32)

        k_end = (k * jnp.exp(b_last - b)).astype(BF16)
        state_ref[...] = state * jnp.exp(b_last) + lax.dot_general(v, k_end, TN_DIMS,
                                                                   preferred_element_type=F32)

        y = o * lax.rsqrt(jnp.mean(o * o, axis=-1, keepdims=True) + EPS) * norm_w
        o_ref[0, rows, :] = (y * _silu(g_ref[0, rows, :])).astype(o_ref.dtype)
        return carry

    lax.fori_loop(0, MIX_ROWS // C, chunk, 0)


def _hgrn(proj, lb_logits, norm_w, layer_j):
    bsz, seq, _ = proj.shape
    d_h = norm_w.shape[0]
    n_heads = d_h // HEAD_DIM
    n_layers = lb_logits.shape[0]

    def col(group):
        return pl.BlockSpec((1, MIX_ROWS, HEAD_DIM), lambda b, h, s: (b, s, group * n_heads + h))

    return pl.pallas_call(
        functools.partial(_hgrn_kernel, layer_j=layer_j),
        grid=(bsz, n_heads, seq // MIX_ROWS),
        in_specs=[col(0), col(1), col(2), col(3),
                  pl.BlockSpec((n_layers, HEAD_DIM), lambda b, h, s: (0, h)),
                  pl.BlockSpec((1, HEAD_DIM), lambda b, h, s: (0, h))],
        out_specs=pl.BlockSpec((1, MIX_ROWS, HEAD_DIM), lambda b, h, s: (b, s, h)),
        out_shape=jax.ShapeDtypeStruct((bsz, seq, d_h), BF16),
        scratch_shapes=[pltpu.VMEM((HEAD_DIM, HEAD_DIM), F32)],
        compiler_params=_params("parallel", "parallel", "arbitrary"),
        name="hgrn2",
    )(proj, proj, proj, proj, lb_logits, norm_w.reshape(1, d_h))


def _retention_kernel(q_ref, k_ref, v_ref, g_ref, cos_ref, sin_ref, lg_ref, nw_ref, o_ref, state_ref):
    C = RET_CHUNK

    @pl.when(pl.program_id(2) == 0)
    def _():
        state_ref[...] = jnp.zeros_like(state_ref)

    lg = lg_ref[0]
    rows = lax.broadcasted_iota(jnp.int32, (C, C), 0)
    cols = lax.broadcasted_iota(jnp.int32, (C, C), 1)
    rel = (rows - cols).astype(F32)
    intra = jnp.where(rel >= 0, jnp.exp(lg * jnp.maximum(rel, 0.0)), 0.0)
    pos = lax.broadcasted_iota(jnp.int32, (C, HEAD_DIM), 0).astype(F32)
    w_start = jnp.exp(lg * (pos + 1.0))
    w_end = jnp.exp(lg * (C - 1.0 - pos))
    decay_chunk = jnp.exp(lg * C)
    norm_w = nw_ref[...]

    def rope(x, cos, sin_signed):
        return x * cos + pltpu.roll(x, HEAD_DIM // 2, 1) * sin_signed

    def chunk(c, carry):
        r0 = pl.multiple_of(c * C, C)
        sl = pl.ds(r0, C)
        cos, sin_signed = cos_ref[sl, :], sin_ref[sl, :]
        q = rope(q_ref[0, sl, :], cos, sin_signed)
        k = rope(k_ref[0, sl, :], cos, sin_signed) * (HEAD_DIM ** -0.5)
        v = v_ref[0, sl, :].astype(BF16)
        state = state_ref[...]

        scores = lax.dot_general(q.astype(BF16), k.astype(BF16), NT_DIMS, preferred_element_type=F32) * intra
        o = jnp.dot(scores.astype(BF16), v, preferred_element_type=F32)
        o = o + lax.dot_general((q * w_start).astype(BF16), state.astype(BF16), NT_DIMS,
                                preferred_element_type=F32)
        state_ref[...] = decay_chunk * state + lax.dot_general(v, (k * w_end).astype(BF16), TN_DIMS,
                                                               preferred_element_type=F32)

        cen = o - jnp.mean(o, axis=-1, keepdims=True)
        y = cen * lax.rsqrt(jnp.mean(cen * cen, axis=-1, keepdims=True) + EPS) * norm_w
        o_ref[0, sl, :] = (y * _silu(g_ref[0, sl, :])).astype(o_ref.dtype)
        return carry

    lax.fori_loop(0, MIX_ROWS // C, chunk, 0)


def _retention(proj, norm_w, first_group):
    bsz, seq, _ = proj.shape
    d_r = norm_w.shape[0]
    n_heads = d_r // HEAD_DIM
    half = HEAD_DIM // 2

    inv = ROPE_BASE ** (-jnp.arange(0, HEAD_DIM, 2, dtype=F32) / HEAD_DIM)
    ang = jnp.arange(seq).astype(F32)[:, None] * inv[None, :]
    cos = jnp.concatenate([jnp.cos(ang), jnp.cos(ang)], axis=-1)
    sin_signed = jnp.concatenate([-jnp.sin(ang), jnp.sin(ang)], axis=-1)
    assert cos.shape == (seq, 2 * half)
    log_gamma = jnp.log1p(-jnp.exp2(-RET_DECAY_BASE - jnp.arange(n_heads, dtype=F32)))
    lg = jnp.broadcast_to(log_gamma[:, None, None], (n_heads, 1, HEAD_DIM))

    def col(group):
        return pl.BlockSpec((1, MIX_ROWS, HEAD_DIM),
                            lambda b, h, s: (b, s, (first_group + group) * n_heads + h))

    table = pl.BlockSpec((MIX_ROWS, HEAD_DIM), lambda b, h, s: (s, 0))
    return pl.pallas_call(
        _retention_kernel,
        grid=(bsz, n_heads, seq // MIX_ROWS),
        in_specs=[col(0), col(1), col(2), col(3), table, table,
                  pl.BlockSpec((1, 1, HEAD_DIM), lambda b, h, s: (h, 0, 0)),
                  pl.BlockSpec((1, HEAD_DIM), lambda b, h, s: (0, h))],
        out_specs=pl.BlockSpec((1, MIX_ROWS, HEAD_DIM), lambda b, h, s: (b, s, h)),
        out_shape=jax.ShapeDtypeStruct((bsz, seq, d_r), BF16),
        scratch_shapes=[pltpu.VMEM((HEAD_DIM, HEAD_DIM), F32)],
        compiler_params=_params("parallel", "parallel", "arbitrary"),
        name="retention",
    )(proj, proj, proj, proj, cos, sin_signed, lg, norm_w.reshape(1, d_r))


def _forget_scan_kernel(f_ref, bias_ref, c_ref, carry_ref):
    @pl.when(pl.program_id(1) == 0)
    def _():
        carry_ref[...] = jnp.zeros_like(carry_ref)

    log_f = _log_sigmoid(f_ref[0] + bias_ref[...])
    c = jnp.dot(_lower_tri(SCAN_ROWS), log_f, precision=lax.Precision.HIGHEST,
                preferred_element_type=F32) + carry_ref[...]
    c_ref[0] = c
    carry_ref[...] = c[SCAN_ROWS - 1:SCAN_ROWS, :]


def _forget_scan(f, bias):
    bsz, seq, lanes = f.shape
    return pl.pallas_call(
        _forget_scan_kernel,
        grid=(bsz, seq // SCAN_ROWS),
        in_specs=[pl.BlockSpec((1, SCAN_ROWS, lanes), lambda b, s: (b, s, 0)),
                  pl.BlockSpec((1, lanes), lambda b, s: (0, 0))],
        out_specs=pl.BlockSpec((1, SCAN_ROWS, lanes), lambda b, s: (b, s, 0)),
        out_shape=jax.ShapeDtypeStruct((bsz, seq, lanes), F32),
        scratch_shapes=[pltpu.VMEM((1, lanes), F32)],
        compiler_params=_params("parallel", "arbitrary"),
        name="forget_scan",
    )(f, bias)


def _fox_kernel(q_ref, k_ref, v_ref, g_ref, c_ref, ct_ref, o_ref):
    T = FOX_BLOCK
    h = pl.program_id(1)
    qi = pl.program_id(2)

    q = (q_ref[0].astype(F32) * (HEAD_DIM ** -0.5)).astype(BF16)
    lane = lax.broadcasted_iota(jnp.int32, (T, LANES), 1)
    c_q = jnp.sum(jnp.where(lane == h, c_ref[0], 0.0), axis=-1, keepdims=True)

    def step(j, carry, diagonal):
        m, l, acc = carry
        sl = pl.ds(pl.multiple_of(j * T, T), T)
        logits = lax.dot_general(q, k_ref[0, sl, :], NT_DIMS, preferred_element_type=F32)
        logits = logits + (c_q - ct_ref[0, 0, pl.ds(j, 1), :])
        if diagonal:
            rows = lax.broadcasted_iota(jnp.int32, (T, T), 0)
            cols = lax.broadcasted_iota(jnp.int32, (T, T), 1)
            logits = jnp.where(rows >= cols, logits, MASK_VALUE)
        m_new = jnp.maximum(m, jnp.max(logits, axis=-1, keepdims=True))
        alpha = jnp.exp(m - m_new)
        p = jnp.exp(logits - m_new)
        l = alpha * l + jnp.sum(p, axis=-1, keepdims=True)
        acc = alpha * acc + jnp.dot(p.astype(BF16), v_ref[0, sl, :], preferred_element_type=F32)
        return m_new, l, acc

    init = (jnp.full((T, 1), MASK_VALUE, F32), jnp.zeros((T, 1), F32), jnp.zeros((T, HEAD_DIM), F32))
    carry = lax.fori_loop(0, qi, functools.partial(step, diagonal=False), init)
    _, l, acc = step(qi, carry, diagonal=True)
    o_ref[0] = (acc / l * _sigmoid(g_ref[0].astype(F32))).astype(o_ref.dtype)


def _fox(proj, c, n_heads):
    bsz, seq, _ = proj.shape
    T = FOX_BLOCK
    n_blocks = seq // T
    c_keys = jnp.transpose(c[:, :, :n_heads], (0, 2, 1)).reshape(bsz, n_heads, n_blocks, T)

    def q_side(group):
        return pl.BlockSpec((1, T, HEAD_DIM), lambda b, h, i: (b, i, group * n_heads + h))

    def kv_side(group):
        return pl.BlockSpec((1, seq, HEAD_DIM), lambda b, h, i: (b, 0, group * n_heads + h))

    return pl.pallas_call(
        _fox_kernel,
        grid=(bsz, n_heads, n_blocks),
        in_specs=[q_side(0), kv_side(1), kv_side(2), q_side(3),
                  pl.BlockSpec((1, T, LANES), lambda b, h, i: (b, i, 0)),
                  pl.BlockSpec((1, 1, n_blocks, T), lambda b, h, i: (b, h, 0, 0))],
        out_specs=pl.BlockSpec((1, T, HEAD_DIM), lambda b, h, i: (b, i, h)),
        out_shape=jax.ShapeDtypeStruct((bsz, seq, n_heads * HEAD_DIM), BF16),
        compiler_params=_params("parallel", "parallel", "arbitrary"),
        name="fox_attention",
    )(proj, proj, proj, proj, c, c_keys)


def _ffn_up_kernel(x_ref, wg_ref, wv_ref, cg_ref, cv_ref, o_ref, ug_ref, uv_ref, *, seq_steps):
    rows = x_ref.shape[0]

    @pl.when(pl.program_id(1) % seq_steps == 0)
    def _():
        ug_ref[0:SUBLANES, :] = jnp.zeros((SUBLANES, ug_ref.shape[1]), F32)
        uv_ref[0:SUBLANES, :] = jnp.zeros((SUBLANES, uv_ref.shape[1]), F32)

    x = x_ref[...]

    def conv(w_ref, c_ref, u_ref):
        u_ref[SUBLANES:SUBLANES + rows, :] = jnp.dot(x, w_ref[...], preferred_element_type=F32)
        out = c_ref[CONV_WIDTH - 1:CONV_WIDTH, :] * u_ref[SUBLANES:SUBLANES + rows, :]
        for tap in range(CONV_WIDTH - 1):
            back = CONV_WIDTH - 1 - tap
            out = out + c_ref[tap:tap + 1, :] * u_ref[SUBLANES - back:SUBLANES - back + rows, :]
        carry = u_ref[rows:rows + SUBLANES, :]
        u_ref[0:SUBLANES, :] = carry
        return out

    gate = conv(wg_ref, cg_ref, ug_ref)
    val = conv(wv_ref, cv_ref, uv_ref)
    o_ref[...] = (_silu(gate) * val).astype(o_ref.dtype)


def _ffn_up(h, w_up, conv_w, seq):
    t, d = h.shape
    f = w_up.shape[1] // 2
    rows = min(MM_ROWS, seq)
    assert seq % rows == 0 and f % FFN_COLS == 0
    n_cols = f // FFN_COLS
    return pl.pallas_call(
        functools.partial(_ffn_up_kernel, seq_steps=seq // rows),
        grid=(n_cols, t // rows),
        in_specs=[pl.BlockSpec((rows, d), lambda j, i: (i, 0)),
                  pl.BlockSpec((d, FFN_COLS), lambda j, i: (0, j)),
                  pl.BlockSpec((d, FFN_COLS), lambda j, i: (0, j + n_cols)),
                  pl.BlockSpec((CONV_WIDTH, FFN_COLS), lambda j, i: (0, j)),
                  pl.BlockSpec((CONV_WIDTH, FFN_COLS), lambda j, i: (0, j + n_cols))],
        out_specs=pl.BlockSpec((rows, FFN_COLS), lambda j, i: (i, j)),
        out_shape=jax.ShapeDtypeStruct((t, f), BF16),
        scratch_shapes=[pltpu.VMEM((rows + SUBLANES, FFN_COLS), F32),
                        pltpu.VMEM((rows + SUBLANES, FFN_COLS), F32)],
        compiler_params=_params("parallel", "arbitrary"),
        name="ffn_up_conv",
    )(h, w_up, w_up, conv_w, conv_w)


def _pad_cols(w, width):
    return jnp.pad(w, ((0, 0), (0, width - w.shape[1])))


def kernel(x, attn_norm_w, ffn_norm_w, final_norm_w, even_w_in, hgrn_lb_logits, hgrn_norm_w, ret_norm_w,
           even_w_out, odd_w_in, odd_b_f, odd_w_out, ffn_w_up, ffn_conv_w, ffn_w_down):
    bsz, seq, d_model = x.shape
    depth = attn_norm_w.shape[0]
    tokens = bsz * seq
    d_hgrn = hgrn_norm_w.shape[1]
    d_fox = odd_w_out.shape[1]
    n_fox = odd_b_f.shape[1]
    d_ff = ffn_w_down.shape[1]
    d_ff_pad = FFN_COLS * math.ceil(d_ff / FFN_COLS)

    xf = x.reshape(tokens, d_model)
    for layer in range(depth):
        j = layer // 2
        h = _rmsnorm(xf, attn_norm_w[layer], BF16)
        if layer % 2 == 0:
            proj = _matmul(h, even_w_in[j].astype(BF16), F32, name="even_in").reshape(bsz, seq, -1)
            o_a = _hgrn(proj, hgrn_lb_logits, hgrn_norm_w[j], j)
            o_b = _retention(proj, ret_norm_w[j], first_group=4 * d_hgrn // ret_norm_w.shape[1])
            o = jnp.concatenate([o_a, o_b], axis=-1).reshape(tokens, -1)
            xf = _matmul(o, even_w_out[j].astype(BF16), F32, residual=xf, name="even_out")
        else:
            w_in = odd_w_in[j]
            proj = _matmul(h, w_in[:, :4 * d_fox].astype(BF16), BF16, name="fox_in").reshape(bsz, seq, -1)
            f = _matmul(h, _pad_cols(w_in[:, 4 * d_fox:], LANES).astype(BF16), F32, name="fox_forget")
            c = _forget_scan(f.reshape(bsz, seq, LANES), _pad_cols(odd_b_f[j][None, :], LANES))
            o = _fox(proj, c, n_fox).reshape(tokens, -1)
            xf = _matmul(o, odd_w_out[j].astype(BF16), F32, residual=xf, name="fox_out")
        h = _rmsnorm(xf, ffn_norm_w[layer], BF16)
        w_up = ffn_w_up[layer]
        w_up = jnp.concatenate([_pad_cols(w_up[:, :d_ff], d_ff_pad), _pad_cols(w_up[:, d_ff:], d_ff_pad)],
                               axis=1).astype(BF16)
        conv_w = ffn_conv_w[layer]
        conv_w = jnp.concatenate([_pad_cols(conv_w[:, :d_ff], d_ff_pad), _pad_cols(conv_w[:, d_ff:], d_ff_pad)],
                                 axis=1)
        act = _ffn_up(h, w_up, conv_w, seq)
        w_down = jnp.pad(ffn_w_down[layer], ((0, d_ff_pad - d_ff), (0, 0))).astype(BF16)
        xf = _matmul(act, w_down, F32, residual=xf, rows=512, name="ffn_down")
    return _rmsnorm(xf, final_norm_w, x.dtype).reshape(bsz, seq, d_model)
```

```python
import functools
import math

import jax
import jax.numpy as jnp
from jax import lax
from jax.experimental import pallas as pl
from jax.experimental.pallas import tpu as pltpu

F32 = jnp.float32
BF16 = jnp.bfloat16

HEAD_DIM = 128
ROPE_BASE = 10000.0
RET_DECAY_BASE = 5.0
EPS = 1e-6
LB_FLOOR = 1e-30
MASK_VALUE = -1e30
CONV_WIDTH = 3
LOG2_E = math.log2(math.e)

LANES = 128
SUBLANES = 8
VMEM_LIMIT_BYTES = 48 * 1024 * 1024

NORM_ROWS = 256
MM_ROWS = 1024
MM_COLS = 512
FFN_COLS = 512
MIX_ROWS = 512
MIX_UNROLL = 2
HGRN_CHUNK = 64
HGRN_SUB = 16
HGRN_HEADS_PER_STEP = 8
RET_CHUNK = 128
RET_HEADS_PER_STEP = 4
FOX_BLOCK = 512
SCAN_ROWS = 256

NT_DIMS = (((1,), (1,)), ((), ()))
TN_DIMS = (((0,), (0,)), ((), ()))


def _params(*semantics):
    return pltpu.CompilerParams(dimension_semantics=semantics, vmem_limit_bytes=VMEM_LIMIT_BYTES)


def _sigmoid(x):
    return 1.0 / (1.0 + jnp.exp(-x))


def _silu(x):
    return x * _sigmoid(x)


def _log_sigmoid(x):
    return jnp.minimum(x, 0.0) - jnp.log1p(jnp.exp(-jnp.abs(x)))


def _lower_tri(n):
    rows = lax.broadcasted_iota(jnp.int32, (n, n), 0)
    cols = lax.broadcasted_iota(jnp.int32, (n, n), 1)
    return (rows >= cols).astype(F32)


def _rmsnorm_kernel(x_ref, w_ref, o_ref):
    x = x_ref[...]
    ms = jnp.mean(x * x, axis=-1, keepdims=True)
    o_ref[...] = (x * lax.rsqrt(ms + EPS) * w_ref[...]).astype(o_ref.dtype)


def _rmsnorm(x, w, out_dtype):
    t, d = x.shape
    return pl.pallas_call(
        _rmsnorm_kernel,
        grid=(t // NORM_ROWS,),
        in_specs=[pl.BlockSpec((NORM_ROWS, d), lambda i: (i, 0)),
                  pl.BlockSpec((1, d), lambda i: (0, 0))],
        out_specs=pl.BlockSpec((NORM_ROWS, d), lambda i: (i, 0)),
        out_shape=jax.ShapeDtypeStruct((t, d), out_dtype),
        compiler_params=_params("parallel"),
        name="rmsnorm",
    )(x, w.reshape(1, d))


def _matmul_kernel(x_ref, w_ref, o_ref):
    o_ref[...] = jnp.dot(x_ref[...], w_ref[...], preferred_element_type=F32).astype(o_ref.dtype)


def _matmul_residual_kernel(x_ref, w_ref, r_ref, o_ref):
    o_ref[...] = r_ref[...] + jnp.dot(x_ref[...], w_ref[...], preferred_element_type=F32)


def _matmul(x, w, out_dtype, residual=None, rows=MM_ROWS, cols=MM_COLS, name="matmul"):
    t, k = x.shape
    n = w.shape[1]
    rows, cols = min(rows, t), min(cols, n)
    assert t % rows == 0 and n % cols == 0
    in_specs = [pl.BlockSpec((rows, k), lambda j, i: (i, 0)),
                pl.BlockSpec((k, cols), lambda j, i: (0, j))]
    args = [x, w]
    body = _matmul_kernel
    if residual is not None:
        in_specs.append(pl.BlockSpec((rows, cols), lambda j, i: (i, j)))
        args.append(residual)
        body = _matmul_residual_kernel
    return pl.pallas_call(
        body,
        grid=(n // cols, t // rows),
        in_specs=in_specs,
        out_specs=pl.BlockSpec((rows, cols), lambda j, i: (i, j)),
        out_shape=jax.ShapeDtypeStruct((t, n), out_dtype),
        compiler_params=_params("parallel", "parallel"),
        name=name,
    )(*args)


def _hgrn_kernel(q_ref, f_ref, i_ref, g_ref, lbl_ref, nw_ref, tri3_ref, sel_ref, mask_ref, o_ref, state_ref,
                 *, layer_j):
    C, L = HGRN_CHUNK, HGRN_SUB
    n_sub = C // L

    @pl.when(pl.program_id(2) == 0)
    def _():
        state_ref[...] = jnp.zeros_like(state_ref)

    lbl = lbl_ref[...]
    e = jnp.exp(lbl - jnp.max(lbl, axis=0, keepdims=True))
    sm = e / jnp.sum(e, axis=0, keepdims=True)
    lb_all = jnp.zeros((1, lbl.shape[1]), F32)
    for r in range(1, layer_j + 1):
        lb_all = lb_all + sm[r:r + 1, :]
    log_lb_all = jnp.log(jnp.maximum(lb_all, LB_FLOOR))
    log_1m_lb_all = jnp.log1p(-lb_all)

    def pair_products(q, k, b, s):
        first = (s // SUBLANES) * SUBLANES
        q3, k3, b3 = (x.reshape(n_sub, L, HEAD_DIM) for x in (q, k, b))
        decay = jnp.exp2(jnp.minimum(b3[:, first:, :] - b3[:, s:s + 1, :], 0.0))
        prod = q3[:, first:, :] * k3[:, s:s + 1, :] * decay
        if first:
            prod = jnp.concatenate([jnp.zeros((n_sub, first, HEAD_DIM), F32), prod], axis=1)
        return prod.reshape(C, HEAD_DIM).astype(BF16)

    def head_chunk(rows, lanes, head):
        lb, log_lb, log_1m_lb = lb_all[:, lanes], log_lb_all[:, lanes], log_1m_lb_all[:, lanes]
        z = f_ref[0, rows, lanes]
        q = _silu(q_ref[0, rows, lanes])
        v = i_ref[0, rows, lanes].astype(BF16)

        e_z = jnp.exp(-jnp.abs(z))
        cand = log_1m_lb + (jnp.minimum(z, 0.0) - jnp.log1p(e_z))
        log_f = jnp.maximum(log_lb, cand) + jnp.log1p(jnp.exp(-jnp.abs(log_lb - cand)))
        k = (1.0 - lb) * (jnp.where(z >= 0.0, e_z, 1.0) / (1.0 + e_z))
        hi = log_f.astype(BF16)
        rest = log_f - hi.astype(F32)
        mid = rest.astype(BF16)
        low = (rest - mid.astype(F32)).astype(BF16)
        b = jnp.dot(tri3_ref[...], jnp.concatenate([hi, mid, low], axis=0),
                    preferred_element_type=F32) * LOG2_E
        b_last = b[C - 1:C, :]

        state = state_ref[head]
        o = lax.dot_general((q * jnp.exp2(b)).astype(BF16), state.astype(BF16), NT_DIMS,
                            preferred_element_type=F32)

        below = [jnp.zeros((L, C), F32)]
        for i in range(1, n_sub):
            lo = i * L
            edge = b[lo - 1:lo, :]
            q_edge = (q[lo:lo + L, :] * jnp.exp2(b[lo:lo + L, :] - edge)).astype(BF16)
            k_edge = (k[:lo, :] * jnp.exp2(edge - b[:lo, :])).astype(BF16)
            k_edge = jnp.concatenate([k_edge, jnp.zeros((C - lo, HEAD_DIM), BF16)], axis=0)
            below.append(lax.dot_general(q_edge, k_edge, NT_DIMS, preferred_element_type=F32))
        pair = jnp.concatenate([pair_products(q, k, b, s) for s in range(L)], axis=1)
        diag = jnp.dot(pair, sel_ref[...], preferred_element_type=F32)
        scores = jnp.concatenate(below, axis=0) + diag * mask_ref[...]
        o = o + jnp.dot(scores.astype(BF16), v, preferred_element_type=F32)

        k_end = (k * jnp.exp2(b_last - b)).astype(BF16)
        state_ref[head] = state * jnp.exp2(b_last) + lax.dot_general(v, k_end, TN_DIMS,
                                                                     preferred_element_type=F32)

        y = o * lax.rsqrt(jnp.mean(o * o, axis=-1, keepdims=True) + EPS) * nw_ref[:, lanes]
        o_ref[0, rows, lanes] = (y * _silu(g_ref[0, rows, lanes])).astype(o_ref.dtype)

    def chunk(c, carry):
        rows = pl.ds(pl.multiple_of(c * C, C), C)
        for head in range(HGRN_HEADS_PER_STEP):
            head_chunk(rows, slice(head * HEAD_DIM, (head + 1) * HEAD_DIM), head)
        return carry

    lax.fori_loop(0, MIX_ROWS // C, chunk, 0, unroll=MIX_UNROLL)


def _hgrn(proj, lb_logits, norm_w, layer_j):
    bsz, seq, _ = proj.shape
    d_h = norm_w.shape[0]
    width = HGRN_HEADS_PER_STEP * HEAD_DIM
    n_groups = d_h // width
    n_layers = lb_logits.shape[0]

    C, L = HGRN_CHUNK, HGRN_SUB
    pos = jnp.arange(C)
    causal = pos[:, None] >= pos[None, :]
    tri3 = jnp.tile(causal.astype(BF16), (1, 3))
    pair_sel = (jnp.repeat(jnp.arange(L), HEAD_DIM)[:, None] == (pos % L)[None, :]).astype(BF16)
    diag_mask = (causal & (pos[:, None] // L == pos[None, :] // L)).astype(F32)

    def col(group):
        return pl.BlockSpec((1, MIX_ROWS, width), lambda b, h, s: (b, s, group * n_groups + h))

    def whole(a):
        return pl.BlockSpec(a.shape, lambda b, h, s: (0, 0))

    return pl.pallas_call(
        functools.partial(_hgrn_kernel, layer_j=layer_j),
        grid=(bsz, n_groups, seq // MIX_ROWS),
        in_specs=[col(0), col(1), col(2), col(3),
                  pl.BlockSpec((n_layers, width), lambda b, h, s: (0, h)),
                  pl.BlockSpec((1, width), lambda b, h, s: (0, h)),
                  whole(tri3), whole(pair_sel), whole(diag_mask)],
        out_specs=pl.BlockSpec((1, MIX_ROWS, width), lambda b, h, s: (b, s, h)),
        out_shape=jax.ShapeDtypeStruct((bsz, seq, d_h), BF16),
        scratch_shapes=[pltpu.VMEM((HGRN_HEADS_PER_STEP, HEAD_DIM, HEAD_DIM), F32)],
        compiler_params=_params("parallel", "parallel", "arbitrary"),
        name="hgrn2",
    )(proj, proj, proj, proj, lb_logits, norm_w.reshape(1, d_h), tri3, pair_sel, diag_mask)


def _retention_kernel(q_ref, k_ref, v_ref, g_ref, cos_ref, sin_ref, intra_ref, edge_ref, nw_ref, o_ref, state_ref):
    C = RET_CHUNK

    @pl.when(pl.program_id(2) == 0)
    def _():
        state_ref[...] = jnp.zeros_like(state_ref)

    def rope(x, cos, sin_signed):
        return x * cos + pltpu.roll(x, HEAD_DIM // 2, 1) * sin_signed

    def head_chunk(sl, lanes, head):
        cos, sin_signed = cos_ref[sl, :], sin_ref[sl, :]
        q = rope(q_ref[0, sl, lanes], cos, sin_signed)
        k = rope(k_ref[0, sl, lanes], cos, sin_signed) * (HEAD_DIM ** -0.5)
        v = v_ref[0, sl, lanes].astype(BF16)
        state = state_ref[head]
        w_start, w_end, decay_chunk = edge_ref[head, 0], edge_ref[head, 1], edge_ref[head, 2, 0:1, :]

        scores = lax.dot_general(q.astype(BF16), k.astype(BF16), NT_DIMS,
                                 preferred_element_type=F32) * intra_ref[head]
        o = jnp.dot(scores.astype(BF16), v, preferred_element_type=F32)
        o = o + lax.dot_general((q * w_start).astype(BF16), state.astype(BF16), NT_DIMS,
                                preferred_element_type=F32)
        state_ref[head] = decay_chunk * state + lax.dot_general(v, (k * w_end).astype(BF16), TN_DIMS,
                                                                preferred_element_type=F32)

        cen = o - jnp.mean(o, axis=-1, keepdims=True)
        y = cen * lax.rsqrt(jnp.mean(cen * cen, axis=-1, keepdims=True) + EPS) * nw_ref[:, lanes]
        o_ref[0, sl, lanes] = (y * _silu(g_ref[0, sl, lanes])).astype(o_ref.dtype)

    def chunk(c, carry):
        sl = pl.ds(pl.multiple_of(c * C, C), C)
        for head in range(RET_HEADS_PER_STEP):
            head_chunk(sl, slice(head * HEAD_DIM, (head + 1) * HEAD_DIM), head)
        return carry

    lax.fori_loop(0, MIX_ROWS // C, chunk, 0, unroll=MIX_UNROLL)


def _retention(proj, norm_w, first_group):
    bsz, seq, _ = proj.shape
    d_r = norm_w.shape[0]
    n_heads = d_r // HEAD_DIM
    half = HEAD_DIM // 2

    inv = ROPE_BASE ** (-jnp.arange(0, HEAD_DIM, 2, dtype=F32) / HEAD_DIM)
    ang = jnp.arange(seq).astype(F32)[:, None] * inv[None, :]
    cos = jnp.concatenate([jnp.cos(ang), jnp.cos(ang)], axis=-1)
    sin_signed = jnp.concatenate([-jnp.sin(ang), jnp.sin(ang)], axis=-1)
    assert cos.shape == (seq, 2 * half)
    C = RET_CHUNK
    log_gamma = jnp.log1p(-jnp.exp2(-RET_DECAY_BASE - jnp.arange(n_heads, dtype=F32)))
    pos = jnp.arange(C, dtype=F32)
    rel = pos[:, None] - pos[None, :]
    intra = jnp.where(rel >= 0, jnp.exp(log_gamma[:, None, None] * jnp.maximum(rel, 0.0)), 0.0)
    w_start = jnp.exp(log_gamma[:, None] * (pos + 1.0)[None, :])
    w_end = jnp.exp(log_gamma[:, None] * (C - 1.0 - pos)[None, :])
    decay_chunk = jnp.broadcast_to(jnp.exp(log_gamma * C)[:, None], (n_heads, C))
    edge = jnp.broadcast_to(jnp.stack([w_start, w_end, decay_chunk], axis=1)[..., None],
                            (n_heads, 3, C, HEAD_DIM))

    per_step = RET_HEADS_PER_STEP
    width = per_step * HEAD_DIM
    n_groups = n_heads // per_step
    first = first_group * n_groups

    def col(group):
        return pl.BlockSpec((1, MIX_ROWS, width), lambda b, h, s: (b, s, first + group * n_groups + h))

    table = pl.BlockSpec((MIX_ROWS, HEAD_DIM), lambda b, h, s: (s, 0))
    return pl.pallas_call(
        _retention_kernel,
        grid=(bsz, n_groups, seq // MIX_ROWS),
        in_specs=[col(0), col(1), col(2), col(3), table, table,
                  pl.BlockSpec((per_step, C, C), lambda b, h, s: (h, 0, 0)),
                  pl.BlockSpec((per_step, 3, C, HEAD_DIM), lambda b, h, s: (h, 0, 0, 0)),
                  pl.BlockSpec((1, width), lambda b, h, s: (0, h))],
        out_specs=pl.BlockSpec((1, MIX_ROWS, width), lambda b, h, s: (b, s, h)),
        out_shape=jax.ShapeDtypeStruct((bsz, seq, d_r), BF16),
        scratch_shapes=[pltpu.VMEM((per_step, HEAD_DIM, HEAD_DIM), F32)],
        compiler_params=_params("parallel", "parallel", "arbitrary"),
        name="retention",
    )(proj, proj, proj, proj, cos, sin_signed, intra, edge, norm_w.reshape(1, d_r))


def _forget_scan_kernel(f_ref, bias_ref, c_ref, carry_ref):
    @pl.when(pl.program_id(1) == 0)
    def _():
        carry_ref[...] = jnp.zeros_like(carry_ref)

    log_f = _log_sigmoid(f_ref[0] + bias_ref[...])
    c = jnp.dot(_lower_tri(SCAN_ROWS), log_f, precision=lax.Precision.HIGHEST,
                preferred_element_type=F32) + carry_ref[...]
    c_ref[0] = c
    carry_ref[...] = c[SCAN_ROWS - 1:SCAN_ROWS, :]


def _forget_scan(f, bias):
    bsz, seq, lanes = f.shape
    return pl.pallas_call(
        _forget_scan_kernel,
        grid=(bsz, seq // SCAN_ROWS),
        in_specs=[pl.BlockSpec((1, SCAN_ROWS, lanes), lambda b, s: (b, s, 0)),
                  pl.BlockSpec((1, lanes), lambda b, s: (0, 0))],
        out_specs=pl.BlockSpec((1, SCAN_ROWS, lanes), lambda b, s: (b, s, 0)),
        out_shape=jax.ShapeDtypeStruct((bsz, seq, lanes), F32),
        scratch_shapes=[pltpu.VMEM((1, lanes), F32)],
        compiler_params=_params("parallel", "arbitrary"),
        name="forget_scan",
    )(f, bias)


def _fox_kernel(q_ref, k_ref, v_ref, g_ref, c_ref, ct_ref, o_ref,
                vones_ref, qs_ref, cq_ref, m_ref, acc_ref, s0_ref, s1_ref, p0_ref, p1_ref, a0_ref, a1_ref):
    T = FOX_BLOCK
    h = pl.program_id(1)
    qi = pl.program_id(2)

    @pl.when(qi == 0)
    def _():
        vones_ref[:, 0:HEAD_DIM] = v_ref[0]
        vones_ref[:, HEAD_DIM:2 * HEAD_DIM] = jnp.ones((v_ref.shape[1], HEAD_DIM), BF16)

    qs_ref[...] = (q_ref[0].astype(F32) * (HEAD_DIM ** -0.5 * LOG2_E)).astype(BF16)
    lane = lax.broadcasted_iota(jnp.int32, (T, LANES), 1)
    cq_ref[...] = jnp.sum(jnp.where(lane == h, c_ref[0], 0.0), axis=-1, keepdims=True) * LOG2_E
    m_ref[...] = jnp.full((T, 1), MASK_VALUE, F32)
    acc_ref[...] = jnp.zeros((T, 2 * HEAD_DIM), F32)

    def block_of(n):
        return jnp.where(n == 0, qi, jnp.minimum(n - 1, qi))

    def keys_of(n):
        return pl.ds(pl.multiple_of(block_of(n) * T, T), T)

    def form_logits(n, s_ref):
        c_k = ct_ref[0, 0, pl.ds(block_of(n), 1), :] * LOG2_E
        s_ref[...] = lax.dot_general(qs_ref[...], k_ref[0, keys_of(n), :], NT_DIMS,
                                     preferred_element_type=F32) - c_k

    def softmax_update(s, valid, p_ref, a_ref):
        m, c_q = m_ref[...], cq_ref[...]
        top = jnp.max(s, axis=-1, keepdims=True) + c_q
        m_new = jnp.maximum(m, jnp.where(valid, top, MASK_VALUE))
        p_ref[...] = jnp.exp2(s - jnp.where(valid, m_new - c_q, -MASK_VALUE)).astype(BF16)
        a_ref[...] = jnp.exp2(m - m_new)
        m_ref[...] = m_new

    def accumulate(n, p_ref, a_ref):
        acc_ref[...] = a_ref[...] * acc_ref[...] + jnp.dot(p_ref[...], vones_ref[keys_of(n), :],
                                                           preferred_element_type=F32)

    form_logits(0, s0_ref)
    form_logits(1, s1_ref)
    row_id = lax.broadcasted_iota(jnp.int32, (T, T), 0)
    col_id = lax.broadcasted_iota(jnp.int32, (T, T), 1)
    softmax_update(jnp.where(row_id >= col_id, s0_ref[...], MASK_VALUE), True, p0_ref, a0_ref)

    def two_ticks(i, carry):
        n = 2 + 2 * i
        form_logits(n, s0_ref)
        softmax_update(s1_ref[...], n - 1 <= qi, p1_ref, a1_ref)
        accumulate(n - 2, p0_ref, a0_ref)
        form_logits(n + 1, s1_ref)
        softmax_update(s0_ref[...], n <= qi, p0_ref, a0_ref)
        accumulate(n - 1, p1_ref, a1_ref)
        return carry

    lax.fori_loop(0, (qi + 2) // 2, two_ticks, 0)
    acc = acc_ref[...]
    o = acc[:, :HEAD_DIM] / acc[:, HEAD_DIM:]
    o_ref[0] = (o * _sigmoid(g_ref[0].astype(F32))).astype(o_ref.dtype)


def _fox(proj, c, n_heads):
    bsz, seq, _ = proj.shape
    T = FOX_BLOCK
    n_blocks = seq // T
    c_keys = jnp.transpose(c[:, :, :n_heads], (0, 2, 1)).reshape(bsz, n_heads, n_blocks, T)

    def q_side(group):
        return pl.BlockSpec((1, T, HEAD_DIM), lambda b, h, i: (b, i, group * n_heads + h))

    def kv_side(group):
        return pl.BlockSpec((1, seq, HEAD_DIM), lambda b, h, i: (b, 0, group * n_heads + h))

    return pl.pallas_call(
        _fox_kernel,
        grid=(bsz, n_heads, n_blocks),
        in_specs=[q_side(0), kv_side(1), kv_side(2), q_side(3),
                  pl.BlockSpec((1, T, LANES), lambda b, h, i: (b, i, 0)),
                  pl.BlockSpec((1, 1, n_blocks, T), lambda b, h, i: (b, h, 0, 0))],
        out_specs=pl.BlockSpec((1, T, HEAD_DIM), lambda b, h, i: (b, i, h)),
        out_shape=jax.ShapeDtypeStruct((bsz, seq, n_heads * HEAD_DIM), BF16),
        scratch_shapes=[pltpu.VMEM((seq, 2 * HEAD_DIM), BF16),
                        pltpu.VMEM((T, HEAD_DIM), BF16),
                        pltpu.VMEM((T, 1), F32),
                        pltpu.VMEM((T, 1), F32),
                        pltpu.VMEM((T, 2 * HEAD_DIM), F32),
                        pltpu.VMEM((T, T), F32), pltpu.VMEM((T, T), F32),
                        pltpu.VMEM((T, T), BF16), pltpu.VMEM((T, T), BF16),
                        pltpu.VMEM((T, 1), F32), pltpu.VMEM((T, 1), F32)],
        compiler_params=_params("parallel", "parallel", "arbitrary"),
        name="fox_attention",
    )(proj, proj, proj, proj, c, c_keys)


def _ffn_up_kernel(x_ref, wg_ref, wv_ref, cg_ref, cv_ref, o_ref, ug_ref, uv_ref, *, seq_steps):
    rows = x_ref.shape[0]

    @pl.when(pl.program_id(1) % seq_steps == 0)
    def _():
        ug_ref[0:SUBLANES, :] = jnp.zeros((SUBLANES, ug_ref.shape[1]), F32)
        uv_ref[0:SUBLANES, :] = jnp.zeros((SUBLANES, uv_ref.shape[1]), F32)

    x = x_ref[...]

    def conv(w_ref, c_ref, u_ref):
        u_ref[SUBLANES:SUBLANES + rows, :] = jnp.dot(x, w_ref[...], preferred_element_type=F32)
        out = c_ref[CONV_WIDTH - 1:CONV_WIDTH, :] * u_ref[SUBLANES:SUBLANES + rows, :]
        for tap in range(CONV_WIDTH - 1):
            back = CONV_WIDTH - 1 - tap
            out = out + c_ref[tap:tap + 1, :] * u_ref[SUBLANES - back:SUBLANES - back + rows, :]
        carry = u_ref[rows:rows + SUBLANES, :]
        u_ref[0:SUBLANES, :] = carry
        return out

    gate = conv(wg_ref, cg_ref, ug_ref)
    val = conv(wv_ref, cv_ref, uv_ref)
    o_ref[...] = (_silu(gate) * val).astype(o_ref.dtype)


def _ffn_up(h, w_up, conv_w, seq):
    t, d = h.shape
    f = w_up.shape[1] // 2
    rows = min(MM_ROWS, seq)
    assert seq % rows == 0 and f % FFN_COLS == 0
    n_cols = f // FFN_COLS
    return pl.pallas_call(
        functools.partial(_ffn_up_kernel, seq_steps=seq // rows),
        grid=(n_cols, t // rows),
        in_specs=[pl.BlockSpec((rows, d), lambda j, i: (i, 0)),
                  pl.BlockSpec((d, FFN_COLS), lambda j, i: (0, j)),
                  pl.BlockSpec((d, FFN_COLS), lambda j, i: (0, j + n_cols)),
                  pl.BlockSpec((CONV_WIDTH, FFN_COLS), lambda j, i: (0, j)),
                  pl.BlockSpec((CONV_WIDTH, FFN_COLS), lambda j, i: (0, j + n_cols))],
        out_specs=pl.BlockSpec((rows, FFN_COLS), lambda j, i: (i, j)),
        out_shape=jax.ShapeDtypeStruct((t, f), BF16),
        scratch_shapes=[pltpu.VMEM((rows + SUBLANES, FFN_COLS), F32),
                        pltpu.VMEM((rows + SUBLANES, FFN_COLS), F32)],
        compiler_params=_params("parallel", "arbitrary"),
        name="ffn_up_conv",
    )(h, w_up, w_up, conv_w, conv_w)


def _pad_cols(w, width):
    return jnp.pad(w, ((0, 0), (0, width - w.shape[1])))


def kernel(x, attn_norm_w, ffn_norm_w, final_norm_w, even_w_in, hgrn_lb_logits, hgrn_norm_w, ret_norm_w,
           even_w_out, odd_w_in, odd_b_f, odd_w_out, ffn_w_up, ffn_conv_w, ffn_w_down):
    bsz, seq, d_model = x.shape
    depth = attn_norm_w.shape[0]
    tokens = bsz * seq
    d_hgrn = hgrn_norm_w.shape[1]
    d_fox = odd_w_out.shape[1]
    n_fox = odd_b_f.shape[1]
    d_ff = ffn_w_down.shape[1]
    d_ff_pad = FFN_COLS * math.ceil(d_ff / FFN_COLS)

    xf = x.reshape(tokens, d_model)
    for layer in range(depth):
        j = layer // 2
        h = _rmsnorm(xf, attn_norm_w[layer], BF16)
        if layer % 2 == 0:
            proj = _matmul(h, even_w_in[j].astype(BF16), F32, name="even_in").reshape(bsz, seq, -1)
            o_a = _hgrn(proj, hgrn_lb_logits, hgrn_norm_w[j], j)
            o_b = _retention(proj, ret_norm_w[j], first_group=4 * d_hgrn // ret_norm_w.shape[1])
            o = jnp.concatenate([o_a, o_b], axis=-1).reshape(tokens, -1)
            xf = _matmul(o, even_w_out[j].astype(BF16), F32, residual=xf, name="even_out")
        else:
            w_in = odd_w_in[j]
            proj = _matmul(h, w_in[:, :4 * d_fox].astype(BF16), BF16, name="fox_in").reshape(bsz, seq, -1)
            f = _matmul(h, _pad_cols(w_in[:, 4 * d_fox:], LANES).astype(BF16), F32, name="fox_forget")
            c = _forget_scan(f.reshape(bsz, seq, LANES), _pad_cols(odd_b_f[j][None, :], LANES))
            o = _fox(proj, c, n_fox).reshape(tokens, -1)
            xf = _matmul(o, odd_w_out[j].astype(BF16), F32, residual=xf, name="fox_out")
        h = _rmsnorm(xf, ffn_norm_w[layer], BF16)
        w_up = ffn_w_up[layer]
        w_up = jnp.concatenate([_pad_cols(w_up[:, :d_ff], d_ff_pad), _pad_cols(w_up[:, d_ff:], d_ff_pad)],
                               axis=1).astype(BF16)
        conv_w = ffn_conv_w[layer]
        conv_w = jnp.concatenate([_pad_cols(conv_w[:, :d_ff], d_ff_pad), _pad_cols(conv_w[:, d_ff:], d_ff_pad)],
                                 axis=1)
        act = _ffn_up(h, w_up, conv_w, seq)
        w_down = jnp.pad(ffn_w_down[layer], ((0, d_ff_pad - d_ff), (0, 0))).astype(BF16)
        xf = _matmul(act, w_down, F32, residual=xf, rows=512, name="ffn_down")
    return _rmsnorm(xf, final_norm_w, x.dtype).reshape(bsz, seq, d_model)
```

```python
import functools
import math

import jax
import jax.numpy as jnp
from jax import lax
from jax.experimental import pallas as pl
from jax.experimental.pallas import tpu as pltpu

F32 = jnp.float32
BF16 = jnp.bfloat16

HEAD_DIM = 128
ROPE_BASE = 10000.0
RET_DECAY_BASE = 5.0
EPS = 1e-6
LB_FLOOR = 1e-30
MASK_VALUE = -1e30
CONV_WIDTH = 3
LOG2_E = math.log2(math.e)

LANES = 128
SUBLANES = 8
VMEM_LIMIT_BYTES = 48 * 1024 * 1024

NORM_ROWS = 256
MM_ROWS = 1024
MM_COLS = 512
IN_COLS = 1024
OUT_ROWS = 512
DOWN_ROWS = 256
FFN_COLS = 512
MIX_ROWS = 512
MIX_UNROLL = 2
HGRN_CHUNK = 64
HGRN_SUB = 16
HGRN_HEADS_PER_STEP = 8
RET_CHUNK = 128
RET_HEADS_PER_STEP = 4
FOX_BLOCK = 512
SCAN_ROWS = 256

NT_DIMS = (((1,), (1,)), ((), ()))
TN_DIMS = (((0,), (0,)), ((), ()))


def _params(*semantics):
    return pltpu.CompilerParams(dimension_semantics=semantics, vmem_limit_bytes=VMEM_LIMIT_BYTES)


def _sigmoid(x):
    return 1.0 / (1.0 + jnp.exp(-x))


def _silu(x):
    return x * _sigmoid(x)


def _log_sigmoid(x):
    return jnp.minimum(x, 0.0) - jnp.log1p(jnp.exp(-jnp.abs(x)))


def _lower_tri(n):
    rows = lax.broadcasted_iota(jnp.int32, (n, n), 0)
    cols = lax.broadcasted_iota(jnp.int32, (n, n), 1)
    return (rows >= cols).astype(F32)


def _rmsnorm_kernel(x_ref, w_ref, o_ref):
    x = x_ref[...]
    ms = jnp.mean(x * x, axis=-1, keepdims=True)
    o_ref[...] = (x * lax.rsqrt(ms + EPS) * w_ref[...]).astype(o_ref.dtype)


def _rmsnorm(x, w, out_dtype):
    t, d = x.shape
    return pl.pallas_call(
        _rmsnorm_kernel,
        grid=(t // NORM_ROWS,),
        in_specs=[pl.BlockSpec((NORM_ROWS, d), lambda i: (i, 0)),
                  pl.BlockSpec((1, d), lambda i: (0, 0))],
        out_specs=pl.BlockSpec((NORM_ROWS, d), lambda i: (i, 0)),
        out_shape=jax.ShapeDtypeStruct((t, d), out_dtype),
        compiler_params=_params("parallel"),
        name="rmsnorm",
    )(x, w.reshape(1, d))


def _matmul_kernel(x_ref, w_ref, o_ref):
    o_ref[...] = jnp.dot(x_ref[...], w_ref[...], preferred_element_type=F32).astype(o_ref.dtype)


def _matmul(x, w, out_dtype, rows=MM_ROWS, cols=MM_COLS, name="matmul"):
    t, k = x.shape
    n = w.shape[1]
    rows, cols = min(rows, t), min(cols, n)
    assert t % rows == 0 and n % cols == 0
    return pl.pallas_call(
        _matmul_kernel,
        grid=(n // cols, t // rows),
        in_specs=[pl.BlockSpec((rows, k), lambda j, i: (i, 0)),
                  pl.BlockSpec((k, cols), lambda j, i: (0, j))],
        out_specs=pl.BlockSpec((rows, cols), lambda j, i: (i, j)),
        out_shape=jax.ShapeDtypeStruct((t, n), out_dtype),
        compiler_params=_params("parallel", "parallel"),
        name=name,
    )(x, w)


def _residual_norm_kernel(x_ref, w_ref, r_ref, nw_ref, *out_refs):
    y = r_ref[...] + jnp.dot(x_ref[...], w_ref[...], preferred_element_type=F32)
    if len(out_refs) == 2:
        out_refs[0][...] = y
    h_ref = out_refs[-1]
    ms = jnp.mean(y * y, axis=-1, keepdims=True)
    h_ref[...] = (y * lax.rsqrt(ms + EPS) * nw_ref[...]).astype(h_ref.dtype)


def _matmul_residual_norm(x, w, residual, norm_w, h_dtype, rows, keep_sum=True, name="matmul_residual_norm"):
    t, k = x.shape
    n = w.shape[1]
    assert t % rows == 0
    row_block = pl.BlockSpec((rows, n), lambda i: (i, 0))
    out_shape = [jax.ShapeDtypeStruct((t, n), h_dtype)]
    if keep_sum:
        out_shape.insert(0, jax.ShapeDtypeStruct((t, n), F32))
    outs = pl.pallas_call(
        _residual_norm_kernel,
        grid=(t // rows,),
        in_specs=[pl.BlockSpec((rows, k), lambda i: (i, 0)),
                  pl.BlockSpec((k, n), lambda i: (0, 0), pipeline_mode=pl.Buffered(1)),
                  row_block,
                  pl.BlockSpec((1, n), lambda i: (0, 0))],
        out_specs=[row_block] * len(out_shape),
        out_shape=out_shape,
        compiler_params=_params("parallel"),
        name=name,
    )(x, w, residual, norm_w.reshape(1, n))
    return (outs[0], outs[1]) if keep_sum else (None, outs[0])


def _hgrn_kernel(q_ref, f_ref, i_ref, g_ref, lbl_ref, nw_ref, tri3_ref, sel_ref, mask_ref, o_ref, state_ref,
                 *, layer_j):
    C, L = HGRN_CHUNK, HGRN_SUB
    n_sub = C // L

    @pl.when(pl.program_id(2) == 0)
    def _():
        state_ref[...] = jnp.zeros_like(state_ref)

    lbl = lbl_ref[...]
    e = jnp.exp(lbl - jnp.max(lbl, axis=0, keepdims=True))
    sm = e / jnp.sum(e, axis=0, keepdims=True)
    lb_all = jnp.zeros((1, lbl.shape[1]), F32)
    for r in range(1, layer_j + 1):
        lb_all = lb_all + sm[r:r + 1, :]
    log_lb_all = jnp.log(jnp.maximum(lb_all, LB_FLOOR))
    log_1m_lb_all = jnp.log1p(-lb_all)

    def pair_products(q, k, b, s):
        first = (s // SUBLANES) * SUBLANES
        q3, k3, b3 = (x.reshape(n_sub, L, HEAD_DIM) for x in (q, k, b))
        decay = jnp.exp2(jnp.minimum(b3[:, first:, :] - b3[:, s:s + 1, :], 0.0))
        prod = q3[:, first:, :] * k3[:, s:s + 1, :] * decay
        if first:
            prod = jnp.concatenate([jnp.zeros((n_sub, first, HEAD_DIM), F32), prod], axis=1)
        return prod.reshape(C, HEAD_DIM).astype(BF16)

    def head_chunk(rows, lanes, head):
        lb, log_lb, log_1m_lb = lb_all[:, lanes], log_lb_all[:, lanes], log_1m_lb_all[:, lanes]
        z = f_ref[0, rows, lanes]
        q = _silu(q_ref[0, rows, lanes])
        v = i_ref[0, rows, lanes].astype(BF16)

        e_z = jnp.exp(-jnp.abs(z))
        cand = log_1m_lb + (jnp.minimum(z, 0.0) - jnp.log1p(e_z))
        log_f = jnp.maximum(log_lb, cand) + jnp.log1p(jnp.exp(-jnp.abs(log_lb - cand)))
        k = (1.0 - lb) * (jnp.where(z >= 0.0, e_z, 1.0) / (1.0 + e_z))
        hi = log_f.astype(BF16)
        rest = log_f - hi.astype(F32)
        mid = rest.astype(BF16)
        low = (rest - mid.astype(F32)).astype(BF16)
        b = jnp.dot(tri3_ref[...], jnp.concatenate([hi, mid, low], axis=0),
                    preferred_element_type=F32) * LOG2_E
        b_last = b[C - 1:C, :]

        state = state_ref[head]
        o = lax.dot_general((q * jnp.exp2(b)).astype(BF16), state.astype(BF16), NT_DIMS,
                            preferred_element_type=F32)

        below = [jnp.zeros((L, C), F32)]
        for i in range(1, n_sub):
            lo = i * L
            edge = b[lo - 1:lo, :]
            q_edge = (q[lo:lo + L, :] * jnp.exp2(b[lo:lo + L, :] - edge)).astype(BF16)
            k_edge = (k[:lo, :] * jnp.exp2(edge - b[:lo, :])).astype(BF16)
            k_edge = jnp.concatenate([k_edge, jnp.zeros((C - lo, HEAD_DIM), BF16)], axis=0)
            below.append(lax.dot_general(q_edge, k_edge, NT_DIMS, preferred_element_type=F32))
        pair = jnp.concatenate([pair_products(q, k, b, s) for s in range(L)], axis=1)
        diag = jnp.dot(pair, sel_ref[...], preferred_element_type=F32)
        scores = jnp.concatenate(below, axis=0) + diag * mask_ref[...]
        o = o + jnp.dot(scores.astype(BF16), v, preferred_element_type=F32)

        k_end = (k * jnp.exp2(b_last - b)).astype(BF16)
        state_ref[head] = state * jnp.exp2(b_last) + lax.dot_general(v, k_end, TN_DIMS,
                                                                     preferred_element_type=F32)

        y = o * lax.rsqrt(jnp.mean(o * o, axis=-1, keepdims=True) + EPS) * nw_ref[:, lanes]
        o_ref[0, rows, lanes] = (y * _silu(g_ref[0, rows, lanes])).astype(o_ref.dtype)

    def chunk(c, carry):
        rows = pl.ds(pl.multiple_of(c * C, C), C)
        for head in range(HGRN_HEADS_PER_STEP):
            head_chunk(rows, slice(head * HEAD_DIM, (head + 1) * HEAD_DIM), head)
        return carry

    lax.fori_loop(0, MIX_ROWS // C, chunk, 0, unroll=MIX_UNROLL)


def _hgrn(proj, lb_logits, norm_w, layer_j):
    bsz, seq, _ = proj.shape
    d_h = norm_w.shape[0]
    width = HGRN_HEADS_PER_STEP * HEAD_DIM
    n_groups = d_h // width
    n_layers = lb_logits.shape[0]

    C, L = HGRN_CHUNK, HGRN_SUB
    pos = jnp.arange(C)
    causal = pos[:, None] >= pos[None, :]
    tri3 = jnp.tile(causal.astype(BF16), (1, 3))
    pair_sel = (jnp.repeat(jnp.arange(L), HEAD_DIM)[:, None] == (pos % L)[None, :]).astype(BF16)
    diag_mask = (causal & (pos[:, None] // L == pos[None, :] // L)).astype(F32)

    def col(group):
        return pl.BlockSpec((1, MIX_ROWS, width), lambda b, h, s: (b, s, group * n_groups + h))

    def whole(a):
        return pl.BlockSpec(a.shape, lambda b, h, s: (0, 0))

    return pl.pallas_call(
        functools.partial(_hgrn_kernel, layer_j=layer_j),
        grid=(bsz, n_groups, seq // MIX_ROWS),
        in_specs=[col(0), col(1), col(2), col(3),
                  pl.BlockSpec((n_layers, width), lambda b, h, s: (0, h)),
                  pl.BlockSpec((1, width), lambda b, h, s: (0, h)),
                  whole(tri3), whole(pair_sel), whole(diag_mask)],
        out_specs=pl.BlockSpec((1, MIX_ROWS, width), lambda b, h, s: (b, s, h)),
        out_shape=jax.ShapeDtypeStruct((bsz, seq, d_h), BF16),
        scratch_shapes=[pltpu.VMEM((HGRN_HEADS_PER_STEP, HEAD_DIM, HEAD_DIM), F32)],
        compiler_params=_params("parallel", "parallel", "arbitrary"),
        name="hgrn2",
    )(proj, proj, proj, proj, lb_logits, norm_w.reshape(1, d_h), tri3, pair_sel, diag_mask)


def _retention_kernel(q_ref, k_ref, v_ref, g_ref, cos_ref, sin_ref, intra_ref, edge_ref, nw_ref, o_ref, state_ref):
    C = RET_CHUNK

    @pl.when(pl.program_id(2) == 0)
    def _():
        state_ref[...] = jnp.zeros_like(state_ref)

    def rope(x, cos, sin_signed):
        return x * cos + pltpu.roll(x, HEAD_DIM // 2, 1) * sin_signed

    def head_chunk(sl, lanes, head):
        cos, sin_signed = cos_ref[sl, :], sin_ref[sl, :]
        q = rope(q_ref[0, sl, lanes], cos, sin_signed)
        k = rope(k_ref[0, sl, lanes], cos, sin_signed) * (HEAD_DIM ** -0.5)
        v = v_ref[0, sl, lanes].astype(BF16)
        state = state_ref[head]
        w_start, w_end, decay_chunk = edge_ref[head, 0], edge_ref[head, 1], edge_ref[head, 2, 0:1, :]

        scores = lax.dot_general(q.astype(BF16), k.astype(BF16), NT_DIMS,
                                 preferred_element_type=F32) * intra_ref[head]
        o = jnp.dot(scores.astype(BF16), v, preferred_element_type=F32)
        o = o + lax.dot_general((q * w_start).astype(BF16), state.astype(BF16), NT_DIMS,
                                preferred_element_type=F32)
        state_ref[head] = decay_chunk * state + lax.dot_general(v, (k * w_end).astype(BF16), TN_DIMS,
                                                                preferred_element_type=F32)

        cen = o - jnp.mean(o, axis=-1, keepdims=True)
        y = cen * lax.rsqrt(jnp.mean(cen * cen, axis=-1, keepdims=True) + EPS) * nw_ref[:, lanes]
        o_ref[0, sl, lanes] = (y * _silu(g_ref[0, sl, lanes])).astype(o_ref.dtype)

    def chunk(c, carry):
        sl = pl.ds(pl.multiple_of(c * C, C), C)
        for head in range(RET_HEADS_PER_STEP):
            head_chunk(sl, slice(head * HEAD_DIM, (head + 1) * HEAD_DIM), head)
        return carry

    lax.fori_loop(0, MIX_ROWS // C, chunk, 0, unroll=MIX_UNROLL)


def _retention(proj, norm_w, first_group):
    bsz, seq, _ = proj.shape
    d_r = norm_w.shape[0]
    n_heads = d_r // HEAD_DIM
    half = HEAD_DIM // 2

    inv = ROPE_BASE ** (-jnp.arange(0, HEAD_DIM, 2, dtype=F32) / HEAD_DIM)
    ang = jnp.arange(seq).astype(F32)[:, None] * inv[None, :]
    cos = jnp.concatenate([jnp.cos(ang), jnp.cos(ang)], axis=-1)
    sin_signed = jnp.concatenate([-jnp.sin(ang), jnp.sin(ang)], axis=-1)
    assert cos.shape == (seq, 2 * half)
    C = RET_CHUNK
    log_gamma = jnp.log1p(-jnp.exp2(-RET_DECAY_BASE - jnp.arange(n_heads, dtype=F32)))
    pos = jnp.arange(C, dtype=F32)
    rel = pos[:, None] - pos[None, :]
    intra = jnp.where(rel >= 0, jnp.exp(log_gamma[:, None, None] * jnp.maximum(rel, 0.0)), 0.0)
    w_start = jnp.exp(log_gamma[:, None] * (pos + 1.0)[None, :])
    w_end = jnp.exp(log_gamma[:, None] * (C - 1.0 - pos)[None, :])
    decay_chunk = jnp.broadcast_to(jnp.exp(log_gamma * C)[:, None], (n_heads, C))
    edge = jnp.broadcast_to(jnp.stack([w_start, w_end, decay_chunk], axis=1)[..., None],
                            (n_heads, 3, C, HEAD_DIM))

    per_step = RET_HEADS_PER_STEP
    width = per_step * HEAD_DIM
    n_groups = n_heads // per_step
    first = first_group * n_groups

    def col(group):
        return pl.BlockSpec((1, MIX_ROWS, width), lambda b, h, s: (b, s, first + group * n_groups + h))

    table = pl.BlockSpec((MIX_ROWS, HEAD_DIM), lambda b, h, s: (s, 0))
    return pl.pallas_call(
        _retention_kernel,
        grid=(bsz, n_groups, seq // MIX_ROWS),
        in_specs=[col(0), col(1), col(2), col(3), table, table,
                  pl.BlockSpec((per_step, C, C), lambda b, h, s: (h, 0, 0)),
                  pl.BlockSpec((per_step, 3, C, HEAD_DIM), lambda b, h, s: (h, 0, 0, 0)),
                  pl.BlockSpec((1, width), lambda b, h, s: (0, h))],
        out_specs=pl.BlockSpec((1, MIX_ROWS, width), lambda b, h, s: (b, s, h)),
        out_shape=jax.ShapeDtypeStruct((bsz, seq, d_r), BF16),
        scratch_shapes=[pltpu.VMEM((per_step, HEAD_DIM, HEAD_DIM), F32)],
        compiler_params=_params("parallel", "parallel", "arbitrary"),
        name="retention",
    )(proj, proj, proj, proj, cos, sin_signed, intra, edge, norm_w.reshape(1, d_r))


def _forget_scan_kernel(f_ref, bias_ref, c_ref, carry_ref):
    @pl.when(pl.program_id(1) == 0)
    def _():
        carry_ref[...] = jnp.zeros_like(carry_ref)

    log_f = _log_sigmoid(f_ref[0] + bias_ref[...])
    c = jnp.dot(_lower_tri(SCAN_ROWS), log_f, precision=lax.Precision.HIGHEST,
                preferred_element_type=F32) + carry_ref[...]
    c_ref[0] = c
    carry_ref[...] = c[SCAN_ROWS - 1:SCAN_ROWS, :]


def _forget_scan(f, bias):
    bsz, seq, lanes = f.shape
    return pl.pallas_call(
        _forget_scan_kernel,
        grid=(bsz, seq // SCAN_ROWS),
        in_specs=[pl.BlockSpec((1, SCAN_ROWS, lanes), lambda b, s: (b, s, 0)),
                  pl.BlockSpec((1, lanes), lambda b, s: (0, 0))],
        out_specs=pl.BlockSpec((1, SCAN_ROWS, lanes), lambda b, s: (b, s, 0)),
        out_shape=jax.ShapeDtypeStruct((bsz, seq, lanes), F32),
        scratch_shapes=[pltpu.VMEM((1, lanes), F32)],
        compiler_params=_params("parallel", "arbitrary"),
        name="forget_scan",
    )(f, bias)


def _split3(x):
    hi = x.astype(BF16).astype(F32)
    rest = x - hi
    mid = rest.astype(BF16).astype(F32)
    low = (rest - mid).astype(BF16).astype(F32)
    return hi, mid, low


def _fox_kernel(pairs_ref, q_ref, k_ref, v_ref, g_ref, ct_ref, o_ref,
                kaug_ref, vt_ref, qaug_ref, m_ref, acc_ref, s0_ref, s1_ref, p0_ref, p1_ref, a0_ref, a1_ref):
    T = FOX_BLOCK
    n_blocks = vt_ref.shape[0]
    n_below = n_blocks * (n_blocks - 1) // 2
    aug_row = lax.broadcasted_iota(jnp.int32, (HEAD_DIM, T), 0)

    def bias_rows(c_row, for_queries):
        hi, mid, low = _split3(c_row)
        if for_queries:
            pieces = jnp.where(aug_row == 3, hi, jnp.where(aug_row == 4, mid, jnp.where(aug_row == 5, low, 0.0)))
            return jnp.where(aug_row < 3, 1.0, pieces)
        pieces = jnp.where(aug_row == 0, -hi, jnp.where(aug_row == 1, -mid, jnp.where(aug_row == 2, -low, 0.0)))
        return jnp.where((aug_row >= 3) & (aug_row < 6), 1.0, pieces)

    for j in range(n_blocks):
        rows = slice(j * T, (j + 1) * T)
        c_j = ct_ref[0, 0, j:j + 1, :] * LOG2_E
        kaug_ref[rows, 0:HEAD_DIM] = k_ref[0, rows, :]
        kaug_ref[rows, HEAD_DIM:2 * HEAD_DIM] = bias_rows(c_j, False).T.astype(BF16)
        vt_ref[j, 0:HEAD_DIM, :] = v_ref[0, rows, :].astype(F32).T.astype(BF16)
        vt_ref[j, HEAD_DIM:2 * HEAD_DIM, :] = jnp.ones((HEAD_DIM, T), BF16)
        qaug_ref[j, 0:HEAD_DIM, :] = (q_ref[0, rows, :].astype(F32) * (HEAD_DIM ** -0.5 * LOG2_E)).T.astype(BF16)
        qaug_ref[j, HEAD_DIM:2 * HEAD_DIM, :] = bias_rows(c_j, True).astype(BF16)
    m_ref[...] = jnp.full(m_ref.shape, MASK_VALUE, F32)
    acc_ref[...] = jnp.zeros(acc_ref.shape, F32)
    for p_ref, a_ref in ((p0_ref, a0_ref), (p1_ref, a1_ref)):
        p_ref[...] = jnp.zeros((T, T), BF16)
        a_ref[...] = jnp.ones((1, T), F32)

    key_id = lax.broadcasted_iota(jnp.int32, (T, T), 0)
    query_id = lax.broadcasted_iota(jnp.int32, (T, T), 1)

    def form_logits(pair, s_ref):
        qb, kb = pair
        keys = pl.ds(pl.multiple_of(kb * T, T), T)
        s_ref[...] = jnp.dot(kaug_ref[keys, :], qaug_ref[qb], preferred_element_type=F32)

    def softmax_update(pair, valid, diagonal, s_ref, p_ref, a_ref):
        qb, _ = pair
        s = s_ref[...]
        if diagonal:
            s = jnp.where(key_id <= query_id, s, MASK_VALUE)
        m = m_ref[qb]
        m_new = jnp.maximum(m, jnp.where(valid, jnp.max(s, axis=0, keepdims=True), MASK_VALUE))
        p_ref[...] = jnp.exp2(s - jnp.where(valid, m_new, -MASK_VALUE)).astype(BF16)
        a_ref[...] = jnp.exp2(m - m_new)
        m_ref[qb] = m_new

    def accumulate(pair, p_ref, a_ref):
        qb, kb = pair
        acc_ref[qb] = a_ref[...] * acc_ref[qb] + jnp.dot(vt_ref[kb], p_ref[...], preferred_element_type=F32)

    def sweep(count, pair_of, diagonal):
        if count == 0:
            return

        def clamped(n):
            return pair_of(jnp.clip(n, 0, count - 1))

        def tick(n, s_now, p_now, a_now, s_next, p_prev, a_prev):
            form_logits(clamped(n + 1), s_next)
            softmax_update(clamped(n), n < count, diagonal, s_now, p_now, a_now)
            accumulate(clamped(n - 1), p_prev, a_prev)

        def two_ticks(i, carry):
            n = 2 * i
            tick(n, s0_ref, p0_ref, a0_ref, s1_ref, p1_ref, a1_ref)
            tick(n + 1, s1_ref, p1_ref, a1_ref, s0_ref, p0_ref, a0_ref)
            return carry

        form_logits(clamped(0), s0_ref)
        lax.fori_loop(0, (count + 3) // 2, two_ticks, 0)

    sweep(n_blocks, lambda n: (n, n), True)
    sweep(n_below, lambda n: (pairs_ref[0, n], pairs_ref[1, n]), False)

    for j in range(n_blocks):
        rows = slice(j * T, (j + 1) * T)
        acc = acc_ref[j]
        o = (acc[:HEAD_DIM, :] / acc[HEAD_DIM:, :]).T
        o_ref[0, rows, :] = (o * _sigmoid(g_ref[0, rows, :].astype(F32))).astype(o_ref.dtype)


def _fox(proj, c, n_heads):
    bsz, seq, _ = proj.shape
    T = FOX_BLOCK
    n_blocks = seq // T
    c_rows = jnp.transpose(c[:, :, :n_heads], (0, 2, 1)).reshape(bsz, n_heads, n_blocks, T)
    below = [(qb, kb) for qb in range(n_blocks) for kb in range(qb)] or [(0, 0)]
    pairs = jnp.asarray(below, jnp.int32).T

    def head_cols(group):
        return pl.BlockSpec((1, seq, HEAD_DIM), lambda b, h: (b, 0, group * n_heads + h))

    return pl.pallas_call(
        _fox_kernel,
        grid=(bsz, n_heads),
        in_specs=[pl.BlockSpec(memory_space=pltpu.SMEM),
                  head_cols(0), head_cols(1), head_cols(2), head_cols(3),
                  pl.BlockSpec((1, 1, n_blocks, T), lambda b, h: (b, h, 0, 0))],
        out_specs=pl.BlockSpec((1, seq, HEAD_DIM), lambda b, h: (b, 0, h)),
        out_shape=jax.ShapeDtypeStruct((bsz, seq, n_heads * HEAD_DIM), BF16),
        scratch_shapes=[pltpu.VMEM((seq, 2 * HEAD_DIM), BF16),
                        pltpu.VMEM((n_blocks, 2 * HEAD_DIM, T), BF16),
                        pltpu.VMEM((n_blocks, 2 * HEAD_DIM, T), BF16),
                        pltpu.VMEM((n_blocks, 1, T), F32),
                        pltpu.VMEM((n_blocks, 2 * HEAD_DIM, T), F32),
                        pltpu.VMEM((T, T), F32), pltpu.VMEM((T, T), F32),
                        pltpu.VMEM((T, T), BF16), pltpu.VMEM((T, T), BF16),
                        pltpu.VMEM((1, T), F32), pltpu.VMEM((1, T), F32)],
        compiler_params=_params("parallel", "parallel"),
        name="fox_attention",
    )(pairs, proj, proj, proj, proj, c_rows)


def _ffn_up_kernel(x_ref, wg_ref, wv_ref, cg_ref, cv_ref, o_ref, ug_ref, uv_ref, *, seq_steps):
    rows = x_ref.shape[0]

    @pl.when(pl.program_id(1) % seq_steps == 0)
    def _():
        ug_ref[0:SUBLANES, :] = jnp.zeros((SUBLANES, ug_ref.shape[1]), F32)
        uv_ref[0:SUBLANES, :] = jnp.zeros((SUBLANES, uv_ref.shape[1]), F32)

    x = x_ref[...]

    def conv(w_ref, c_ref, u_ref):
        u_ref[SUBLANES:SUBLANES + rows, :] = jnp.dot(x, w_ref[...], preferred_element_type=F32)
        out = c_ref[CONV_WIDTH - 1:CONV_WIDTH, :] * u_ref[SUBLANES:SUBLANES + rows, :]
        for tap in range(CONV_WIDTH - 1):
            back = CONV_WIDTH - 1 - tap
            out = out + c_ref[tap:tap + 1, :] * u_ref[SUBLANES - back:SUBLANES - back + rows, :]
        carry = u_ref[rows:rows + SUBLANES, :]
        u_ref[0:SUBLANES, :] = carry
        return out

    gate = conv(wg_ref, cg_ref, ug_ref)
    val = conv(wv_ref, cv_ref, uv_ref)
    o_ref[...] = (_silu(gate) * val).astype(o_ref.dtype)


def _ffn_up(h, w_up, conv_w, seq):
    t, d = h.shape
    f = w_up.shape[1] // 2
    rows = min(MM_ROWS, seq)
    assert seq % rows == 0 and f % FFN_COLS == 0
    n_cols = f // FFN_COLS
    return pl.pallas_call(
        functools.partial(_ffn_up_kernel, seq_steps=seq // rows),
        grid=(n_cols, t // rows),
        in_specs=[pl.BlockSpec((rows, d), lambda j, i: (i, 0)),
                  pl.BlockSpec((d, FFN_COLS), lambda j, i: (0, j)),
                  pl.BlockSpec((d, FFN_COLS), lambda j, i: (0, j + n_cols)),
                  pl.BlockSpec((CONV_WIDTH, FFN_COLS), lambda j, i: (0, j)),
                  pl.BlockSpec((CONV_WIDTH, FFN_COLS), lambda j, i: (0, j + n_cols))],
        out_specs=pl.BlockSpec((rows, FFN_COLS), lambda j, i: (i, j)),
        out_shape=jax.ShapeDtypeStruct((t, f), BF16),
        scratch_shapes=[pltpu.VMEM((rows + SUBLANES, FFN_COLS), F32),
                        pltpu.VMEM((rows + SUBLANES, FFN_COLS), F32)],
        compiler_params=_params("parallel", "arbitrary"),
        name="ffn_up_conv",
    )(h, w_up, w_up, conv_w, conv_w)


def _pad_cols(w, width):
    return jnp.pad(w, ((0, 0), (0, width - w.shape[1])))


def kernel(x, attn_norm_w, ffn_norm_w, final_norm_w, even_w_in, hgrn_lb_logits, hgrn_norm_w, ret_norm_w,
           even_w_out, odd_w_in, odd_b_f, odd_w_out, ffn_w_up, ffn_conv_w, ffn_w_down):
    bsz, seq, d_model = x.shape
    depth = attn_norm_w.shape[0]
    tokens = bsz * seq
    d_hgrn = hgrn_norm_w.shape[1]
    d_fox = odd_w_out.shape[1]
    n_fox = odd_b_f.shape[1]
    d_ff = ffn_w_down.shape[1]
    d_ff_pad = FFN_COLS * math.ceil(d_ff / FFN_COLS)

    xf = x.reshape(tokens, d_model)
    h = _rmsnorm(xf, attn_norm_w[0], BF16)
    for layer in range(depth):
        j = layer // 2
        if layer % 2 == 0:
            proj = _matmul(h, even_w_in[j].astype(BF16), F32, cols=IN_COLS, name="even_in").reshape(bsz, seq, -1)
            o_a = _hgrn(proj, hgrn_lb_logits, hgrn_norm_w[j], j)
            o_b = _retention(proj, ret_norm_w[j], first_group=4 * d_hgrn // ret_norm_w.shape[1])
            o = jnp.concatenate([o_a, o_b], axis=-1).reshape(tokens, -1)
            w_out = even_w_out[j]
        else:
            w_in = odd_w_in[j]
            proj = _matmul(h, w_in[:, :4 * d_fox].astype(BF16), BF16, cols=IN_COLS,
                           name="fox_in").reshape(bsz, seq, -1)
            f = _matmul(h, _pad_cols(w_in[:, 4 * d_fox:], LANES).astype(BF16), F32, name="fox_forget")
            c = _forget_scan(f.reshape(bsz, seq, LANES), _pad_cols(odd_b_f[j][None, :], LANES))
            o = _fox(proj, c, n_fox).reshape(tokens, -1)
            w_out = odd_w_out[j]
        xf, h = _matmul_residual_norm(o, w_out.astype(BF16), xf, ffn_norm_w[layer], BF16, rows=OUT_ROWS,
                                      name="mixer_out_norm")
        w_up = ffn_w_up[layer]
        w_up = jnp.concatenate([_pad_cols(w_up[:, :d_ff], d_ff_pad), _pad_cols(w_up[:, d_ff:], d_ff_pad)],
                               axis=1).astype(BF16)
        conv_w = ffn_conv_w[layer]
        conv_w = jnp.concatenate([_pad_cols(conv_w[:, :d_ff], d_ff_pad), _pad_cols(conv_w[:, d_ff:], d_ff_pad)],
                                 axis=1)
        act = _ffn_up(h, w_up, conv_w, seq)
        w_down = jnp.pad(ffn_w_down[layer], ((0, d_ff_pad - d_ff), (0, 0))).astype(BF16)
        last = layer == depth - 1
        xf, h = _matmul_residual_norm(act, w_down, xf, final_norm_w if last else attn_norm_w[layer + 1],
                                      x.dtype if last else BF16, rows=DOWN_ROWS, keep_sum=not last,
                                      name="ffn_down_norm")
    return h.reshape(bsz, seq, d_model)
```

```python
import functools
import math

import jax
import jax.numpy as jnp
from jax import lax
from jax.experimental import pallas as pl
from jax.experimental.pallas import tpu as pltpu

F32 = jnp.float32
BF16 = jnp.bfloat16

HEAD_DIM = 128
ROPE_BASE = 10000.0
RET_DECAY_BASE = 5.0
EPS = 1e-6
LB_FLOOR = 1e-30
MASK_VALUE = -1e30
CONV_WIDTH = 3
LOG2_E = math.log2(math.e)

LANES = 128
SUBLANES = 8
VMEM_LIMIT_BYTES = 48 * 1024 * 1024

NORM_ROWS = 256
MM_ROWS = 1024
MM_COLS = 512
IN_COLS = 1024
OUT_ROWS = 512
DOWN_ROWS = 256
FFN_COLS = 512
MIX_ROWS = 512
MIX_UNROLL = 2
HGRN_CHUNK = 64
HGRN_SUB = 16
HGRN_HEADS_PER_STEP = 8
RET_CHUNK = 128
RET_HEADS_PER_STEP = 4
FOX_BLOCK = 512
FOX_TICKS_PER_STEP = 4
FOX_ONES_ROWS = 16
SCAN_ROWS = 256

NT_DIMS = (((1,), (1,)), ((), ()))
TN_DIMS = (((0,), (0,)), ((), ()))


def _params(*semantics):
    return pltpu.CompilerParams(dimension_semantics=semantics, vmem_limit_bytes=VMEM_LIMIT_BYTES)


def _sigmoid(x):
    return 1.0 / (1.0 + jnp.exp(-x))


def _silu(x):
    return x * _sigmoid(x)


def _log_sigmoid(x):
    return jnp.minimum(x, 0.0) - jnp.log1p(jnp.exp(-jnp.abs(x)))


def _lower_tri(n):
    rows = lax.broadcasted_iota(jnp.int32, (n, n), 0)
    cols = lax.broadcasted_iota(jnp.int32, (n, n), 1)
    return (rows >= cols).astype(F32)


def _rmsnorm_kernel(x_ref, w_ref, o_ref):
    x = x_ref[...]
    ms = jnp.mean(x * x, axis=-1, keepdims=True)
    o_ref[...] = (x * lax.rsqrt(ms + EPS) * w_ref[...]).astype(o_ref.dtype)


def _rmsnorm(x, w, out_dtype):
    t, d = x.shape
    return pl.pallas_call(
        _rmsnorm_kernel,
        grid=(t // NORM_ROWS,),
        in_specs=[pl.BlockSpec((NORM_ROWS, d), lambda i: (i, 0)),
                  pl.BlockSpec((1, d), lambda i: (0, 0))],
        out_specs=pl.BlockSpec((NORM_ROWS, d), lambda i: (i, 0)),
        out_shape=jax.ShapeDtypeStruct((t, d), out_dtype),
        compiler_params=_params("parallel"),
        name="rmsnorm",
    )(x, w.reshape(1, d))


def _cast_weight_tile(w_ref, wb_ref, valid_cols):
    w = w_ref[...]
    if valid_cols < w.shape[1]:
        col = lax.broadcasted_iota(jnp.int32, w.shape, 1)
        w = jnp.where(col < valid_cols, w, 0.0)
    wb_ref[...] = w.astype(BF16)


def _matmul_kernel(x_ref, w_ref, o_ref, wb_ref, *, valid_cols):
    @pl.when(pl.program_id(1) == 0)
    def _():
        _cast_weight_tile(w_ref, wb_ref, valid_cols)

    o_ref[...] = jnp.dot(x_ref[...], wb_ref[...], preferred_element_type=F32).astype(o_ref.dtype)


def _matmul(x, w, out_dtype, n_out, rows=MM_ROWS, cols=MM_COLS, first_col=0, name="matmul"):
    t, k = x.shape
    rows, cols = min(rows, t), min(cols, n_out)
    assert t % rows == 0 and n_out % cols == 0 and first_col % cols == 0
    first_block = first_col // cols
    valid_cols = min(cols, w.shape[1] - first_col) if n_out == cols else cols
    assert first_col + n_out <= w.shape[1] or n_out == cols
    return pl.pallas_call(
        functools.partial(_matmul_kernel, valid_cols=valid_cols),
        grid=(n_out // cols, t // rows),
        in_specs=[pl.BlockSpec((rows, k), lambda j, i: (i, 0)),
                  pl.BlockSpec((k, cols), lambda j, i: (0, first_block + j))],
        out_specs=pl.BlockSpec((rows, cols), lambda j, i: (i, j)),
        out_shape=jax.ShapeDtypeStruct((t, n_out), out_dtype),
        scratch_shapes=[pltpu.VMEM((k, cols), BF16)],
        compiler_params=_params("parallel", "arbitrary"),
        name=name,
    )(x, w)


def _residual_norm_kernel(x_ref, w_ref, r_ref, nw_ref, *out_refs):
    y = r_ref[...] + jnp.dot(x_ref[...], w_ref[...], preferred_element_type=F32)
    if len(out_refs) == 2:
        out_refs[0][...] = y
    h_ref = out_refs[-1]
    ms = jnp.mean(y * y, axis=-1, keepdims=True)
    h_ref[...] = (y * lax.rsqrt(ms + EPS) * nw_ref[...]).astype(h_ref.dtype)


def _matmul_residual_norm(x, w, residual, norm_w, h_dtype, rows, keep_sum=True, name="matmul_residual_norm"):
    t, k = x.shape
    n = w.shape[1]
    assert t % rows == 0
    row_block = pl.BlockSpec((rows, n), lambda i: (i, 0))
    out_shape = [jax.ShapeDtypeStruct((t, n), h_dtype)]
    if keep_sum:
        out_shape.insert(0, jax.ShapeDtypeStruct((t, n), F32))
    outs = pl.pallas_call(
        _residual_norm_kernel,
        grid=(t // rows,),
        in_specs=[pl.BlockSpec((rows, k), lambda i: (i, 0)),
                  pl.BlockSpec((k, n), lambda i: (0, 0), pipeline_mode=pl.Buffered(1)),
                  row_block,
                  pl.BlockSpec((1, n), lambda i: (0, 0))],
        out_specs=[row_block] * len(out_shape),
        out_shape=out_shape,
        compiler_params=_params("parallel"),
        name=name,
    )(x, w, residual, norm_w.reshape(1, n))
    return (outs[0], outs[1]) if keep_sum else (None, outs[0])


def _hgrn_kernel(q_ref, f_ref, i_ref, g_ref, lbl_ref, nw_ref, tri3_ref, sel_ref, mask_ref, o_ref, state_ref,
                 *, layer_j):
    C, L = HGRN_CHUNK, HGRN_SUB
    n_sub = C // L

    @pl.when(pl.program_id(2) == 0)
    def _():
        state_ref[...] = jnp.zeros_like(state_ref)

    lbl = lbl_ref[...]
    e = jnp.exp(lbl - jnp.max(lbl, axis=0, keepdims=True))
    sm = e / jnp.sum(e, axis=0, keepdims=True)
    lb_all = jnp.zeros((1, lbl.shape[1]), F32)
    for r in range(1, layer_j + 1):
        lb_all = lb_all + sm[r:r + 1, :]

    def pair_products(q, k, b, s):
        first = (s // SUBLANES) * SUBLANES
        q3, k3, b3 = (x.reshape(n_sub, L, HEAD_DIM) for x in (q, k, b))
        decay = jnp.exp2(jnp.minimum(b3[:, first:, :] - b3[:, s:s + 1, :], 0.0))
        prod = q3[:, first:, :] * k3[:, s:s + 1, :] * decay
        if first:
            prod = jnp.concatenate([jnp.zeros((n_sub, first, HEAD_DIM), F32), prod], axis=1)
        return prod.reshape(C, HEAD_DIM).astype(BF16)

    def head_chunk(rows, lanes, head):
        lb_floor, one_m_lb = jnp.maximum(lb_all[:, lanes], LB_FLOOR), 1.0 - lb_all[:, lanes]
        z = f_ref[0, rows, lanes]
        q = _silu(q_ref[0, rows, lanes])
        v = i_ref[0, rows, lanes].astype(BF16)

        e_z = jnp.exp(-jnp.abs(z))
        inv = 1.0 / (1.0 + e_z)
        log_f = jnp.log(lb_floor + one_m_lb * (jnp.where(z >= 0.0, 1.0, e_z) * inv))
        k = one_m_lb * (jnp.where(z >= 0.0, e_z, 1.0) * inv)
        hi = log_f.astype(BF16)
        rest = log_f - hi.astype(F32)
        mid = rest.astype(BF16)
        low = (rest - mid.astype(F32)).astype(BF16)
        b = jnp.dot(tri3_ref[...], jnp.concatenate([hi, mid, low], axis=0),
                    preferred_element_type=F32) * LOG2_E
        b_last = b[C - 1:C, :]

        state = state_ref[head]
        o = lax.dot_general((q * jnp.exp2(b)).astype(BF16), state.astype(BF16), NT_DIMS,
                            preferred_element_type=F32)

        below = [jnp.zeros((L, C), F32)]
        for i in range(1, n_sub):
            lo = i * L
            edge = b[lo - 1:lo, :]
            q_edge = (q[lo:lo + L, :] * jnp.exp2(b[lo:lo + L, :] - edge)).astype(BF16)
            k_edge = (k[:lo, :] * jnp.exp2(edge - b[:lo, :])).astype(BF16)
            k_edge = jnp.concatenate([k_edge, jnp.zeros((C - lo, HEAD_DIM), BF16)], axis=0)
            below.append(lax.dot_general(q_edge, k_edge, NT_DIMS, preferred_element_type=F32))
        pair = jnp.concatenate([pair_products(q, k, b, s) for s in range(L)], axis=1)
        diag = jnp.dot(pair, sel_ref[...], preferred_element_type=F32)
        scores = jnp.concatenate(below, axis=0) + diag * mask_ref[...]
        o = o + jnp.dot(scores.astype(BF16), v, preferred_element_type=F32)

        k_end = (k * jnp.exp2(b_last - b)).astype(BF16)
        state_ref[head] = state * jnp.exp2(b_last) + lax.dot_general(v, k_end, TN_DIMS,
                                                                     preferred_element_type=F32)

        y = o * lax.rsqrt(jnp.mean(o * o, axis=-1, keepdims=True) + EPS) * nw_ref[:, lanes]
        o_ref[0, rows, lanes] = (y * _silu(g_ref[0, rows, lanes])).astype(o_ref.dtype)

    def chunk(c, carry):
        rows = pl.ds(pl.multiple_of(c * C, C), C)
        for head in range(HGRN_HEADS_PER_STEP):
            head_chunk(rows, slice(head * HEAD_DIM, (head + 1) * HEAD_DIM), head)
        return carry

    lax.fori_loop(0, MIX_ROWS // C, chunk, 0, unroll=MIX_UNROLL)


def _hgrn(proj, lb_logits, norm_w, layer_j):
    bsz, seq, _ = proj.shape
    d_h = norm_w.shape[0]
    width = HGRN_HEADS_PER_STEP * HEAD_DIM
    n_groups = d_h // width
    n_layers = lb_logits.shape[0]

    C, L = HGRN_CHUNK, HGRN_SUB
    pos = jnp.arange(C)
    causal = pos[:, None] >= pos[None, :]
    tri3 = jnp.tile(causal.astype(BF16), (1, 3))
    pair_sel = (jnp.repeat(jnp.arange(L), HEAD_DIM)[:, None] == (pos % L)[None, :]).astype(BF16)
    diag_mask = (causal & (pos[:, None] // L == pos[None, :] // L)).astype(F32)

    def col(group):
        return pl.BlockSpec((1, MIX_ROWS, width), lambda b, h, s: (b, s, group * n_groups + h))

    def whole(a):
        return pl.BlockSpec(a.shape, lambda b, h, s: (0, 0))

    return pl.pallas_call(
        functools.partial(_hgrn_kernel, layer_j=layer_j),
        grid=(bsz, n_groups, seq // MIX_ROWS),
        in_specs=[col(0), col(1), col(2), col(3),
                  pl.BlockSpec((n_layers, width), lambda b, h, s: (0, h)),
                  pl.BlockSpec((1, width), lambda b, h, s: (0, h)),
                  whole(tri3), whole(pair_sel), whole(diag_mask)],
        out_specs=pl.BlockSpec((1, MIX_ROWS, width), lambda b, h, s: (b, s, h)),
        out_shape=jax.ShapeDtypeStruct((bsz, seq, d_h), BF16),
        scratch_shapes=[pltpu.VMEM((HGRN_HEADS_PER_STEP, HEAD_DIM, HEAD_DIM), F32)],
        compiler_params=_params("parallel", "parallel", "arbitrary"),
        name="hgrn2",
    )(proj, proj, proj, proj, lb_logits, norm_w.reshape(1, d_h), tri3, pair_sel, diag_mask)


def _retention_kernel(q_ref, k_ref, v_ref, g_ref, cos_ref, sin_ref, intra_ref, edge_ref, nw_ref, o_ref, state_ref):
    C = RET_CHUNK

    @pl.when(pl.program_id(2) == 0)
    def _():
        state_ref[...] = jnp.zeros_like(state_ref)

    def rope(x, cos, sin_signed):
        return x * cos + pltpu.roll(x, HEAD_DIM // 2, 1) * sin_signed

    def head_chunk(sl, lanes, head):
        cos, sin_signed = cos_ref[sl, :], sin_ref[sl, :]
        q = rope(q_ref[0, sl, lanes], cos, sin_signed)
        k = rope(k_ref[0, sl, lanes], cos, sin_signed) * (HEAD_DIM ** -0.5)
        v = v_ref[0, sl, lanes].astype(BF16)
        state = state_ref[head]
        w_start, w_end, decay_chunk = edge_ref[head, 0], edge_ref[head, 1], edge_ref[head, 2, 0:1, :]

        scores = lax.dot_general(q.astype(BF16), k.astype(BF16), NT_DIMS,
                                 preferred_element_type=F32) * intra_ref[head]
        o = jnp.dot(scores.astype(BF16), v, preferred_element_type=F32)
        o = o + lax.dot_general((q * w_start).astype(BF16), state.astype(BF16), NT_DIMS,
                                preferred_element_type=F32)
        state_ref[head] = decay_chunk * state + lax.dot_general(v, (k * w_end).astype(BF16), TN_DIMS,
                                                                preferred_element_type=F32)

        cen = o - jnp.mean(o, axis=-1, keepdims=True)
        y = cen * lax.rsqrt(jnp.mean(cen * cen, axis=-1, keepdims=True) + EPS) * nw_ref[:, lanes]
        o_ref[0, sl, lanes] = (y * _silu(g_ref[0, sl, lanes])).astype(o_ref.dtype)

    def chunk(c, carry):
        sl = pl.ds(pl.multiple_of(c * C, C), C)
        for head in range(RET_HEADS_PER_STEP):
            head_chunk(sl, slice(head * HEAD_DIM, (head + 1) * HEAD_DIM), head)
        return carry

    lax.fori_loop(0, MIX_ROWS // C, chunk, 0, unroll=MIX_UNROLL)


def _retention(proj, norm_w, first_group):
    bsz, seq, _ = proj.shape
    d_r = norm_w.shape[0]
    n_heads = d_r // HEAD_DIM
    half = HEAD_DIM // 2

    inv = ROPE_BASE ** (-jnp.arange(0, HEAD_DIM, 2, dtype=F32) / HEAD_DIM)
    ang = jnp.arange(seq).astype(F32)[:, None] * inv[None, :]
    cos = jnp.concatenate([jnp.cos(ang), jnp.cos(ang)], axis=-1)
    sin_signed = jnp.concatenate([-jnp.sin(ang), jnp.sin(ang)], axis=-1)
    assert cos.shape == (seq, 2 * half)
    C = RET_CHUNK
    log_gamma = jnp.log1p(-jnp.exp2(-RET_DECAY_BASE - jnp.arange(n_heads, dtype=F32)))
    pos = jnp.arange(C, dtype=F32)
    rel = pos[:, None] - pos[None, :]
    intra = jnp.where(rel >= 0, jnp.exp(log_gamma[:, None, None] * jnp.maximum(rel, 0.0)), 0.0)
    w_start = jnp.exp(log_gamma[:, None] * (pos + 1.0)[None, :])
    w_end = jnp.exp(log_gamma[:, None] * (C - 1.0 - pos)[None, :])
    decay_chunk = jnp.broadcast_to(jnp.exp(log_gamma * C)[:, None], (n_heads, C))
    edge = jnp.broadcast_to(jnp.stack([w_start, w_end, decay_chunk], axis=1)[..., None],
                            (n_heads, 3, C, HEAD_DIM))

    per_step = RET_HEADS_PER_STEP
    width = per_step * HEAD_DIM
    n_groups = n_heads // per_step
    first = first_group * n_groups

    def col(group):
        return pl.BlockSpec((1, MIX_ROWS, width), lambda b, h, s: (b, s, first + group * n_groups + h))

    table = pl.BlockSpec((MIX_ROWS, HEAD_DIM), lambda b, h, s: (s, 0))
    return pl.pallas_call(
        _retention_kernel,
        grid=(bsz, n_groups, seq // MIX_ROWS),
        in_specs=[col(0), col(1), col(2), col(3), table, table,
                  pl.BlockSpec((per_step, C, C), lambda b, h, s: (h, 0, 0)),
                  pl.BlockSpec((per_step, 3, C, HEAD_DIM), lambda b, h, s: (h, 0, 0, 0)),
                  pl.BlockSpec((1, width), lambda b, h, s: (0, h))],
        out_specs=pl.BlockSpec((1, MIX_ROWS, width), lambda b, h, s: (b, s, h)),
        out_shape=jax.ShapeDtypeStruct((bsz, seq, d_r), BF16),
        scratch_shapes=[pltpu.VMEM((per_step, HEAD_DIM, HEAD_DIM), F32)],
        compiler_params=_params("parallel", "parallel", "arbitrary"),
        name="retention",
    )(proj, proj, proj, proj, cos, sin_signed, intra, edge, norm_w.reshape(1, d_r))


def _forget_scan_kernel(f_ref, bias_ref, c_ref, carry_ref):
    @pl.when(pl.program_id(1) == 0)
    def _():
        carry_ref[...] = jnp.zeros_like(carry_ref)

    log_f = _log_sigmoid(f_ref[0] + bias_ref[...])
    c = jnp.dot(_lower_tri(SCAN_ROWS), log_f, precision=lax.Precision.HIGHEST,
                preferred_element_type=F32) + carry_ref[...]
    c_ref[0] = c
    carry_ref[...] = c[SCAN_ROWS - 1:SCAN_ROWS, :]


def _forget_scan(f, bias):
    bsz, seq, lanes = f.shape
    return pl.pallas_call(
        _forget_scan_kernel,
        grid=(bsz, seq // SCAN_ROWS),
        in_specs=[pl.BlockSpec((1, SCAN_ROWS, lanes), lambda b, s: (b, s, 0)),
                  pl.BlockSpec((1, lanes), lambda b, s: (0, 0))],
        out_specs=pl.BlockSpec((1, SCAN_ROWS, lanes), lambda b, s: (b, s, 0)),
        out_shape=jax.ShapeDtypeStruct((bsz, seq, lanes), F32),
        scratch_shapes=[pltpu.VMEM((1, lanes), F32)],
        compiler_params=_params("parallel", "arbitrary"),
        name="forget_scan",
    )(f, bias)


def _split3(x):
    hi = x.astype(BF16).astype(F32)
    rest = x - hi
    mid = rest.astype(BF16).astype(F32)
    low = (rest - mid).astype(BF16).astype(F32)
    return hi, mid, low


def _fox_kernel(pairs_ref, q_ref, k_ref, v_ref, g_ref, ct_ref, o_ref,
                kaug_ref, vt_ref, qaug_ref, m_ref, acc_ref, s0_ref, s1_ref, p0_ref, p1_ref, a0_ref, a1_ref):
    T = FOX_BLOCK
    n_blocks = vt_ref.shape[0]
    n_below = n_blocks * (n_blocks - 1) // 2
    aug_row = lax.broadcasted_iota(jnp.int32, (HEAD_DIM, T), 0)

    def bias_rows(c_row, for_queries):
        hi, mid, low = _split3(c_row)
        if for_queries:
            pieces = jnp.where(aug_row == 3, hi, jnp.where(aug_row == 4, mid, jnp.where(aug_row == 5, low, 0.0)))
            return jnp.where(aug_row < 3, 1.0, pieces)
        pieces = jnp.where(aug_row == 0, -hi, jnp.where(aug_row == 1, -mid, jnp.where(aug_row == 2, -low, 0.0)))
        return jnp.where((aug_row >= 3) & (aug_row < 6), 1.0, pieces)

    for j in range(n_blocks):
        rows = slice(j * T, (j + 1) * T)
        c_j = ct_ref[0, 0, j:j + 1, :] * LOG2_E
        kaug_ref[rows, 0:HEAD_DIM] = k_ref[0, rows, :]
        kaug_ref[rows, HEAD_DIM:2 * HEAD_DIM] = bias_rows(c_j, False).T.astype(BF16)
        vt_ref[j, 0:HEAD_DIM, :] = v_ref[0, rows, :].astype(F32).T.astype(BF16)
        vt_ref[j, HEAD_DIM:, :] = jnp.ones((FOX_ONES_ROWS, T), BF16)
        qaug_ref[j, 0:HEAD_DIM, :] = (q_ref[0, rows, :].astype(F32) * (HEAD_DIM ** -0.5 * LOG2_E)).T.astype(BF16)
        qaug_ref[j, HEAD_DIM:2 * HEAD_DIM, :] = bias_rows(c_j, True).astype(BF16)
    m_ref[...] = jnp.full(m_ref.shape, MASK_VALUE, F32)
    acc_ref[...] = jnp.zeros(acc_ref.shape, F32)
    for p_ref, a_ref in ((p0_ref, a0_ref), (p1_ref, a1_ref)):
        p_ref[...] = jnp.zeros((T, T), BF16)
        a_ref[...] = jnp.ones((1, T), F32)

    key_id = lax.broadcasted_iota(jnp.int32, (T, T), 0)
    query_id = lax.broadcasted_iota(jnp.int32, (T, T), 1)

    def form_logits(pair, s_ref):
        qb, kb = pair
        keys = pl.ds(pl.multiple_of(kb * T, T), T)
        s_ref[...] = jnp.dot(kaug_ref[keys, :], qaug_ref[qb], preferred_element_type=F32)

    def softmax_update(pair, valid, diagonal, s_ref, p_ref, a_ref):
        qb, _ = pair
        s = s_ref[...]
        if diagonal:
            s = jnp.where(key_id <= query_id, s, MASK_VALUE)
        m = m_ref[qb]
        m_new = jnp.maximum(m, jnp.where(valid, jnp.max(s, axis=0, keepdims=True), MASK_VALUE))
        p_ref[...] = jnp.exp2(s - jnp.where(valid, m_new, -MASK_VALUE)).astype(BF16)
        a_ref[...] = jnp.exp2(m - m_new)
        m_ref[qb] = m_new

    def accumulate(pair, p_ref, a_ref):
        qb, kb = pair
        acc_ref[qb] = a_ref[...] * acc_ref[qb] + jnp.dot(vt_ref[kb], p_ref[...], preferred_element_type=F32)

    def sweep(count, pair_of, diagonal):
        if count == 0:
            return

        def clamped(n):
            return pair_of(jnp.clip(n, 0, count - 1))

        def tick(n, s_now, p_now, a_now, s_next, p_prev, a_prev):
            form_logits(clamped(n + 1), s_next)
            softmax_update(clamped(n), n < count, diagonal, s_now, p_now, a_now)
            accumulate(clamped(n - 1), p_prev, a_prev)

        def ticks(i, carry):
            for k in range(0, FOX_TICKS_PER_STEP, 2):
                n = FOX_TICKS_PER_STEP * i + k
                tick(n, s0_ref, p0_ref, a0_ref, s1_ref, p1_ref, a1_ref)
                tick(n + 1, s1_ref, p1_ref, a1_ref, s0_ref, p0_ref, a0_ref)
            return carry

        form_logits(clamped(0), s0_ref)
        lax.fori_loop(0, -(-(count + 2) // FOX_TICKS_PER_STEP), ticks, 0)

    sweep(n_blocks, lambda n: (n, n), True)
    sweep(n_below, lambda n: (pairs_ref[0, n], pairs_ref[1, n]), False)

    for j in range(n_blocks):
        rows = slice(j * T, (j + 1) * T)
        acc = acc_ref[j]
        o = (acc[:HEAD_DIM, :] / acc[HEAD_DIM:HEAD_DIM + 1, :]).T
        o_ref[0, rows, :] = (o * _sigmoid(g_ref[0, rows, :].astype(F32))).astype(o_ref.dtype)


def _fox(proj, c, n_heads):
    bsz, seq, _ = proj.shape
    T = FOX_BLOCK
    n_blocks = seq // T
    c_rows = jnp.transpose(c[:, :, :n_heads], (0, 2, 1)).reshape(bsz, n_heads, n_blocks, T)
    below = [(qb, kb) for qb in range(n_blocks) for kb in range(qb)] or [(0, 0)]
    pairs = jnp.asarray(below, jnp.int32).T

    def head_cols(group):
        return pl.BlockSpec((1, seq, HEAD_DIM), lambda b, h: (b, 0, group * n_heads + h))

    return pl.pallas_call(
        _fox_kernel,
        grid=(bsz, n_heads),
        in_specs=[pl.BlockSpec(memory_space=pltpu.SMEM),
                  head_cols(0), head_cols(1), head_cols(2), head_cols(3),
                  pl.BlockSpec((1, 1, n_blocks, T), lambda b, h: (b, h, 0, 0))],
        out_specs=pl.BlockSpec((1, seq, HEAD_DIM), lambda b, h: (b, 0, h)),
        out_shape=jax.ShapeDtypeStruct((bsz, seq, n_heads * HEAD_DIM), BF16),
        scratch_shapes=[pltpu.VMEM((seq, 2 * HEAD_DIM), BF16),
                        pltpu.VMEM((n_blocks, HEAD_DIM + FOX_ONES_ROWS, T), BF16),
                        pltpu.VMEM((n_blocks, 2 * HEAD_DIM, T), BF16),
                        pltpu.VMEM((n_blocks, 1, T), F32),
                        pltpu.VMEM((n_blocks, HEAD_DIM + FOX_ONES_ROWS, T), F32),
                        pltpu.VMEM((T, T), F32), pltpu.VMEM((T, T), F32),
                        pltpu.VMEM((T, T), BF16), pltpu.VMEM((T, T), BF16),
                        pltpu.VMEM((1, T), F32), pltpu.VMEM((1, T), F32)],
        compiler_params=_params("parallel", "parallel"),
        name="fox_attention",
    )(pairs, proj, proj, proj, proj, c_rows)


def _ffn_up_kernel(x_ref, wg_ref, wv_ref, cg_ref, cv_ref, o_ref, ug_ref, uv_ref, wgb_ref, wvb_ref,
                   *, seq_steps, d_ff):
    rows = x_ref.shape[0]

    @pl.when(pl.program_id(1) == 0)
    def _():
        valid = d_ff - pl.program_id(0) * wg_ref.shape[1]
        col = lax.broadcasted_iota(jnp.int32, wg_ref.shape, 1)
        wgb_ref[...] = jnp.where(col < valid, wg_ref[...], 0.0).astype(BF16)
        wvb_ref[...] = jnp.where(col < valid, wv_ref[...], 0.0).astype(BF16)

    @pl.when(pl.program_id(1) % seq_steps == 0)
    def _():
        ug_ref[0:SUBLANES, :] = jnp.zeros((SUBLANES, ug_ref.shape[1]), F32)
        uv_ref[0:SUBLANES, :] = jnp.zeros((SUBLANES, uv_ref.shape[1]), F32)

    x = x_ref[...]

    def conv(wb_ref, c_ref, u_ref):
        u_ref[SUBLANES:SUBLANES + rows, :] = jnp.dot(x, wb_ref[...], preferred_element_type=F32)
        out = c_ref[CONV_WIDTH - 1:CONV_WIDTH, :] * u_ref[SUBLANES:SUBLANES + rows, :]
        for tap in range(CONV_WIDTH - 1):
            back = CONV_WIDTH - 1 - tap
            out = out + c_ref[tap:tap + 1, :] * u_ref[SUBLANES - back:SUBLANES - back + rows, :]
        carry = u_ref[rows:rows + SUBLANES, :]
        u_ref[0:SUBLANES, :] = carry
        return out

    gate = conv(wgb_ref, cg_ref, ug_ref)
    val = conv(wvb_ref, cv_ref, uv_ref)
    o_ref[...] = (_silu(gate) * val).astype(o_ref.dtype)


def _ffn_up(h, w_up, conv_w, seq, d_ff, d_ff_pad):
    t, d = h.shape
    rows = min(MM_ROWS, seq)
    assert seq % rows == 0 and d_ff_pad % FFN_COLS == 0
    n_cols = d_ff_pad // FFN_COLS
    w_val = w_up[:, d_ff:]
    weight_tile = pl.BlockSpec((d, FFN_COLS), lambda j, i: (0, j))
    return pl.pallas_call(
        functools.partial(_ffn_up_kernel, seq_steps=seq // rows, d_ff=d_ff),
        grid=(n_cols, t // rows),
        in_specs=[pl.BlockSpec((rows, d), lambda j, i: (i, 0)),
                  weight_tile, weight_tile,
                  pl.BlockSpec((CONV_WIDTH, FFN_COLS), lambda j, i: (0, j)),
                  pl.BlockSpec((CONV_WIDTH, FFN_COLS), lambda j, i: (0, j + n_cols))],
        out_specs=pl.BlockSpec((rows, FFN_COLS), lambda j, i: (i, j)),
        out_shape=jax.ShapeDtypeStruct((t, d_ff_pad), BF16),
        scratch_shapes=[pltpu.VMEM((rows + SUBLANES, FFN_COLS), F32),
                        pltpu.VMEM((rows + SUBLANES, FFN_COLS), F32),
                        pltpu.VMEM((d, FFN_COLS), BF16),
                        pltpu.VMEM((d, FFN_COLS), BF16)],
        compiler_params=_params("parallel", "arbitrary"),
        name="ffn_up_conv",
    )(h, w_up, w_val, conv_w, conv_w)


def _pad_cols(w, width):
    return jnp.pad(w, ((0, 0), (0, width - w.shape[1])))


def kernel(x, attn_norm_w, ffn_norm_w, final_norm_w, even_w_in, hgrn_lb_logits, hgrn_norm_w, ret_norm_w,
           even_w_out, odd_w_in, odd_b_f, odd_w_out, ffn_w_up, ffn_conv_w, ffn_w_down):
    bsz, seq, d_model = x.shape
    depth = attn_norm_w.shape[0]
    tokens = bsz * seq
    d_hgrn = hgrn_norm_w.shape[1]
    d_fox = odd_w_out.shape[1]
    n_fox = odd_b_f.shape[1]
    d_ff = ffn_w_down.shape[1]
    d_ff_pad = FFN_COLS * math.ceil(d_ff / FFN_COLS)

    xf = x.reshape(tokens, d_model)
    h = _rmsnorm(xf, attn_norm_w[0], BF16)
    for layer in range(depth):
        j = layer // 2
        if layer % 2 == 0:
            w_in = even_w_in[j]
            proj = _matmul(h, w_in, F32, w_in.shape[1], cols=IN_COLS, name="even_in").reshape(bsz, seq, -1)
            o_a = _hgrn(proj, hgrn_lb_logits, hgrn_norm_w[j], j)
            o_b = _retention(proj, ret_norm_w[j], first_group=4 * d_hgrn // ret_norm_w.shape[1])
            o = jnp.concatenate([o_a, o_b], axis=-1).reshape(tokens, -1)
            w_out = even_w_out[j]
        else:
            w_in = odd_w_in[j]
            proj = _matmul(h, w_in, BF16, 4 * d_fox, cols=IN_COLS, name="fox_in").reshape(bsz, seq, -1)
            f = _matmul(h, w_in, F32, LANES, cols=LANES, first_col=4 * d_fox, name="fox_forget")
            c = _forget_scan(f.reshape(bsz, seq, LANES), _pad_cols(odd_b_f[j][None, :], LANES))
            o = _fox(proj, c, n_fox).reshape(tokens, -1)
            w_out = odd_w_out[j]
        xf, h = _matmul_residual_norm(o, w_out.astype(BF16), xf, ffn_norm_w[layer], BF16, rows=OUT_ROWS,
                                      name="mixer_out_norm")
        conv_w = ffn_conv_w[layer]
        conv_w = jnp.concatenate([_pad_cols(conv_w[:, :d_ff], d_ff_pad), _pad_cols(conv_w[:, d_ff:], d_ff_pad)],
                                 axis=1)
        act = _ffn_up(h, ffn_w_up[layer], conv_w, seq, d_ff, d_ff_pad)
        w_down = jnp.pad(ffn_w_down[layer], ((0, d_ff_pad - d_ff), (0, 0))).astype(BF16)
        last = layer == depth - 1
        xf, h = _matmul_residual_norm(act, w_down, xf, final_norm_w if last else attn_norm_w[layer + 1],
                                      x.dtype if last else BF16, rows=DOWN_ROWS, keep_sum=not last,
                                      name="ffn_down_norm")
    return h.reshape(bsz, seq, d_model)
```

```python
import functools
import math

import jax
import jax.numpy as jnp
from jax import lax
from jax.experimental import pallas as pl
from jax.experimental.pallas import tpu as pltpu

F32 = jnp.float32
BF16 = jnp.bfloat16

HEAD_DIM = 128
ROPE_BASE = 10000.0
RET_DECAY_BASE = 5.0
EPS = 1e-6
LB_FLOOR = 1e-30
MASK_VALUE = -1e30
CONV_WIDTH = 3
LOG2_E = math.log2(math.e)

LANES = 128
SUBLANES = 8
VMEM_LIMIT_BYTES = 48 * 1024 * 1024

NORM_ROWS = 256
MM_ROWS = 1024
MM_COLS = 512
IN_COLS = 1024
OUT_ROWS = 512
DOWN_ROWS = 256
FFN_COLS = 512
MIX_ROWS = 512
MIX_UNROLL = 2
HGRN_CHUNK = 64
HGRN_SUB = 16
HGRN_HEADS_PER_STEP = 8
RET_CHUNK = 128
RET_HEADS_PER_STEP = 4
FOX_BLOCK = 512
FOX_TICKS_PER_STEP = 4
FOX_ONES_ROWS = 16
SCAN_ROWS = 256

NT_DIMS = (((1,), (1,)), ((), ()))
TN_DIMS = (((0,), (0,)), ((), ()))


def _params(*semantics):
    return pltpu.CompilerParams(dimension_semantics=semantics, vmem_limit_bytes=VMEM_LIMIT_BYTES)


def _sigmoid(x):
    return 1.0 / (1.0 + jnp.exp(-x))


def _silu(x):
    return x * _sigmoid(x)


def _log_sigmoid(x):
    return jnp.minimum(x, 0.0) - jnp.log1p(jnp.exp(-jnp.abs(x)))


def _lower_tri(n):
    rows = lax.broadcasted_iota(jnp.int32, (n, n), 0)
    cols = lax.broadcasted_iota(jnp.int32, (n, n), 1)
    return (rows >= cols).astype(F32)


def _rmsnorm_kernel(x_ref, w_ref, o_ref):
    x = x_ref[...]
    ms = jnp.mean(x * x, axis=-1, keepdims=True)
    o_ref[...] = (x * lax.rsqrt(ms + EPS) * w_ref[...]).astype(o_ref.dtype)


def _rmsnorm(x, w, out_dtype):
    t, d = x.shape
    return pl.pallas_call(
        _rmsnorm_kernel,
        grid=(t // NORM_ROWS,),
        in_specs=[pl.BlockSpec((NORM_ROWS, d), lambda i: (i, 0)),
                  pl.BlockSpec((1, d), lambda i: (0, 0))],
        out_specs=pl.BlockSpec((NORM_ROWS, d), lambda i: (i, 0)),
        out_shape=jax.ShapeDtypeStruct((t, d), out_dtype),
        compiler_params=_params("parallel"),
        name="rmsnorm",
    )(x, w.reshape(1, d))


def _cast_weight_tile(w_ref, wb_ref, valid_cols):
    w = w_ref[...]
    if valid_cols < w.shape[1]:
        col = lax.broadcasted_iota(jnp.int32, w.shape, 1)
        w = jnp.where(col < valid_cols, w, 0.0)
    wb_ref[...] = w.astype(BF16)


def _matmul_kernel(x_ref, w_ref, o_ref, wb_ref, *, valid_cols):
    @pl.when(pl.program_id(1) == 0)
    def _():
        _cast_weight_tile(w_ref, wb_ref, valid_cols)

    o_ref[...] = jnp.dot(x_ref[...], wb_ref[...], preferred_element_type=F32).astype(o_ref.dtype)


def _matmul(x, w, layer, out_dtype, n_out, rows=MM_ROWS, cols=MM_COLS, first_col=0, name="matmul"):
    t, k = x.shape
    rows, cols = min(rows, t), min(cols, n_out)
    assert t % rows == 0 and n_out % cols == 0 and first_col % cols == 0
    first_block = first_col // cols
    valid_cols = min(cols, w.shape[2] - first_col) if n_out == cols else cols
    assert first_col + n_out <= w.shape[2] or n_out == cols
    return pl.pallas_call(
        functools.partial(_matmul_kernel, valid_cols=valid_cols),
        grid=(n_out // cols, t // rows),
        in_specs=[pl.BlockSpec((rows, k), lambda j, i: (i, 0)),
                  pl.BlockSpec((None, k, cols), lambda j, i: (layer, 0, first_block + j))],
        out_specs=pl.BlockSpec((rows, cols), lambda j, i: (i, j)),
        out_shape=jax.ShapeDtypeStruct((t, n_out), out_dtype),
        scratch_shapes=[pltpu.VMEM((k, cols), BF16)],
        compiler_params=_params("parallel", "arbitrary"),
        name=name,
    )(x, w)


def _residual_norm_kernel(x_ref, w_ref, r_ref, nw_ref, *out_refs):
    y = r_ref[...] + jnp.dot(x_ref[...], w_ref[...], preferred_element_type=F32)
    if len(out_refs) == 2:
        out_refs[0][...] = y
    h_ref = out_refs[-1]
    ms = jnp.mean(y * y, axis=-1, keepdims=True)
    h_ref[...] = (y * lax.rsqrt(ms + EPS) * nw_ref[...]).astype(h_ref.dtype)


def _matmul_residual_norm(x, w, residual, norm_w, h_dtype, rows, keep_sum=True, name="matmul_residual_norm"):
    t, k = x.shape
    n = w.shape[1]
    assert t % rows == 0
    row_block = pl.BlockSpec((rows, n), lambda i: (i, 0))
    out_shape = [jax.ShapeDtypeStruct((t, n), h_dtype)]
    if keep_sum:
        out_shape.insert(0, jax.ShapeDtypeStruct((t, n), F32))
    outs = pl.pallas_call(
        _residual_norm_kernel,
        grid=(t // rows,),
        in_specs=[pl.BlockSpec((rows, k), lambda i: (i, 0)),
                  pl.BlockSpec((k, n), lambda i: (0, 0), pipeline_mode=pl.Buffered(1)),
                  row_block,
                  pl.BlockSpec((1, n), lambda i: (0, 0))],
        out_specs=[row_block] * len(out_shape),
        out_shape=out_shape,
        compiler_params=_params("parallel"),
        name=name,
    )(x, w, residual, norm_w.reshape(1, n))
    return (outs[0], outs[1]) if keep_sum else (None, outs[0])


def _hgrn_kernel(q_ref, f_ref, i_ref, g_ref, lbl_ref, nw_ref, tri3_ref, sel_ref, mask_ref, o_ref, state_ref,
                 *, layer_j):
    C, L = HGRN_CHUNK, HGRN_SUB
    n_sub = C // L

    @pl.when(pl.program_id(2) == 0)
    def _():
        state_ref[...] = jnp.zeros_like(state_ref)

    lbl = lbl_ref[...]
    e = jnp.exp(lbl - jnp.max(lbl, axis=0, keepdims=True))
    sm = e / jnp.sum(e, axis=0, keepdims=True)
    lb_all = jnp.zeros((1, lbl.shape[1]), F32)
    for r in range(1, layer_j + 1):
        lb_all = lb_all + sm[r:r + 1, :]

    def pair_products(q, k, b, s):
        first = (s // SUBLANES) * SUBLANES
        q3, k3, b3 = (x.reshape(n_sub, L, HEAD_DIM) for x in (q, k, b))
        decay = jnp.exp2(jnp.minimum(b3[:, first:, :] - b3[:, s:s + 1, :], 0.0))
        prod = q3[:, first:, :] * k3[:, s:s + 1, :] * decay
        if first:
            prod = jnp.concatenate([jnp.zeros((n_sub, first, HEAD_DIM), F32), prod], axis=1)
        return prod.reshape(C, HEAD_DIM).astype(BF16)

    def head_chunk(rows, lanes, head):
        lb_floor, one_m_lb = jnp.maximum(lb_all[:, lanes], LB_FLOOR), 1.0 - lb_all[:, lanes]
        z = f_ref[0, rows, lanes]
        q = _silu(q_ref[0, rows, lanes])
        v = i_ref[0, rows, lanes].astype(BF16)

        e_z = jnp.exp(-jnp.abs(z))
        inv = 1.0 / (1.0 + e_z)
        log_f = jnp.log(lb_floor + one_m_lb * (jnp.where(z >= 0.0, 1.0, e_z) * inv))
        k = one_m_lb * (jnp.where(z >= 0.0, e_z, 1.0) * inv)
        hi = log_f.astype(BF16)
        rest = log_f - hi.astype(F32)
        mid = rest.astype(BF16)
        low = (rest - mid.astype(F32)).astype(BF16)
        b = jnp.dot(tri3_ref[...], jnp.concatenate([hi, mid, low], axis=0),
                    preferred_element_type=F32) * LOG2_E
        b_last = b[C - 1:C, :]

        state = state_ref[head]
        o = lax.dot_general((q * jnp.exp2(b)).astype(BF16), state.astype(BF16), NT_DIMS,
                            preferred_element_type=F32)

        below = [jnp.zeros((L, C), F32)]
        for i in range(1, n_sub):
            lo = i * L
            edge = b[lo - 1:lo, :]
            q_edge = (q[lo:lo + L, :] * jnp.exp2(b[lo:lo + L, :] - edge)).astype(BF16)
            k_edge = (k[:lo, :] * jnp.exp2(edge - b[:lo, :])).astype(BF16)
            k_edge = jnp.concatenate([k_edge, jnp.zeros((C - lo, HEAD_DIM), BF16)], axis=0)
            below.append(lax.dot_general(q_edge, k_edge, NT_DIMS, preferred_element_type=F32))
        pair = jnp.concatenate([pair_products(q, k, b, s) for s in range(L)], axis=1)
        diag = jnp.dot(pair, sel_ref[...], preferred_element_type=F32)
        scores = jnp.concatenate(below, axis=0) + diag * mask_ref[...]
        o = o + jnp.dot(scores.astype(BF16), v, preferred_element_type=F32)

        k_end = (k * jnp.exp2(b_last - b)).astype(BF16)
        state_ref[head] = state * jnp.exp2(b_last) + lax.dot_general(v, k_end, TN_DIMS,
                                                                     preferred_element_type=F32)

        y = o * lax.rsqrt(jnp.mean(o * o, axis=-1, keepdims=True) + EPS) * nw_ref[:, lanes]
        o_ref[0, rows, lanes] = (y * _silu(g_ref[0, rows, lanes])).astype(o_ref.dtype)

    def chunk(c, carry):
        rows = pl.ds(pl.multiple_of(c * C, C), C)
        for head in range(HGRN_HEADS_PER_STEP):
            head_chunk(rows, slice(head * HEAD_DIM, (head + 1) * HEAD_DIM), head)
        return carry

    lax.fori_loop(0, MIX_ROWS // C, chunk, 0, unroll=MIX_UNROLL)


def _hgrn(proj, lb_logits, norm_w, layer_j):
    bsz, seq, _ = proj.shape
    d_h = norm_w.shape[0]
    width = HGRN_HEADS_PER_STEP * HEAD_DIM
    n_groups = d_h // width
    n_layers = lb_logits.shape[0]

    C, L = HGRN_CHUNK, HGRN_SUB
    pos = jnp.arange(C)
    causal = pos[:, None] >= pos[None, :]
    tri3 = jnp.tile(causal.astype(BF16), (1, 3))
    pair_sel = (jnp.repeat(jnp.arange(L), HEAD_DIM)[:, None] == (pos % L)[None, :]).astype(BF16)
    diag_mask = (causal & (pos[:, None] // L == pos[None, :] // L)).astype(F32)

    def col(group):
        return pl.BlockSpec((1, MIX_ROWS, width), lambda b, h, s: (b, s, group * n_groups + h))

    def whole(a):
        return pl.BlockSpec(a.shape, lambda b, h, s: (0, 0))

    return pl.pallas_call(
        functools.partial(_hgrn_kernel, layer_j=layer_j),
        grid=(bsz, n_groups, seq // MIX_ROWS),
        in_specs=[col(0), col(1), col(2), col(3),
                  pl.BlockSpec((n_layers, width), lambda b, h, s: (0, h)),
                  pl.BlockSpec((1, width), lambda b, h, s: (0, h)),
                  whole(tri3), whole(pair_sel), whole(diag_mask)],
        out_specs=pl.BlockSpec((1, MIX_ROWS, width), lambda b, h, s: (b, s, h)),
        out_shape=jax.ShapeDtypeStruct((bsz, seq, d_h), BF16),
        scratch_shapes=[pltpu.VMEM((HGRN_HEADS_PER_STEP, HEAD_DIM, HEAD_DIM), F32)],
        compiler_params=_params("parallel", "parallel", "arbitrary"),
        name="hgrn2",
    )(proj, proj, proj, proj, lb_logits, norm_w.reshape(1, d_h), tri3, pair_sel, diag_mask)


def _retention_kernel(q_ref, k_ref, v_ref, g_ref, cos_ref, sin_ref, intra_ref, edge_ref, nw_ref, o_ref, state_ref):
    C = RET_CHUNK

    @pl.when(pl.program_id(2) == 0)
    def _():
        state_ref[...] = jnp.zeros_like(state_ref)

    def rope(x, cos, sin_signed):
        return x * cos + pltpu.roll(x, HEAD_DIM // 2, 1) * sin_signed

    def head_chunk(sl, lanes, head):
        cos, sin_signed = cos_ref[sl, :], sin_ref[sl, :]
        q = rope(q_ref[0, sl, lanes], cos, sin_signed)
        k = rope(k_ref[0, sl, lanes], cos, sin_signed) * (HEAD_DIM ** -0.5)
        v = v_ref[0, sl, lanes].astype(BF16)
        state = state_ref[head]
        w_start, w_end, decay_chunk = edge_ref[head, 0], edge_ref[head, 1], edge_ref[head, 2, 0:1, :]

        scores = lax.dot_general(q.astype(BF16), k.astype(BF16), NT_DIMS,
                                 preferred_element_type=F32) * intra_ref[head]
        o = jnp.dot(scores.astype(BF16), v, preferred_element_type=F32)
        o = o + lax.dot_general((q * w_start).astype(BF16), state.astype(BF16), NT_DIMS,
                                preferred_element_type=F32)
        state_ref[head] = decay_chunk * state + lax.dot_general(v, (k * w_end).astype(BF16), TN_DIMS,
                                                                preferred_element_type=F32)

        cen = o - jnp.mean(o, axis=-1, keepdims=True)
        y = cen * lax.rsqrt(jnp.mean(cen * cen, axis=-1, keepdims=True) + EPS) * nw_ref[:, lanes]
        o_ref[0, sl, lanes] = (y * _silu(g_ref[0, sl, lanes])).astype(o_ref.dtype)

    def chunk(c, carry):
        sl = pl.ds(pl.multiple_of(c * C, C), C)
        for head in range(RET_HEADS_PER_STEP):
            head_chunk(sl, slice(head * HEAD_DIM, (head + 1) * HEAD_DIM), head)
        return carry

    lax.fori_loop(0, MIX_ROWS // C, chunk, 0, unroll=MIX_UNROLL)


def _retention(proj, norm_w, first_group):
    bsz, seq, _ = proj.shape
    d_r = norm_w.shape[0]
    n_heads = d_r // HEAD_DIM
    half = HEAD_DIM // 2

    inv = ROPE_BASE ** (-jnp.arange(0, HEAD_DIM, 2, dtype=F32) / HEAD_DIM)
    ang = jnp.arange(seq).astype(F32)[:, None] * inv[None, :]
    cos = jnp.concatenate([jnp.cos(ang), jnp.cos(ang)], axis=-1)
    sin_signed = jnp.concatenate([-jnp.sin(ang), jnp.sin(ang)], axis=-1)
    assert cos.shape == (seq, 2 * half)
    C = RET_CHUNK
    log_gamma = jnp.log1p(-jnp.exp2(-RET_DECAY_BASE - jnp.arange(n_heads, dtype=F32)))
    pos = jnp.arange(C, dtype=F32)
    rel = pos[:, None] - pos[None, :]
    intra = jnp.where(rel >= 0, jnp.exp(log_gamma[:, None, None] * jnp.maximum(rel, 0.0)), 0.0)
    w_start = jnp.exp(log_gamma[:, None] * (pos + 1.0)[None, :])
    w_end = jnp.exp(log_gamma[:, None] * (C - 1.0 - pos)[None, :])
    decay_chunk = jnp.broadcast_to(jnp.exp(log_gamma * C)[:, None], (n_heads, C))
    edge = jnp.broadcast_to(jnp.stack([w_start, w_end, decay_chunk], axis=1)[..., None],
                            (n_heads, 3, C, HEAD_DIM))

    per_step = RET_HEADS_PER_STEP
    width = per_step * HEAD_DIM
    n_groups = n_heads // per_step
    first = first_group * n_groups

    def col(group):
        return pl.BlockSpec((1, MIX_ROWS, width), lambda b, h, s: (b, s, first + group * n_groups + h))

    table = pl.BlockSpec((MIX_ROWS, HEAD_DIM), lambda b, h, s: (s, 0))
    return pl.pallas_call(
        _retention_kernel,
        grid=(bsz, n_groups, seq // MIX_ROWS),
        in_specs=[col(0), col(1), col(2), col(3), table, table,
                  pl.BlockSpec((per_step, C, C), lambda b, h, s: (h, 0, 0)),
                  pl.BlockSpec((per_step, 3, C, HEAD_DIM), lambda b, h, s: (h, 0, 0, 0)),
                  pl.BlockSpec((1, width), lambda b, h, s: (0, h))],
        out_specs=pl.BlockSpec((1, MIX_ROWS, width), lambda b, h, s: (b, s, h)),
        out_shape=jax.ShapeDtypeStruct((bsz, seq, d_r), BF16),
        scratch_shapes=[pltpu.VMEM((per_step, HEAD_DIM, HEAD_DIM), F32)],
        compiler_params=_params("parallel", "parallel", "arbitrary"),
        name="retention",
    )(proj, proj, proj, proj, cos, sin_signed, intra, edge, norm_w.reshape(1, d_r))


def _forget_scan_kernel(f_ref, bias_ref, c_ref, carry_ref):
    @pl.when(pl.program_id(1) == 0)
    def _():
        carry_ref[...] = jnp.zeros_like(carry_ref)

    log_f = _log_sigmoid(f_ref[0] + bias_ref[...])
    c = jnp.dot(_lower_tri(SCAN_ROWS), log_f, precision=lax.Precision.HIGHEST,
                preferred_element_type=F32) + carry_ref[...]
    c_ref[0] = c
    carry_ref[...] = c[SCAN_ROWS - 1:SCAN_ROWS, :]


def _forget_scan(f, bias):
    bsz, seq, lanes = f.shape
    return pl.pallas_call(
        _forget_scan_kernel,
        grid=(bsz, seq // SCAN_ROWS),
        in_specs=[pl.BlockSpec((1, SCAN_ROWS, lanes), lambda b, s: (b, s, 0)),
                  pl.BlockSpec((1, lanes), lambda b, s: (0, 0))],
        out_specs=pl.BlockSpec((1, SCAN_ROWS, lanes), lambda b, s: (b, s, 0)),
        out_shape=jax.ShapeDtypeStruct((bsz, seq, lanes), F32),
        scratch_shapes=[pltpu.VMEM((1, lanes), F32)],
        compiler_params=_params("parallel", "arbitrary"),
        name="forget_scan",
    )(f, bias)


def _split3(x):
    hi = x.astype(BF16).astype(F32)
    rest = x - hi
    mid = rest.astype(BF16).astype(F32)
    low = (rest - mid).astype(BF16).astype(F32)
    return hi, mid, low


def _fox_kernel(pairs_ref, q_ref, k_ref, v_ref, g_ref, ct_ref, o_ref,
                kaug_ref, vt_ref, qaug_ref, m_ref, acc_ref, s0_ref, s1_ref, p0_ref, p1_ref, a0_ref, a1_ref):
    T = FOX_BLOCK
    n_blocks = vt_ref.shape[0]
    n_pairs = n_blocks * (n_blocks + 1) // 2
    aug_row = lax.broadcasted_iota(jnp.int32, (HEAD_DIM, T), 0)

    def bias_rows(c_row, for_queries):
        hi, mid, low = _split3(c_row)
        if for_queries:
            pieces = jnp.where(aug_row == 3, hi, jnp.where(aug_row == 4, mid, jnp.where(aug_row == 5, low, 0.0)))
            return jnp.where(aug_row < 3, 1.0, pieces)
        pieces = jnp.where(aug_row == 0, -hi, jnp.where(aug_row == 1, -mid, jnp.where(aug_row == 2, -low, 0.0)))
        return jnp.where((aug_row >= 3) & (aug_row < 6), 1.0, pieces)

    for j in range(n_blocks):
        rows = slice(j * T, (j + 1) * T)
        c_j = ct_ref[0, 0, j:j + 1, :] * LOG2_E
        kaug_ref[rows, 0:HEAD_DIM] = k_ref[0, rows, :]
        kaug_ref[rows, HEAD_DIM:2 * HEAD_DIM] = bias_rows(c_j, False).T.astype(BF16)
        vt_ref[j, 0:HEAD_DIM, :] = v_ref[0, rows, :].astype(F32).T.astype(BF16)
        vt_ref[j, HEAD_DIM:, :] = jnp.ones((FOX_ONES_ROWS, T), BF16)
        qaug_ref[j, 0:HEAD_DIM, :] = (q_ref[0, rows, :].astype(F32) * (HEAD_DIM ** -0.5 * LOG2_E)).T.astype(BF16)
        qaug_ref[j, HEAD_DIM:2 * HEAD_DIM, :] = bias_rows(c_j, True).astype(BF16)
    m_ref[...] = jnp.full(m_ref.shape, MASK_VALUE, F32)
    acc_ref[...] = jnp.zeros(acc_ref.shape, F32)
    for p_ref, a_ref in ((p0_ref, a0_ref), (p1_ref, a1_ref)):
        p_ref[...] = jnp.zeros((T, T), BF16)
        a_ref[...] = jnp.ones((1, T), F32)

    key_id = lax.broadcasted_iota(jnp.int32, (T, T), 0)
    query_id = lax.broadcasted_iota(jnp.int32, (T, T), 1)

    def form_logits(pair, s_ref):
        qb, kb = pair
        keys = pl.ds(pl.multiple_of(kb * T, T), T)
        s_ref[...] = jnp.dot(kaug_ref[keys, :], qaug_ref[qb], preferred_element_type=F32)

    def softmax_update(pair, valid, diagonal, s_ref, p_ref, a_ref):
        qb, _ = pair
        s = s_ref[...]
        if diagonal:
            s = jnp.where(key_id <= query_id, s, MASK_VALUE)
        m = m_ref[qb]
        m_new = jnp.maximum(m, jnp.where(valid, jnp.max(s, axis=0, keepdims=True), MASK_VALUE))
        p_ref[...] = jnp.exp2(s - jnp.where(valid, m_new, -MASK_VALUE)).astype(BF16)
        a_ref[...] = jnp.exp2(m - m_new)
        m_ref[qb] = m_new

    def accumulate(pair, p_ref, a_ref):
        qb, kb = pair
        acc_ref[qb] = a_ref[...] * acc_ref[qb] + jnp.dot(vt_ref[kb], p_ref[...], preferred_element_type=F32)

    def pair_at(n):
        n = jnp.clip(n, 0, n_pairs - 1)
        return pairs_ref[0, n], pairs_ref[1, n]

    def tick(n, diagonal, s_now, p_now, a_now, s_next, p_prev, a_prev):
        form_logits(pair_at(n + 1), s_next)
        softmax_update(pair_at(n), n < n_pairs, diagonal, s_now, p_now, a_now)
        accumulate(pair_at(n - 1), p_prev, a_prev)

    def step(i, is_diagonal):
        for k in range(0, FOX_TICKS_PER_STEP, 2):
            n = FOX_TICKS_PER_STEP * i + k
            tick(n, is_diagonal(n), s0_ref, p0_ref, a0_ref, s1_ref, p1_ref, a1_ref)
            tick(n + 1, is_diagonal(n + 1), s1_ref, p1_ref, a1_ref, s0_ref, p0_ref, a0_ref)

    def diagonal_step(i, carry):
        step(i, lambda n: True)
        return carry

    def below_step(i, carry):
        step(i, lambda n: False)
        return carry

    n_steps = -(-(n_pairs + 2) // FOX_TICKS_PER_STEP)
    diagonal_steps = n_blocks // FOX_TICKS_PER_STEP
    form_logits(pair_at(0), s0_ref)
    lax.fori_loop(0, diagonal_steps, diagonal_step, 0)
    if n_blocks % FOX_TICKS_PER_STEP:
        step(diagonal_steps, lambda n: n < n_blocks)
        diagonal_steps += 1
    lax.fori_loop(diagonal_steps, n_steps, below_step, 0)

    for j in range(n_blocks):
        rows = slice(j * T, (j + 1) * T)
        acc = acc_ref[j]
        o = (acc[:HEAD_DIM, :] / acc[HEAD_DIM:HEAD_DIM + 1, :]).T
        o_ref[0, rows, :] = (o * _sigmoid(g_ref[0, rows, :].astype(F32))).astype(o_ref.dtype)


def _fox(proj, c, n_heads):
    bsz, seq, _ = proj.shape
    T = FOX_BLOCK
    n_blocks = seq // T
    c_rows = jnp.transpose(c[:, :, :n_heads], (0, 2, 1)).reshape(bsz, n_heads, n_blocks, T)
    causal_pairs = ([(qb, qb) for qb in range(n_blocks)]
                    + [(qb, kb) for qb in range(n_blocks) for kb in range(qb)])
    pairs = jnp.asarray(causal_pairs, jnp.int32).T

    def head_cols(group):
        return pl.BlockSpec((1, seq, HEAD_DIM), lambda b, h: (b, 0, group * n_heads + h))

    return pl.pallas_call(
        _fox_kernel,
        grid=(bsz, n_heads),
        in_specs=[pl.BlockSpec(memory_space=pltpu.SMEM),
                  head_cols(0), head_cols(1), head_cols(2), head_cols(3),
                  pl.BlockSpec((1, 1, n_blocks, T), lambda b, h: (b, h, 0, 0))],
        out_specs=pl.BlockSpec((1, seq, HEAD_DIM), lambda b, h: (b, 0, h)),
        out_shape=jax.ShapeDtypeStruct((bsz, seq, n_heads * HEAD_DIM), BF16),
        scratch_shapes=[pltpu.VMEM((seq, 2 * HEAD_DIM), BF16),
                        pltpu.VMEM((n_blocks, HEAD_DIM + FOX_ONES_ROWS, T), BF16),
                        pltpu.VMEM((n_blocks, 2 * HEAD_DIM, T), BF16),
                        pltpu.VMEM((n_blocks, 1, T), F32),
                        pltpu.VMEM((n_blocks, HEAD_DIM + FOX_ONES_ROWS, T), F32),
                        pltpu.VMEM((T, T), F32), pltpu.VMEM((T, T), F32),
                        pltpu.VMEM((T, T), BF16), pltpu.VMEM((T, T), BF16),
                        pltpu.VMEM((1, T), F32), pltpu.VMEM((1, T), F32)],
        compiler_params=_params("parallel", "parallel"),
        name="fox_attention",
    )(pairs, proj, proj, proj, proj, c_rows)


def _ffn_up_kernel(x_ref, wg_ref, wv_ref, cg_ref, cv_ref, o_ref, ug_ref, uv_ref, wgb_ref, wvb_ref,
                   *, seq_steps, d_ff):
    rows = x_ref.shape[0]

    @pl.when(pl.program_id(1) == 0)
    def _():
        valid = d_ff - pl.program_id(0) * wg_ref.shape[1]
        col = lax.broadcasted_iota(jnp.int32, wg_ref.shape, 1)
        wgb_ref[...] = jnp.where(col < valid, wg_ref[...], 0.0).astype(BF16)
        wvb_ref[...] = jnp.where(col < valid, wv_ref[...], 0.0).astype(BF16)

    @pl.when(pl.program_id(1) % seq_steps == 0)
    def _():
        ug_ref[0:SUBLANES, :] = jnp.zeros((SUBLANES, ug_ref.shape[1]), F32)
        uv_ref[0:SUBLANES, :] = jnp.zeros((SUBLANES, uv_ref.shape[1]), F32)

    x = x_ref[...]

    def conv(wb_ref, c_ref, u_ref):
        u_ref[SUBLANES:SUBLANES + rows, :] = jnp.dot(x, wb_ref[...], preferred_element_type=F32)
        out = c_ref[CONV_WIDTH - 1:CONV_WIDTH, :] * u_ref[SUBLANES:SUBLANES + rows, :]
        for tap in range(CONV_WIDTH - 1):
            back = CONV_WIDTH - 1 - tap
            out = out + c_ref[tap:tap + 1, :] * u_ref[SUBLANES - back:SUBLANES - back + rows, :]
        carry = u_ref[rows:rows + SUBLANES, :]
        u_ref[0:SUBLANES, :] = carry
        return out

    gate = conv(wgb_ref, cg_ref, ug_ref)
    val = conv(wvb_ref, cv_ref, uv_ref)
    o_ref[...] = (_silu(gate) * val).astype(o_ref.dtype)


def _ffn_up(h, w_up, w_val, layer, conv_w, seq, d_ff, d_ff_pad):
    t, d = h.shape
    rows = min(MM_ROWS, seq)
    assert seq % rows == 0 and d_ff_pad % FFN_COLS == 0
    n_cols = d_ff_pad // FFN_COLS
    weight_tile = pl.BlockSpec((None, d, FFN_COLS), lambda j, i: (layer, 0, j))
    return pl.pallas_call(
        functools.partial(_ffn_up_kernel, seq_steps=seq // rows, d_ff=d_ff),
        grid=(n_cols, t // rows),
        in_specs=[pl.BlockSpec((rows, d), lambda j, i: (i, 0)),
                  weight_tile, weight_tile,
                  pl.BlockSpec((CONV_WIDTH, FFN_COLS), lambda j, i: (0, j)),
                  pl.BlockSpec((CONV_WIDTH, FFN_COLS), lambda j, i: (0, j + n_cols))],
        out_specs=pl.BlockSpec((rows, FFN_COLS), lambda j, i: (i, j)),
        out_shape=jax.ShapeDtypeStruct((t, d_ff_pad), BF16),
        scratch_shapes=[pltpu.VMEM((rows + SUBLANES, FFN_COLS), F32),
                        pltpu.VMEM((rows + SUBLANES, FFN_COLS), F32),
                        pltpu.VMEM((d, FFN_COLS), BF16),
                        pltpu.VMEM((d, FFN_COLS), BF16)],
        compiler_params=_params("parallel", "arbitrary"),
        name="ffn_up_conv",
    )(h, w_up, w_val, conv_w, conv_w)


def _pad_cols(w, width):
    return jnp.pad(w, ((0, 0), (0, width - w.shape[1])))


def kernel(x, attn_norm_w, ffn_norm_w, final_norm_w, even_w_in, hgrn_lb_logits, hgrn_norm_w, ret_norm_w,
           even_w_out, odd_w_in, odd_b_f, odd_w_out, ffn_w_up, ffn_conv_w, ffn_w_down):
    bsz, seq, d_model = x.shape
    depth = attn_norm_w.shape[0]
    tokens = bsz * seq
    d_hgrn = hgrn_norm_w.shape[1]
    d_fox = odd_w_out.shape[1]
    n_fox = odd_b_f.shape[1]
    d_ff = ffn_w_down.shape[1]
    d_ff_pad = FFN_COLS * math.ceil(d_ff / FFN_COLS)

    ffn_w_val = ffn_w_up[:, :, d_ff:]
    xf = x.reshape(tokens, d_model)
    h = _rmsnorm(xf, attn_norm_w[0], BF16)
    for layer in range(depth):
        j = layer // 2
        if layer % 2 == 0:
            proj = _matmul(h, even_w_in, j, F32, even_w_in.shape[2], cols=IN_COLS,
                           name="even_in").reshape(bsz, seq, -1)
            o_a = _hgrn(proj, hgrn_lb_logits, hgrn_norm_w[j], j)
            o_b = _retention(proj, ret_norm_w[j], first_group=4 * d_hgrn // ret_norm_w.shape[1])
            o = jnp.concatenate([o_a, o_b], axis=-1).reshape(tokens, -1)
            w_out = even_w_out[j]
        else:
            proj = _matmul(h, odd_w_in, j, BF16, 4 * d_fox, cols=IN_COLS, name="fox_in").reshape(bsz, seq, -1)
            f = _matmul(h, odd_w_in, j, F32, LANES, cols=LANES, first_col=4 * d_fox, name="fox_forget")
            c = _forget_scan(f.reshape(bsz, seq, LANES), _pad_cols(odd_b_f[j][None, :], LANES))
            o = _fox(proj, c, n_fox).reshape(tokens, -1)
            w_out = odd_w_out[j]
        xf, h = _matmul_residual_norm(o, w_out.astype(BF16), xf, ffn_norm_w[layer], BF16, rows=OUT_ROWS,
                                      name="mixer_out_norm")
        conv_w = ffn_conv_w[layer]
        conv_w = jnp.concatenate([_pad_cols(conv_w[:, :d_ff], d_ff_pad), _pad_cols(conv_w[:, d_ff:], d_ff_pad)],
                                 axis=1)
        act = _ffn_up(h, ffn_w_up, ffn_w_val, layer, conv_w, seq, d_ff, d_ff_pad)
        w_down = jnp.pad(ffn_w_down[layer], ((0, d_ff_pad - d_ff), (0, 0))).astype(BF16)
        last = layer == depth - 1
        xf, h = _matmul_residual_norm(act, w_down, xf, final_norm_w if last else attn_norm_w[layer + 1],
                                      x.dtype if last else BF16, rows=DOWN_ROWS, keep_sum=not last,
                                      name="ffn_down_norm")
    return h.reshape(bsz, seq, d_model)
```

```python
import functools
import math

import jax
import jax.numpy as jnp
from jax import lax
from jax.experimental import pallas as pl
from jax.experimental.pallas import tpu as pltpu

F32 = jnp.float32
BF16 = jnp.bfloat16

HEAD_DIM = 128
ROPE_BASE = 10000.0
RET_DECAY_BASE = 5.0
EPS = 1e-6
LB_FLOOR = 1e-30
MASK_VALUE = -1e30
CONV_WIDTH = 3
LOG2_E = math.log2(math.e)

LANES = 128
SUBLANES = 8
VMEM_LIMIT_BYTES = 48 * 1024 * 1024

NORM_ROWS = 256
MM_ROWS = 1024
MM_COLS = 512
IN_COLS = 1024
OUT_ROWS = 512
DOWN_ROWS = 256
FFN_COLS = 512
MIX_ROWS = 512
MIX_UNROLL = 2
HGRN_CHUNK = 64
HGRN_SUB = 16
HGRN_HEADS_PER_STEP = 8
RET_CHUNK = 128
RET_HEADS_PER_STEP = 4
FOX_BLOCK = 512
FOX_TICKS_PER_STEP = 4
FOX_ONES_ROWS = 16
SCAN_ROWS = 256

NT_DIMS = (((1,), (1,)), ((), ()))
TN_DIMS = (((0,), (0,)), ((), ()))


def _params(*semantics):
    return pltpu.CompilerParams(dimension_semantics=semantics, vmem_limit_bytes=VMEM_LIMIT_BYTES)


def _sigmoid(x):
    return 1.0 / (1.0 + jnp.exp(-x))


def _silu(x):
    return x * _sigmoid(x)


def _log_sigmoid(x):
    return jnp.minimum(x, 0.0) - jnp.log1p(jnp.exp(-jnp.abs(x)))


def _lower_tri(n):
    rows = lax.broadcasted_iota(jnp.int32, (n, n), 0)
    cols = lax.broadcasted_iota(jnp.int32, (n, n), 1)
    return (rows >= cols).astype(F32)


def _rmsnorm_kernel(x_ref, w_ref, o_ref):
    x = x_ref[...]
    ms = jnp.mean(x * x, axis=-1, keepdims=True)
    o_ref[...] = (x * lax.rsqrt(ms + EPS) * w_ref[...]).astype(o_ref.dtype)


def _rmsnorm(x, w, out_dtype):
    t, d = x.shape
    return pl.pallas_call(
        _rmsnorm_kernel,
        grid=(t // NORM_ROWS,),
        in_specs=[pl.BlockSpec((NORM_ROWS, d), lambda i: (i, 0)),
                  pl.BlockSpec((1, d), lambda i: (0, 0))],
        out_specs=pl.BlockSpec((NORM_ROWS, d), lambda i: (i, 0)),
        out_shape=jax.ShapeDtypeStruct((t, d), out_dtype),
        compiler_params=_params("parallel"),
        name="rmsnorm",
    )(x, w.reshape(1, d))


def _cast_weight_tile(w_ref, wb_ref, valid_cols):
    w = w_ref[...]
    if valid_cols < w.shape[1]:
        col = lax.broadcasted_iota(jnp.int32, w.shape, 1)
        w = jnp.where(col < valid_cols, w, 0.0)
    wb_ref[...] = w.astype(BF16)


def _matmul_kernel(x_ref, w_ref, o_ref, wb_ref, *, valid_cols):
    @pl.when(pl.program_id(1) == 0)
    def _():
        _cast_weight_tile(w_ref, wb_ref, valid_cols)

    o_ref[...] = jnp.dot(x_ref[...], wb_ref[...], preferred_element_type=F32).astype(o_ref.dtype)


def _matmul(x, w, layer, out_dtype, n_out, rows=MM_ROWS, cols=MM_COLS, first_col=0, name="matmul"):
    t, k = x.shape
    rows, cols = min(rows, t), min(cols, n_out)
    assert t % rows == 0 and n_out % cols == 0 and first_col % cols == 0
    first_block = first_col // cols
    valid_cols = min(cols, w.shape[2] - first_col) if n_out == cols else cols
    assert first_col + n_out <= w.shape[2] or n_out == cols
    return pl.pallas_call(
        functools.partial(_matmul_kernel, valid_cols=valid_cols),
        grid=(n_out // cols, t // rows),
        in_specs=[pl.BlockSpec((rows, k), lambda j, i: (i, 0)),
                  pl.BlockSpec((None, k, cols), lambda j, i: (layer, 0, first_block + j))],
        out_specs=pl.BlockSpec((rows, cols), lambda j, i: (i, j)),
        out_shape=jax.ShapeDtypeStruct((t, n_out), out_dtype),
        scratch_shapes=[pltpu.VMEM((k, cols), BF16)],
        compiler_params=_params("parallel", "arbitrary"),
        name=name,
    )(x, w)


def _residual_norm_kernel(*refs, n_parts, keep_sum):
    x_refs, (w_ref, r_ref, nw_ref), out_refs = refs[:n_parts], refs[n_parts:n_parts + 3], refs[n_parts + 3:]
    y = r_ref[...]
    k0 = 0
    for x_ref in x_refs:
        k1 = k0 + x_ref.shape[1]
        y = y + jnp.dot(x_ref[...], w_ref[k0:k1, :], preferred_element_type=F32)
        k0 = k1
    if keep_sum:
        out_refs[0][...] = y
    h_ref = out_refs[-1]
    ms = jnp.mean(y * y, axis=-1, keepdims=True)
    h_ref[...] = (y * lax.rsqrt(ms + EPS) * nw_ref[...]).astype(h_ref.dtype)


def _matmul_residual_norm(xs, w, residual, norm_w, h_dtype, rows, keep_sum=True, name="matmul_residual_norm"):
    t = xs[0].shape[0]
    k, n = w.shape
    assert t % rows == 0 and sum(x.shape[1] for x in xs) == k
    row_block = pl.BlockSpec((rows, n), lambda i: (i, 0))
    out_shape = [jax.ShapeDtypeStruct((t, n), h_dtype)]
    if keep_sum:
        out_shape.insert(0, jax.ShapeDtypeStruct((t, n), F32))
    outs = pl.pallas_call(
        functools.partial(_residual_norm_kernel, n_parts=len(xs), keep_sum=keep_sum),
        grid=(t // rows,),
        in_specs=[pl.BlockSpec((rows, x.shape[1]), lambda i: (i, 0)) for x in xs]
        + [pl.BlockSpec((k, n), lambda i: (0, 0), pipeline_mode=pl.Buffered(1)),
           row_block,
           pl.BlockSpec((1, n), lambda i: (0, 0))],
        out_specs=[row_block] * len(out_shape),
        out_shape=out_shape,
        compiler_params=_params("parallel"),
        name=name,
    )(*xs, w, residual, norm_w.reshape(1, n))
    return (outs[0], outs[1]) if keep_sum else (None, outs[0])


def _hgrn_kernel(q_ref, f_ref, i_ref, g_ref, lbl_ref, nw_ref, tri3_ref, sel_ref, mask_ref, o_ref, state_ref,
                 *, layer_j):
    C, L = HGRN_CHUNK, HGRN_SUB
    n_sub = C // L

    @pl.when(pl.program_id(2) == 0)
    def _():
        state_ref[...] = jnp.zeros_like(state_ref)

    lbl = lbl_ref[...]
    e = jnp.exp(lbl - jnp.max(lbl, axis=0, keepdims=True))
    sm = e / jnp.sum(e, axis=0, keepdims=True)
    lb_all = jnp.zeros((1, lbl.shape[1]), F32)
    for r in range(1, layer_j + 1):
        lb_all = lb_all + sm[r:r + 1, :]

    def pair_products(q, k, b, s):
        first = (s // SUBLANES) * SUBLANES
        q3, k3, b3 = (x.reshape(n_sub, L, HEAD_DIM) for x in (q, k, b))
        decay = jnp.exp2(jnp.minimum(b3[:, first:, :] - b3[:, s:s + 1, :], 0.0))
        prod = q3[:, first:, :] * k3[:, s:s + 1, :] * decay
        if first:
            prod = jnp.concatenate([jnp.zeros((n_sub, first, HEAD_DIM), F32), prod], axis=1)
        return prod.reshape(C, HEAD_DIM).astype(BF16)

    def head_chunk(rows, lanes, head):
        lb_floor, one_m_lb = jnp.maximum(lb_all[:, lanes], LB_FLOOR), 1.0 - lb_all[:, lanes]
        z = f_ref[0, rows, lanes]
        q = _silu(q_ref[0, rows, lanes])
        v = i_ref[0, rows, lanes].astype(BF16)

        e_z = jnp.exp(-jnp.abs(z))
        inv = 1.0 / (1.0 + e_z)
        log_f = jnp.log(lb_floor + one_m_lb * (jnp.where(z >= 0.0, 1.0, e_z) * inv))
        k = one_m_lb * (jnp.where(z >= 0.0, e_z, 1.0) * inv)
        hi = log_f.astype(BF16)
        rest = log_f - hi.astype(F32)
        mid = rest.astype(BF16)
        low = (rest - mid.astype(F32)).astype(BF16)
        b = jnp.dot(tri3_ref[...], jnp.concatenate([hi, mid, low], axis=0),
                    preferred_element_type=F32) * LOG2_E
        b_last = b[C - 1:C, :]

        state = state_ref[head]
        o = lax.dot_general((q * jnp.exp2(b)).astype(BF16), state.astype(BF16), NT_DIMS,
                            preferred_element_type=F32)

        below = [jnp.zeros((L, C), F32)]
        for i in range(1, n_sub):
            lo = i * L
            edge = b[lo - 1:lo, :]
            q_edge = (q[lo:lo + L, :] * jnp.exp2(b[lo:lo + L, :] - edge)).astype(BF16)
            k_edge = (k[:lo, :] * jnp.exp2(edge - b[:lo, :])).astype(BF16)
            k_edge = jnp.concatenate([k_edge, jnp.zeros((C - lo, HEAD_DIM), BF16)], axis=0)
            below.append(lax.dot_general(q_edge, k_edge, NT_DIMS, preferred_element_type=F32))
        pair = jnp.concatenate([pair_products(q, k, b, s) for s in range(L)], axis=1)
        diag = jnp.dot(pair, sel_ref[...], preferred_element_type=F32)
        scores = jnp.concatenate(below, axis=0) + diag * mask_ref[...]
        o = o + jnp.dot(scores.astype(BF16), v, preferred_element_type=F32)

        k_end = (k * jnp.exp2(b_last - b)).astype(BF16)
        state_ref[head] = state * jnp.exp2(b_last) + lax.dot_general(v, k_end, TN_DIMS,
                                                                     preferred_element_type=F32)

        y = o * lax.rsqrt(jnp.mean(o * o, axis=-1, keepdims=True) + EPS) * nw_ref[:, lanes]
        o_ref[0, rows, lanes] = (y * _silu(g_ref[0, rows, lanes])).astype(o_ref.dtype)

    def chunk(c, carry):
        rows = pl.ds(pl.multiple_of(c * C, C), C)
        for head in range(HGRN_HEADS_PER_STEP):
            head_chunk(rows, slice(head * HEAD_DIM, (head + 1) * HEAD_DIM), head)
        return carry

    lax.fori_loop(0, MIX_ROWS // C, chunk, 0, unroll=MIX_UNROLL)


def _hgrn(proj, lb_logits, norm_w, layer_j):
    bsz, seq, _ = proj.shape
    d_h = norm_w.shape[0]
    width = HGRN_HEADS_PER_STEP * HEAD_DIM
    n_groups = d_h // width
    n_layers = lb_logits.shape[0]

    C, L = HGRN_CHUNK, HGRN_SUB
    pos = jnp.arange(C)
    causal = pos[:, None] >= pos[None, :]
    tri3 = jnp.tile(causal.astype(BF16), (1, 3))
    pair_sel = (jnp.repeat(jnp.arange(L), HEAD_DIM)[:, None] == (pos % L)[None, :]).astype(BF16)
    diag_mask = (causal & (pos[:, None] // L == pos[None, :] // L)).astype(F32)

    def col(group):
        return pl.BlockSpec((1, MIX_ROWS, width), lambda b, h, s: (b, s, group * n_groups + h))

    def whole(a):
        return pl.BlockSpec(a.shape, lambda b, h, s: (0, 0))

    return pl.pallas_call(
        functools.partial(_hgrn_kernel, layer_j=layer_j),
        grid=(bsz, n_groups, seq // MIX_ROWS),
        in_specs=[col(0), col(1), col(2), col(3),
                  pl.BlockSpec((n_layers, width), lambda b, h, s: (0, h)),
                  pl.BlockSpec((1, width), lambda b, h, s: (0, h)),
                  whole(tri3), whole(pair_sel), whole(diag_mask)],
        out_specs=pl.BlockSpec((1, MIX_ROWS, width), lambda b, h, s: (b, s, h)),
        out_shape=jax.ShapeDtypeStruct((bsz, seq, d_h), BF16),
        scratch_shapes=[pltpu.VMEM((HGRN_HEADS_PER_STEP, HEAD_DIM, HEAD_DIM), F32)],
        compiler_params=_params("parallel", "parallel", "arbitrary"),
        name="hgrn2",
    )(proj, proj, proj, proj, lb_logits, norm_w.reshape(1, d_h), tri3, pair_sel, diag_mask)


def _retention_kernel(q_ref, k_ref, v_ref, g_ref, cos_ref, sin_ref, intra_ref, edge_ref, nw_ref, o_ref, state_ref):
    C = RET_CHUNK

    @pl.when(pl.program_id(2) == 0)
    def _():
        state_ref[...] = jnp.zeros_like(state_ref)

    def rope(x, cos, sin_signed):
        return x * cos + pltpu.roll(x, HEAD_DIM // 2, 1) * sin_signed

    def head_chunk(sl, lanes, head):
        cos, sin_signed = cos_ref[sl, :], sin_ref[sl, :]
        q = rope(q_ref[0, sl, lanes], cos, sin_signed)
        k = rope(k_ref[0, sl, lanes], cos, sin_signed) * (HEAD_DIM ** -0.5)
        v = v_ref[0, sl, lanes].astype(BF16)
        state = state_ref[head]
        w_start, w_end, decay_chunk = edge_ref[head, 0], edge_ref[head, 1], edge_ref[head, 2, 0:1, :]

        scores = lax.dot_general(q.astype(BF16), k.astype(BF16), NT_DIMS,
                                 preferred_element_type=F32) * intra_ref[head]
        o = jnp.dot(scores.astype(BF16), v, preferred_element_type=F32)
        o = o + lax.dot_general((q * w_start).astype(BF16), state.astype(BF16), NT_DIMS,
                                preferred_element_type=F32)
        state_ref[head] = decay_chunk * state + lax.dot_general(v, (k * w_end).astype(BF16), TN_DIMS,
                                                                preferred_element_type=F32)

        cen = o - jnp.mean(o, axis=-1, keepdims=True)
        y = cen * lax.rsqrt(jnp.mean(cen * cen, axis=-1, keepdims=True) + EPS) * nw_ref[:, lanes]
        o_ref[0, sl, lanes] = (y * _silu(g_ref[0, sl, lanes])).astype(o_ref.dtype)

    def chunk(c, carry):
        sl = pl.ds(pl.multiple_of(c * C, C), C)
        for head in range(RET_HEADS_PER_STEP):
            head_chunk(sl, slice(head * HEAD_DIM, (head + 1) * HEAD_DIM), head)
        return carry

    lax.fori_loop(0, MIX_ROWS // C, chunk, 0, unroll=MIX_UNROLL)


def _retention(proj, norm_w, first_group):
    bsz, seq, _ = proj.shape
    d_r = norm_w.shape[0]
    n_heads = d_r // HEAD_DIM
    half = HEAD_DIM // 2

    inv = ROPE_BASE ** (-jnp.arange(0, HEAD_DIM, 2, dtype=F32) / HEAD_DIM)
    ang = jnp.arange(seq).astype(F32)[:, None] * inv[None, :]
    cos = jnp.concatenate([jnp.cos(ang), jnp.cos(ang)], axis=-1)
    sin_signed = jnp.concatenate([-jnp.sin(ang), jnp.sin(ang)], axis=-1)
    assert cos.shape == (seq, 2 * half)
    C = RET_CHUNK
    log_gamma = jnp.log1p(-jnp.exp2(-RET_DECAY_BASE - jnp.arange(n_heads, dtype=F32)))
    pos = jnp.arange(C, dtype=F32)
    rel = pos[:, None] - pos[None, :]
    intra = jnp.where(rel >= 0, jnp.exp(log_gamma[:, None, None] * jnp.maximum(rel, 0.0)), 0.0)
    w_start = jnp.exp(log_gamma[:, None] * (pos + 1.0)[None, :])
    w_end = jnp.exp(log_gamma[:, None] * (C - 1.0 - pos)[None, :])
    decay_chunk = jnp.broadcast_to(jnp.exp(log_gamma * C)[:, None], (n_heads, C))
    edge = jnp.broadcast_to(jnp.stack([w_start, w_end, decay_chunk], axis=1)[..., None],
                            (n_heads, 3, C, HEAD_DIM))

    per_step = RET_HEADS_PER_STEP
    width = per_step * HEAD_DIM
    n_groups = n_heads // per_step
    first = first_group * n_groups

    def col(group):
        return pl.BlockSpec((1, MIX_ROWS, width), lambda b, h, s: (b, s, first + group * n_groups + h))

    table = pl.BlockSpec((MIX_ROWS, HEAD_DIM), lambda b, h, s: (s, 0))
    return pl.pallas_call(
        _retention_kernel,
        grid=(bsz, n_groups, seq // MIX_ROWS),
        in_specs=[col(0), col(1), col(2), col(3), table, table,
                  pl.BlockSpec((per_step, C, C), lambda b, h, s: (h, 0, 0)),
                  pl.BlockSpec((per_step, 3, C, HEAD_DIM), lambda b, h, s: (h, 0, 0, 0)),
                  pl.BlockSpec((1, width), lambda b, h, s: (0, h))],
        out_specs=pl.BlockSpec((1, MIX_ROWS, width), lambda b, h, s: (b, s, h)),
        out_shape=jax.ShapeDtypeStruct((bsz, seq, d_r), BF16),
        scratch_shapes=[pltpu.VMEM((per_step, HEAD_DIM, HEAD_DIM), F32)],
        compiler_params=_params("parallel", "parallel", "arbitrary"),
        name="retention",
    )(proj, proj, proj, proj, cos, sin_signed, intra, edge, norm_w.reshape(1, d_r))


def _forget_scan_kernel(f_ref, bias_ref, c_ref, carry_ref):
    @pl.when(pl.program_id(1) == 0)
    def _():
        carry_ref[...] = jnp.zeros_like(carry_ref)

    log_f = _log_sigmoid(f_ref[0] + bias_ref[...])
    c = jnp.dot(_lower_tri(SCAN_ROWS), log_f, precision=lax.Precision.HIGHEST,
                preferred_element_type=F32) + carry_ref[...]
    c_ref[0] = c
    carry_ref[...] = c[SCAN_ROWS - 1:SCAN_ROWS, :]


def _forget_scan(f, bias):
    bsz, seq, lanes = f.shape
    return pl.pallas_call(
        _forget_scan_kernel,
        grid=(bsz, seq // SCAN_ROWS),
        in_specs=[pl.BlockSpec((1, SCAN_ROWS, lanes), lambda b, s: (b, s, 0)),
                  pl.BlockSpec((1, lanes), lambda b, s: (0, 0))],
        out_specs=pl.BlockSpec((1, SCAN_ROWS, lanes), lambda b, s: (b, s, 0)),
        out_shape=jax.ShapeDtypeStruct((bsz, seq, lanes), F32),
        scratch_shapes=[pltpu.VMEM((1, lanes), F32)],
        compiler_params=_params("parallel", "arbitrary"),
        name="forget_scan",
    )(f, bias)


def _split3(x):
    hi = x.astype(BF16).astype(F32)
    rest = x - hi
    mid = rest.astype(BF16).astype(F32)
    low = (rest - mid).astype(BF16).astype(F32)
    return hi, mid, low


def _fox_kernel(pairs_ref, q_ref, k_ref, v_ref, g_ref, ct_ref, o_ref,
                kaug_ref, vt_ref, qaug_ref, m_ref, acc_ref, s0_ref, s1_ref, p0_ref, p1_ref, a0_ref, a1_ref):
    T = FOX_BLOCK
    n_blocks = vt_ref.shape[0]
    n_pairs = n_blocks * (n_blocks + 1) // 2
    aug_row = lax.broadcasted_iota(jnp.int32, (HEAD_DIM, T), 0)

    def bias_rows(c_row, for_queries):
        hi, mid, low = _split3(c_row)
        if for_queries:
            pieces = jnp.where(aug_row == 3, hi, jnp.where(aug_row == 4, mid, jnp.where(aug_row == 5, low, 0.0)))
            return jnp.where(aug_row < 3, 1.0, pieces)
        pieces = jnp.where(aug_row == 0, -hi, jnp.where(aug_row == 1, -mid, jnp.where(aug_row == 2, -low, 0.0)))
        return jnp.where((aug_row >= 3) & (aug_row < 6), 1.0, pieces)

    for j in range(n_blocks):
        rows = slice(j * T, (j + 1) * T)
        c_j = ct_ref[0, 0, j:j + 1, :] * LOG2_E
        kaug_ref[rows, 0:HEAD_DIM] = k_ref[0, rows, :]
        kaug_ref[rows, HEAD_DIM:2 * HEAD_DIM] = bias_rows(c_j, False).T.astype(BF16)
        vt_ref[j, 0:HEAD_DIM, :] = v_ref[0, rows, :].astype(F32).T.astype(BF16)
        vt_ref[j, HEAD_DIM:, :] = jnp.ones((FOX_ONES_ROWS, T), BF16)
        qaug_ref[j, 0:HEAD_DIM, :] = (q_ref[0, rows, :].astype(F32) * (HEAD_DIM ** -0.5 * LOG2_E)).T.astype(BF16)
        qaug_ref[j, HEAD_DIM:2 * HEAD_DIM, :] = bias_rows(c_j, True).astype(BF16)
    m_ref[...] = jnp.full(m_ref.shape, MASK_VALUE, F32)
    acc_ref[...] = jnp.zeros(acc_ref.shape, F32)
    for p_ref, a_ref in ((p0_ref, a0_ref), (p1_ref, a1_ref)):
        p_ref[...] = jnp.zeros((T, T), BF16)
        a_ref[...] = jnp.ones((1, T), F32)

    key_id = lax.broadcasted_iota(jnp.int32, (T, T), 0)
    query_id = lax.broadcasted_iota(jnp.int32, (T, T), 1)

    def form_logits(pair, s_ref):
        qb, kb = pair
        keys = pl.ds(pl.multiple_of(kb * T, T), T)
        s_ref[...] = jnp.dot(kaug_ref[keys, :], qaug_ref[qb], preferred_element_type=F32)

    def softmax_update(pair, valid, diagonal, s_ref, p_ref, a_ref):
        qb, _ = pair
        s = s_ref[...]
        if diagonal:
            s = jnp.where(key_id <= query_id, s, MASK_VALUE)
        m = m_ref[qb]
        m_new = jnp.maximum(m, jnp.where(valid, jnp.max(s, axis=0, keepdims=True), MASK_VALUE))
        p_ref[...] = jnp.exp2(s - jnp.where(valid, m_new, -MASK_VALUE)).astype(BF16)
        a_ref[...] = jnp.exp2(m - m_new)
        m_ref[qb] = m_new

    def accumulate(pair, p_ref, a_ref):
        qb, kb = pair
        acc_ref[qb] = a_ref[...] * acc_ref[qb] + jnp.dot(vt_ref[kb], p_ref[...], preferred_element_type=F32)

    def pair_at(n):
        n = jnp.clip(n, 0, n_pairs - 1)
        return pairs_ref[0, n], pairs_ref[1, n]

    def tick(n, diagonal, s_now, p_now, a_now, s_next, p_prev, a_prev):
        form_logits(pair_at(n + 1), s_next)
        softmax_update(pair_at(n), n < n_pairs, diagonal, s_now, p_now, a_now)
        accumulate(pair_at(n - 1), p_prev, a_prev)

    def step(i, is_diagonal):
        for k in range(0, FOX_TICKS_PER_STEP, 2):
            n = FOX_TICKS_PER_STEP * i + k
            tick(n, is_diagonal(n), s0_ref, p0_ref, a0_ref, s1_ref, p1_ref, a1_ref)
            tick(n + 1, is_diagonal(n + 1), s1_ref, p1_ref, a1_ref, s0_ref, p0_ref, a0_ref)

    def diagonal_step(i, carry):
        step(i, lambda n: True)
        return carry

    def below_step(i, carry):
        step(i, lambda n: False)
        return carry

    n_steps = -(-(n_pairs + 2) // FOX_TICKS_PER_STEP)
    diagonal_steps = n_blocks // FOX_TICKS_PER_STEP
    form_logits(pair_at(0), s0_ref)
    lax.fori_loop(0, diagonal_steps, diagonal_step, 0)
    if n_blocks % FOX_TICKS_PER_STEP:
        step(diagonal_steps, lambda n: n < n_blocks)
        diagonal_steps += 1
    lax.fori_loop(diagonal_steps, n_steps, below_step, 0)

    for j in range(n_blocks):
        rows = slice(j * T, (j + 1) * T)
        acc = acc_ref[j]
        o = (acc[:HEAD_DIM, :] / acc[HEAD_DIM:HEAD_DIM + 1, :]).T
        o_ref[0, rows, :] = (o * _sigmoid(g_ref[0, rows, :].astype(F32))).astype(o_ref.dtype)


def _fox(proj, c, n_heads):
    bsz, seq, _ = proj.shape
    T = FOX_BLOCK
    n_blocks = seq // T
    c_rows = jnp.transpose(c[:, :, :n_heads], (0, 2, 1)).reshape(bsz, n_heads, n_blocks, T)
    causal_pairs = ([(qb, qb) for qb in range(n_blocks)]
                    + [(qb, kb) for qb in range(n_blocks) for kb in range(qb)])
    pairs = jnp.asarray(causal_pairs, jnp.int32).T

    def head_cols(group):
        return pl.BlockSpec((1, seq, HEAD_DIM), lambda b, h: (b, 0, group * n_heads + h))

    return pl.pallas_call(
        _fox_kernel,
        grid=(bsz, n_heads),
        in_specs=[pl.BlockSpec(memory_space=pltpu.SMEM),
                  head_cols(0), head_cols(1), head_cols(2), head_cols(3),
                  pl.BlockSpec((1, 1, n_blocks, T), lambda b, h: (b, h, 0, 0))],
        out_specs=pl.BlockSpec((1, seq, HEAD_DIM), lambda b, h: (b, 0, h)),
        out_shape=jax.ShapeDtypeStruct((bsz, seq, n_heads * HEAD_DIM), BF16),
        scratch_shapes=[pltpu.VMEM((seq, 2 * HEAD_DIM), BF16),
                        pltpu.VMEM((n_blocks, HEAD_DIM + FOX_ONES_ROWS, T), BF16),
                        pltpu.VMEM((n_blocks, 2 * HEAD_DIM, T), BF16),
                        pltpu.VMEM((n_blocks, 1, T), F32),
                        pltpu.VMEM((n_blocks, HEAD_DIM + FOX_ONES_ROWS, T), F32),
                        pltpu.VMEM((T, T), F32), pltpu.VMEM((T, T), F32),
                        pltpu.VMEM((T, T), BF16), pltpu.VMEM((T, T), BF16),
                        pltpu.VMEM((1, T), F32), pltpu.VMEM((1, T), F32)],
        compiler_params=_params("parallel", "parallel"),
        name="fox_attention",
    )(pairs, proj, proj, proj, proj, c_rows)


def _ffn_up_kernel(x_ref, wg_ref, wv_ref, cg_ref, cv_ref, o_ref, ug_ref, uv_ref, wgb_ref, wvb_ref,
                   *, seq_steps, d_ff):
    rows = x_ref.shape[0]

    @pl.when(pl.program_id(1) == 0)
    def _():
        valid = d_ff - pl.program_id(0) * wg_ref.shape[1]
        col = lax.broadcasted_iota(jnp.int32, wg_ref.shape, 1)
        wgb_ref[...] = jnp.where(col < valid, wg_ref[...], 0.0).astype(BF16)
        wvb_ref[...] = jnp.where(col < valid, wv_ref[...], 0.0).astype(BF16)

    @pl.when(pl.program_id(1) % seq_steps == 0)
    def _():
        ug_ref[0:SUBLANES, :] = jnp.zeros((SUBLANES, ug_ref.shape[1]), F32)
        uv_ref[0:SUBLANES, :] = jnp.zeros((SUBLANES, uv_ref.shape[1]), F32)

    x = x_ref[...]

    def conv(wb_ref, c_ref, u_ref):
        u_ref[SUBLANES:SUBLANES + rows, :] = jnp.dot(x, wb_ref[...], preferred_element_type=F32)
        out = c_ref[CONV_WIDTH - 1:CONV_WIDTH, :] * u_ref[SUBLANES:SUBLANES + rows, :]
        for tap in range(CONV_WIDTH - 1):
            back = CONV_WIDTH - 1 - tap
            out = out + c_ref[tap:tap + 1, :] * u_ref[SUBLANES - back:SUBLANES - back + rows, :]
        carry = u_ref[rows:rows + SUBLANES, :]
        u_ref[0:SUBLANES, :] = carry
        return out

    gate = conv(wgb_ref, cg_ref, ug_ref)
    val = conv(wvb_ref, cv_ref, uv_ref)
    o_ref[...] = (_silu(gate) * val).astype(o_ref.dtype)


def _ffn_up(h, w_up, layer, conv_w, seq, d_ff, d_ff_pad):
    t, d = h.shape
    rows = min(MM_ROWS, seq)
    assert seq % rows == 0 and d_ff_pad % FFN_COLS == 0
    n_cols = d_ff_pad // FFN_COLS
    return pl.pallas_call(
        functools.partial(_ffn_up_kernel, seq_steps=seq // rows, d_ff=d_ff),
        grid=(n_cols, t // rows),
        in_specs=[pl.BlockSpec((rows, d), lambda j, i: (i, 0)),
                  pl.BlockSpec((None, d, FFN_COLS), lambda j, i: (layer, 0, j)),
                  pl.BlockSpec((None, pl.Element(d), pl.Element(FFN_COLS, (0, d_ff_pad - d_ff))),
                               lambda j, i: (layer, 0, pl.multiple_of(d_ff + j * FFN_COLS, LANES))),
                  pl.BlockSpec((CONV_WIDTH, FFN_COLS), lambda j, i: (0, j)),
                  pl.BlockSpec((CONV_WIDTH, FFN_COLS), lambda j, i: (0, j + n_cols))],
        out_specs=pl.BlockSpec((rows, FFN_COLS), lambda j, i: (i, j)),
        out_shape=jax.ShapeDtypeStruct((t, d_ff_pad), BF16),
        scratch_shapes=[pltpu.VMEM((rows + SUBLANES, FFN_COLS), F32),
                        pltpu.VMEM((rows + SUBLANES, FFN_COLS), F32),
                        pltpu.VMEM((d, FFN_COLS), BF16),
                        pltpu.VMEM((d, FFN_COLS), BF16)],
        compiler_params=_params("parallel", "arbitrary"),
        name="ffn_up_conv",
    )(h, w_up, w_up, conv_w, conv_w)


def _pad_cols(w, width):
    return jnp.pad(w, ((0, 0), (0, width - w.shape[1])))


def kernel(x, attn_norm_w, ffn_norm_w, final_norm_w, even_w_in, hgrn_lb_logits, hgrn_norm_w, ret_norm_w,
           even_w_out, odd_w_in, odd_b_f, odd_w_out, ffn_w_up, ffn_conv_w, ffn_w_down):
    bsz, seq, d_model = x.shape
    depth = attn_norm_w.shape[0]
    tokens = bsz * seq
    d_hgrn = hgrn_norm_w.shape[1]
    d_fox = odd_w_out.shape[1]
    n_fox = odd_b_f.shape[1]
    d_ff = ffn_w_down.shape[1]
    d_ff_pad = FFN_COLS * math.ceil(d_ff / FFN_COLS)

    xf = x.reshape(tokens, d_model)
    h = _rmsnorm(xf, attn_norm_w[0], BF16)
    for layer in range(depth):
        j = layer // 2
        if layer % 2 == 0:
            proj = _matmul(h, even_w_in, j, F32, even_w_in.shape[2], cols=IN_COLS,
                           name="even_in").reshape(bsz, seq, -1)
            o_a = _hgrn(proj, hgrn_lb_logits, hgrn_norm_w[j], j)
            o_b = _retention(proj, ret_norm_w[j], first_group=4 * d_hgrn // ret_norm_w.shape[1])
            mixed = [o_a.reshape(tokens, -1), o_b.reshape(tokens, -1)]
            w_out = even_w_out[j]
        else:
            proj = _matmul(h, odd_w_in, j, BF16, 4 * d_fox, cols=IN_COLS, name="fox_in").reshape(bsz, seq, -1)
            f = _matmul(h, odd_w_in, j, F32, LANES, cols=LANES, first_col=4 * d_fox, name="fox_forget")
            c = _forget_scan(f.reshape(bsz, seq, LANES), _pad_cols(odd_b_f[j][None, :], LANES))
            mixed = [_fox(proj, c, n_fox).reshape(tokens, -1)]
            w_out = odd_w_out[j]
        xf, h = _matmul_residual_norm(mixed, w_out.astype(BF16), xf, ffn_norm_w[layer], BF16, rows=OUT_ROWS,
                                      name="mixer_out_norm")
        conv_w = ffn_conv_w[layer]
        conv_w = jnp.concatenate([_pad_cols(conv_w[:, :d_ff], d_ff_pad), _pad_cols(conv_w[:, d_ff:], d_ff_pad)],
                                 axis=1)
        act = _ffn_up(h, ffn_w_up, layer, conv_w, seq, d_ff, d_ff_pad)
        w_down = jnp.pad(ffn_w_down[layer], ((0, d_ff_pad - d_ff), (0, 0))).astype(BF16)
        last = layer == depth - 1
        xf, h = _matmul_residual_norm([act], w_down, xf, final_norm_w if last else attn_norm_w[layer + 1],
                                      x.dtype if last else BF16, rows=DOWN_ROWS, keep_sum=not last,
                                      name="ffn_down_norm")
    return h.reshape(bsz, seq, d_model)
```

```python
import functools
import math

import jax
import jax.numpy as jnp
from jax import lax
from jax.experimental import pallas as pl
from jax.experimental.pallas import tpu as pltpu

F32 = jnp.float32
BF16 = jnp.bfloat16

HEAD_DIM = 128
ROPE_BASE = 10000.0
RET_DECAY_BASE = 5.0
EPS = 1e-6
LB_FLOOR = 1e-30
MASK_VALUE = -1e30
CONV_WIDTH = 3
LOG2_E = math.log2(math.e)

LANES = 128
SUBLANES = 8
VMEM_LIMIT_BYTES = 48 * 1024 * 1024

NORM_ROWS = 256
MM_ROWS = 1024
MM_COLS = 512
IN_COLS = 1024
OUT_ROWS = 512
DOWN_ROWS = 256
FFN_COLS = 512
MIX_ROWS = 512
MIX_UNROLL = 2
HGRN_CHUNK = 64
HGRN_SUB = 16
HGRN_HEADS_PER_STEP = 8
RET_CHUNK = 128
RET_HEADS_PER_STEP = 4
FOX_BLOCK = 512
FOX_TICKS_PER_STEP = 4
FOX_ONES_ROWS = 16
SCAN_ROWS = 256

NT_DIMS = (((1,), (1,)), ((), ()))
TN_DIMS = (((0,), (0,)), ((), ()))


def _params(*semantics):
    return pltpu.CompilerParams(dimension_semantics=semantics, vmem_limit_bytes=VMEM_LIMIT_BYTES)


def _sigmoid(x):
    return 1.0 / (1.0 + jnp.exp(-x))


def _silu(x):
    return x * _sigmoid(x)


def _log_sigmoid(x):
    return jnp.minimum(x, 0.0) - jnp.log1p(jnp.exp(-jnp.abs(x)))


def _lower_tri(n):
    rows = lax.broadcasted_iota(jnp.int32, (n, n), 0)
    cols = lax.broadcasted_iota(jnp.int32, (n, n), 1)
    return (rows >= cols).astype(F32)


def _rmsnorm_kernel(x_ref, w_ref, o_ref):
    x = x_ref[...]
    ms = jnp.mean(x * x, axis=-1, keepdims=True)
    o_ref[...] = (x * lax.rsqrt(ms + EPS) * w_ref[...]).astype(o_ref.dtype)


def _rmsnorm(x, w, out_dtype):
    t, d = x.shape
    return pl.pallas_call(
        _rmsnorm_kernel,
        grid=(t // NORM_ROWS,),
        in_specs=[pl.BlockSpec((NORM_ROWS, d), lambda i: (i, 0)),
                  pl.BlockSpec((1, d), lambda i: (0, 0))],
        out_specs=pl.BlockSpec((NORM_ROWS, d), lambda i: (i, 0)),
        out_shape=jax.ShapeDtypeStruct((t, d), out_dtype),
        compiler_params=_params("parallel"),
        name="rmsnorm",
    )(x, w.reshape(1, d))


def _cast_weight_tile(w_ref, wb_ref, valid_cols):
    w = w_ref[...]
    if valid_cols < w.shape[1]:
        col = lax.broadcasted_iota(jnp.int32, w.shape, 1)
        w = jnp.where(col < valid_cols, w, 0.0)
    wb_ref[...] = w.astype(BF16)


def _matmul_kernel(x_ref, w_ref, o_ref, wb_ref, *, valid_cols):
    @pl.when(pl.program_id(1) == 0)
    def _():
        _cast_weight_tile(w_ref, wb_ref, valid_cols)

    o_ref[...] = jnp.dot(x_ref[...], wb_ref[...], preferred_element_type=F32).astype(o_ref.dtype)


def _matmul(x, w, layer, out_dtype, n_out, rows=MM_ROWS, cols=MM_COLS, first_col=0, name="matmul"):
    t, k = x.shape
    rows, cols = min(rows, t), min(cols, n_out)
    assert t % rows == 0 and n_out % cols == 0 and first_col % cols == 0
    first_block = first_col // cols
    valid_cols = min(cols, w.shape[2] - first_col) if n_out == cols else cols
    assert first_col + n_out <= w.shape[2] or n_out == cols
    return pl.pallas_call(
        functools.partial(_matmul_kernel, valid_cols=valid_cols),
        grid=(n_out // cols, t // rows),
        in_specs=[pl.BlockSpec((rows, k), lambda j, i: (i, 0)),
                  pl.BlockSpec((None, k, cols), lambda j, i: (layer, 0, first_block + j))],
        out_specs=pl.BlockSpec((rows, cols), lambda j, i: (i, j)),
        out_shape=jax.ShapeDtypeStruct((t, n_out), out_dtype),
        scratch_shapes=[pltpu.VMEM((k, cols), BF16)],
        compiler_params=_params("parallel", "arbitrary"),
        name=name,
    )(x, w)


def _residual_norm_kernel(*refs, n_parts, keep_sum):
    x_refs, (w_ref, r_ref, nw_ref), out_refs = refs[:n_parts], refs[n_parts:n_parts + 3], refs[n_parts + 3:]
    y = r_ref[...]
    k0 = 0
    for x_ref in x_refs:
        k1 = k0 + x_ref.shape[1]
        y = y + jnp.dot(x_ref[...], w_ref[k0:k1, :], preferred_element_type=F32)
        k0 = k1
    if keep_sum:
        out_refs[0][...] = y
    h_ref = out_refs[-1]
    ms = jnp.mean(y * y, axis=-1, keepdims=True)
    h_ref[...] = (y * lax.rsqrt(ms + EPS) * nw_ref[...]).astype(h_ref.dtype)


def _matmul_residual_norm(xs, w, residual, norm_w, h_dtype, rows, keep_sum=True, name="matmul_residual_norm"):
    t = xs[0].shape[0]
    k, n = w.shape
    assert t % rows == 0 and sum(x.shape[1] for x in xs) == k
    row_block = pl.BlockSpec((rows, n), lambda i: (i, 0))
    out_shape = [jax.ShapeDtypeStruct((t, n), h_dtype)]
    if keep_sum:
        out_shape.insert(0, jax.ShapeDtypeStruct((t, n), F32))
    outs = pl.pallas_call(
        functools.partial(_residual_norm_kernel, n_parts=len(xs), keep_sum=keep_sum),
        grid=(t // rows,),
        in_specs=[pl.BlockSpec((rows, x.shape[1]), lambda i: (i, 0)) for x in xs]
        + [pl.BlockSpec((k, n), lambda i: (0, 0), pipeline_mode=pl.Buffered(1)),
           row_block,
           pl.BlockSpec((1, n), lambda i: (0, 0))],
        out_specs=[row_block] * len(out_shape),
        out_shape=out_shape,
        compiler_params=_params("parallel"),
        name=name,
    )(*xs, w, residual, norm_w.reshape(1, n))
    return (outs[0], outs[1]) if keep_sum else (None, outs[0])


def _hgrn_kernel(q_ref, f_ref, i_ref, g_ref, lbl_ref, nw_ref, tri3_ref, sel_ref, mask_ref, o_ref, state_ref,
                 *, layer_j):
    C, L = HGRN_CHUNK, HGRN_SUB
    n_sub = C // L

    @pl.when(pl.program_id(2) == 0)
    def _():
        state_ref[...] = jnp.zeros_like(state_ref)

    lbl = lbl_ref[...]
    e = jnp.exp(lbl - jnp.max(lbl, axis=0, keepdims=True))
    sm = e / jnp.sum(e, axis=0, keepdims=True)
    lb_all = jnp.zeros((1, lbl.shape[1]), F32)
    for r in range(1, layer_j + 1):
        lb_all = lb_all + sm[r:r + 1, :]

    def pair_products(q, k, b, s):
        first = (s // SUBLANES) * SUBLANES
        q3, k3, b3 = (x.reshape(n_sub, L, HEAD_DIM) for x in (q, k, b))
        decay = jnp.exp2(jnp.minimum(b3[:, first:, :] - b3[:, s:s + 1, :], 0.0))
        prod = q3[:, first:, :] * k3[:, s:s + 1, :] * decay
        if first:
            prod = jnp.concatenate([jnp.zeros((n_sub, first, HEAD_DIM), F32), prod], axis=1)
        return prod.reshape(C, HEAD_DIM).astype(BF16)

    def head_chunk(rows, lanes, head):
        lb_floor, one_m_lb = jnp.maximum(lb_all[:, lanes], LB_FLOOR), 1.0 - lb_all[:, lanes]
        z = f_ref[0, rows, lanes]
        q = _silu(q_ref[0, rows, lanes])
        v = i_ref[0, rows, lanes].astype(BF16)

        e_z = jnp.exp(-jnp.abs(z))
        inv = 1.0 / (1.0 + e_z)
        log_f = jnp.log(lb_floor + one_m_lb * (jnp.where(z >= 0.0, 1.0, e_z) * inv))
        k = one_m_lb * (jnp.where(z >= 0.0, e_z, 1.0) * inv)
        hi = log_f.astype(BF16)
        rest = log_f - hi.astype(F32)
        mid = rest.astype(BF16)
        low = (rest - mid.astype(F32)).astype(BF16)
        b = jnp.dot(tri3_ref[...], jnp.concatenate([hi, mid, low], axis=0),
                    preferred_element_type=F32) * LOG2_E
        b_last = b[C - 1:C, :]

        state = state_ref[head]
        o = lax.dot_general((q * jnp.exp2(b)).astype(BF16), state.astype(BF16), NT_DIMS,
                            preferred_element_type=F32)

        below = [jnp.zeros((L, C), F32)]
        for i in range(1, n_sub):
            lo = i * L
            edge = b[lo - 1:lo, :]
            q_edge = (q[lo:lo + L, :] * jnp.exp2(b[lo:lo + L, :] - edge)).astype(BF16)
            k_edge = (k[:lo, :] * jnp.exp2(edge - b[:lo, :])).astype(BF16)
            k_edge = jnp.concatenate([k_edge, jnp.zeros((C - lo, HEAD_DIM), BF16)], axis=0)
            below.append(lax.dot_general(q_edge, k_edge, NT_DIMS, preferred_element_type=F32))
        pair = jnp.concatenate([pair_products(q, k, b, s) for s in range(L)], axis=1)
        diag = jnp.dot(pair, sel_ref[...], preferred_element_type=F32)
        scores = jnp.concatenate(below, axis=0) + diag * mask_ref[...]
        o = o + jnp.dot(scores.astype(BF16), v, preferred_element_type=F32)

        k_end = (k * jnp.exp2(b_last - b)).astype(BF16)
        state_ref[head] = state * jnp.exp2(b_last) + lax.dot_general(v, k_end, TN_DIMS,
                                                                     preferred_element_type=F32)

        y = o * lax.rsqrt(jnp.mean(o * o, axis=-1, keepdims=True) + EPS) * nw_ref[:, lanes]
        o_ref[0, rows, lanes] = (y * _silu(g_ref[0, rows, lanes])).astype(o_ref.dtype)

    def chunk(c, carry):
        rows = pl.ds(pl.multiple_of(c * C, C), C)
        for head in range(HGRN_HEADS_PER_STEP):
            head_chunk(rows, slice(head * HEAD_DIM, (head + 1) * HEAD_DIM), head)
        return carry

    lax.fori_loop(0, MIX_ROWS // C, chunk, 0, unroll=MIX_UNROLL)


def _hgrn(proj, lb_logits, norm_w, layer_j):
    bsz, seq, _ = proj.shape
    d_h = norm_w.shape[0]
    width = HGRN_HEADS_PER_STEP * HEAD_DIM
    n_groups = d_h // width
    n_layers = lb_logits.shape[0]

    C, L = HGRN_CHUNK, HGRN_SUB
    pos = jnp.arange(C)
    causal = pos[:, None] >= pos[None, :]
    tri3 = jnp.tile(causal.astype(BF16), (1, 3))
    pair_sel = (jnp.repeat(jnp.arange(L), HEAD_DIM)[:, None] == (pos % L)[None, :]).astype(BF16)
    diag_mask = (causal & (pos[:, None] // L == pos[None, :] // L)).astype(F32)

    def col(group):
        return pl.BlockSpec((1, MIX_ROWS, width), lambda b, h, s: (b, s, group * n_groups + h))

    def whole(a):
        return pl.BlockSpec(a.shape, lambda b, h, s: (0, 0))

    return pl.pallas_call(
        functools.partial(_hgrn_kernel, layer_j=layer_j),
        grid=(bsz, n_groups, seq // MIX_ROWS),
        in_specs=[col(0), col(1), col(2), col(3),
                  pl.BlockSpec((n_layers, width), lambda b, h, s: (0, h)),
                  pl.BlockSpec((1, width), lambda b, h, s: (0, h)),
                  whole(tri3), whole(pair_sel), whole(diag_mask)],
        out_specs=pl.BlockSpec((1, MIX_ROWS, width), lambda b, h, s: (b, s, h)),
        out_shape=jax.ShapeDtypeStruct((bsz, seq, d_h), BF16),
        scratch_shapes=[pltpu.VMEM((HGRN_HEADS_PER_STEP, HEAD_DIM, HEAD_DIM), F32)],
        compiler_params=_params("parallel", "parallel", "arbitrary"),
        name="hgrn2",
    )(proj, proj, proj, proj, lb_logits, norm_w.reshape(1, d_h), tri3, pair_sel, diag_mask)


def _retention_kernel(q_ref, k_ref, v_ref, g_ref, cos_ref, sin_ref, intra_ref, edge_ref, nw_ref, o_ref, state_ref):
    C = RET_CHUNK

    @pl.when(pl.program_id(2) == 0)
    def _():
        state_ref[...] = jnp.zeros_like(state_ref)

    def rope(x, cos, sin_signed):
        return x * cos + pltpu.roll(x, HEAD_DIM // 2, 1) * sin_signed

    def head_chunk(sl, lanes, head):
        cos, sin_signed = cos_ref[sl, :], sin_ref[sl, :]
        q = rope(q_ref[0, sl, lanes], cos, sin_signed)
        k = rope(k_ref[0, sl, lanes], cos, sin_signed) * (HEAD_DIM ** -0.5)
        v = v_ref[0, sl, lanes].astype(BF16)
        state = state_ref[head]
        w_start, w_end, decay_chunk = edge_ref[head, 0], edge_ref[head, 1], edge_ref[head, 2, 0:1, :]

        scores = lax.dot_general(q.astype(BF16), k.astype(BF16), NT_DIMS,
                                 preferred_element_type=F32) * intra_ref[head]
        o = jnp.dot(scores.astype(BF16), v, preferred_element_type=F32)
        o = o + lax.dot_general((q * w_start).astype(BF16), state.astype(BF16), NT_DIMS,
                                preferred_element_type=F32)
        state_ref[head] = decay_chunk * state + lax.dot_general(v, (k * w_end).astype(BF16), TN_DIMS,
                                                                preferred_element_type=F32)

        cen = o - jnp.mean(o, axis=-1, keepdims=True)
        y = cen * lax.rsqrt(jnp.mean(cen * cen, axis=-1, keepdims=True) + EPS) * nw_ref[:, lanes]
        o_ref[0, sl, lanes] = (y * _silu(g_ref[0, sl, lanes])).astype(o_ref.dtype)

    def chunk(c, carry):
        sl = pl.ds(pl.multiple_of(c * C, C), C)
        for head in range(RET_HEADS_PER_STEP):
            head_chunk(sl, slice(head * HEAD_DIM, (head + 1) * HEAD_DIM), head)
        return carry

    lax.fori_loop(0, MIX_ROWS // C, chunk, 0, unroll=MIX_UNROLL)


def _retention(proj, norm_w, first_group):
    bsz, seq, _ = proj.shape
    d_r = norm_w.shape[0]
    n_heads = d_r // HEAD_DIM
    half = HEAD_DIM // 2

    inv = ROPE_BASE ** (-jnp.arange(0, HEAD_DIM, 2, dtype=F32) / HEAD_DIM)
    ang = jnp.arange(seq).astype(F32)[:, None] * inv[None, :]
    cos = jnp.concatenate([jnp.cos(ang), jnp.cos(ang)], axis=-1)
    sin_signed = jnp.concatenate([-jnp.sin(ang), jnp.sin(ang)], axis=-1)
    assert cos.shape == (seq, 2 * half)
    C = RET_CHUNK
    log_gamma = jnp.log1p(-jnp.exp2(-RET_DECAY_BASE - jnp.arange(n_heads, dtype=F32)))
    pos = jnp.arange(C, dtype=F32)
    rel = pos[:, None] - pos[None, :]
    intra = jnp.where(rel >= 0, jnp.exp(log_gamma[:, None, None] * jnp.maximum(rel, 0.0)), 0.0)
    w_start = jnp.exp(log_gamma[:, None] * (pos + 1.0)[None, :])
    w_end = jnp.exp(log_gamma[:, None] * (C - 1.0 - pos)[None, :])
    decay_chunk = jnp.broadcast_to(jnp.exp(log_gamma * C)[:, None], (n_heads, C))
    edge = jnp.broadcast_to(jnp.stack([w_start, w_end, decay_chunk], axis=1)[..., None],
                            (n_heads, 3, C, HEAD_DIM))

    per_step = RET_HEADS_PER_STEP
    width = per_step * HEAD_DIM
    n_groups = n_heads // per_step
    first = first_group * n_groups

    def col(group):
        return pl.BlockSpec((1, MIX_ROWS, width), lambda b, h, s: (b, s, first + group * n_groups + h))

    table = pl.BlockSpec((MIX_ROWS, HEAD_DIM), lambda b, h, s: (s, 0))
    return pl.pallas_call(
        _retention_kernel,
        grid=(bsz, n_groups, seq // MIX_ROWS),
        in_specs=[col(0), col(1), col(2), col(3), table, table,
                  pl.BlockSpec((per_step, C, C), lambda b, h, s: (h, 0, 0)),
                  pl.BlockSpec((per_step, 3, C, HEAD_DIM), lambda b, h, s: (h, 0, 0, 0)),
                  pl.BlockSpec((1, width), lambda b, h, s: (0, h))],
        out_specs=pl.BlockSpec((1, MIX_ROWS, width), lambda b, h, s: (b, s, h)),
        out_shape=jax.ShapeDtypeStruct((bsz, seq, d_r), BF16),
        scratch_shapes=[pltpu.VMEM((per_step, HEAD_DIM, HEAD_DIM), F32)],
        compiler_params=_params("parallel", "parallel", "arbitrary"),
        name="retention",
    )(proj, proj, proj, proj, cos, sin_signed, intra, edge, norm_w.reshape(1, d_r))


def _forget_scan_kernel(f_ref, bias_ref, c_ref, carry_ref):
    @pl.when(pl.program_id(1) == 0)
    def _():
        carry_ref[...] = jnp.zeros_like(carry_ref)

    log_f = _log_sigmoid(f_ref[0] + bias_ref[...])
    c = jnp.dot(_lower_tri(SCAN_ROWS), log_f, precision=lax.Precision.HIGHEST,
                preferred_element_type=F32) + carry_ref[...]
    c_ref[0] = c
    carry_ref[...] = c[SCAN_ROWS - 1:SCAN_ROWS, :]


def _forget_scan(f, bias):
    bsz, seq, lanes = f.shape
    return pl.pallas_call(
        _forget_scan_kernel,
        grid=(bsz, seq // SCAN_ROWS),
        in_specs=[pl.BlockSpec((1, SCAN_ROWS, lanes), lambda b, s: (b, s, 0)),
                  pl.BlockSpec((1, lanes), lambda b, s: (0, 0))],
        out_specs=pl.BlockSpec((1, SCAN_ROWS, lanes), lambda b, s: (b, s, 0)),
        out_shape=jax.ShapeDtypeStruct((bsz, seq, lanes), F32),
        scratch_shapes=[pltpu.VMEM((1, lanes), F32)],
        compiler_params=_params("parallel", "arbitrary"),
        name="forget_scan",
    )(f, bias)


def _split3(x):
    hi = x.astype(BF16).astype(F32)
    rest = x - hi
    mid = rest.astype(BF16).astype(F32)
    low = (rest - mid).astype(BF16).astype(F32)
    return hi, mid, low


def _fox_kernel(pairs_ref, q_ref, k_ref, v_ref, g_ref, ct_ref, o_ref,
                kaug_ref, vt_ref, qaug_ref, m_ref, acc_ref, s0_ref, s1_ref, p0_ref, p1_ref, a0_ref, a1_ref):
    T = FOX_BLOCK
    n_blocks = vt_ref.shape[0]
    n_pairs = n_blocks * (n_blocks + 1) // 2
    aug_row = lax.broadcasted_iota(jnp.int32, (HEAD_DIM, T), 0)

    def bias_rows(c_row, for_queries):
        hi, mid, low = _split3(c_row)
        if for_queries:
            pieces = jnp.where(aug_row == 3, hi, jnp.where(aug_row == 4, mid, jnp.where(aug_row == 5, low, 0.0)))
            return jnp.where(aug_row < 3, 1.0, pieces)
        pieces = jnp.where(aug_row == 0, -hi, jnp.where(aug_row == 1, -mid, jnp.where(aug_row == 2, -low, 0.0)))
        return jnp.where((aug_row >= 3) & (aug_row < 6), 1.0, pieces)

    for j in range(n_blocks):
        rows = slice(j * T, (j + 1) * T)
        c_j = ct_ref[0, 0, j:j + 1, :] * LOG2_E
        kaug_ref[rows, 0:HEAD_DIM] = k_ref[0, rows, :]
        kaug_ref[rows, HEAD_DIM:2 * HEAD_DIM] = bias_rows(c_j, False).T.astype(BF16)
        vt_ref[j, 0:HEAD_DIM, :] = v_ref[0, rows, :].astype(F32).T.astype(BF16)
        vt_ref[j, HEAD_DIM:, :] = jnp.ones((FOX_ONES_ROWS, T), BF16)
        qaug_ref[j, 0:HEAD_DIM, :] = (q_ref[0, rows, :].astype(F32) * (HEAD_DIM ** -0.5 * LOG2_E)).T.astype(BF16)
        qaug_ref[j, HEAD_DIM:2 * HEAD_DIM, :] = bias_rows(c_j, True).astype(BF16)
    m_ref[...] = jnp.full(m_ref.shape, MASK_VALUE, F32)
    acc_ref[...] = jnp.zeros(acc_ref.shape, F32)
    for p_ref, a_ref in ((p0_ref, a0_ref), (p1_ref, a1_ref)):
        p_ref[...] = jnp.zeros((T, T), BF16)
        a_ref[...] = jnp.ones((1, T), F32)

    key_id = lax.broadcasted_iota(jnp.int32, (T, T), 0)
    query_id = lax.broadcasted_iota(jnp.int32, (T, T), 1)

    def form_logits(pair, s_ref):
        qb, kb = pair
        keys = pl.ds(pl.multiple_of(kb * T, T), T)
        s_ref[...] = jnp.dot(kaug_ref[keys, :], qaug_ref[qb], preferred_element_type=F32)

    def softmax_update(pair, valid, diagonal, s_ref, p_ref, a_ref):
        qb, _ = pair
        s = s_ref[...]
        if diagonal:
            s = jnp.where(key_id <= query_id, s, MASK_VALUE)
        m = m_ref[qb]
        m_new = jnp.maximum(m, jnp.where(valid, jnp.max(s, axis=0, keepdims=True), MASK_VALUE))
        p_ref[...] = jnp.exp2(s - jnp.where(valid, m_new, -MASK_VALUE)).astype(BF16)
        a_ref[...] = jnp.exp2(m - m_new)
        m_ref[qb] = m_new

    def accumulate(pair, p_ref, a_ref):
        qb, kb = pair
        acc_ref[qb] = a_ref[...] * acc_ref[qb] + jnp.dot(vt_ref[kb], p_ref[...], preferred_element_type=F32)

    def pair_at(n):
        n = jnp.clip(n, 0, n_pairs - 1)
        return pairs_ref[0, n], pairs_ref[1, n]

    def tick(n, diagonal, s_now, p_now, a_now, s_next, p_prev, a_prev):
        form_logits(pair_at(n + 1), s_next)
        softmax_update(pair_at(n), n < n_pairs, diagonal, s_now, p_now, a_now)
        accumulate(pair_at(n - 1), p_prev, a_prev)

    def step(i, is_diagonal):
        for k in range(0, FOX_TICKS_PER_STEP, 2):
            n = FOX_TICKS_PER_STEP * i + k
            tick(n, is_diagonal(n), s0_ref, p0_ref, a0_ref, s1_ref, p1_ref, a1_ref)
            tick(n + 1, is_diagonal(n + 1), s1_ref, p1_ref, a1_ref, s0_ref, p0_ref, a0_ref)

    def diagonal_step(i, carry):
        step(i, lambda n: True)
        return carry

    def below_step(i, carry):
        step(i, lambda n: False)
        return carry

    n_steps = -(-(n_pairs + 2) // FOX_TICKS_PER_STEP)
    diagonal_steps = n_blocks // FOX_TICKS_PER_STEP
    form_logits(pair_at(0), s0_ref)
    lax.fori_loop(0, diagonal_steps, diagonal_step, 0)
    if n_blocks % FOX_TICKS_PER_STEP:
        step(diagonal_steps, lambda n: n < n_blocks)
        diagonal_steps += 1
    lax.fori_loop(diagonal_steps, n_steps, below_step, 0)

    for j in range(n_blocks):
        rows = slice(j * T, (j + 1) * T)
        acc = acc_ref[j]
        o = (acc[:HEAD_DIM, :] / acc[HEAD_DIM:HEAD_DIM + 1, :]).T
        o_ref[0, rows, :] = (o * _sigmoid(g_ref[0, rows, :].astype(F32))).astype(o_ref.dtype)


def _fox(proj, c, n_heads):
    bsz, seq, _ = proj.shape
    T = FOX_BLOCK
    n_blocks = seq // T
    c_rows = jnp.transpose(c[:, :, :n_heads], (0, 2, 1)).reshape(bsz, n_heads, n_blocks, T)
    causal_pairs = ([(qb, qb) for qb in range(n_blocks)]
                    + [(qb, kb) for qb in range(n_blocks) for kb in range(qb)])
    pairs = jnp.asarray(causal_pairs, jnp.int32).T

    def head_cols(group):
        return pl.BlockSpec((1, seq, HEAD_DIM), lambda b, h: (b, 0, group * n_heads + h))

    return pl.pallas_call(
        _fox_kernel,
        grid=(bsz, n_heads),
        in_specs=[pl.BlockSpec(memory_space=pltpu.SMEM),
                  head_cols(0), head_cols(1), head_cols(2), head_cols(3),
                  pl.BlockSpec((1, 1, n_blocks, T), lambda b, h: (b, h, 0, 0))],
        out_specs=pl.BlockSpec((1, seq, HEAD_DIM), lambda b, h: (b, 0, h)),
        out_shape=jax.ShapeDtypeStruct((bsz, seq, n_heads * HEAD_DIM), BF16),
        scratch_shapes=[pltpu.VMEM((seq, 2 * HEAD_DIM), BF16),
                        pltpu.VMEM((n_blocks, HEAD_DIM + FOX_ONES_ROWS, T), BF16),
                        pltpu.VMEM((n_blocks, 2 * HEAD_DIM, T), BF16),
                        pltpu.VMEM((n_blocks, 1, T), F32),
                        pltpu.VMEM((n_blocks, HEAD_DIM + FOX_ONES_ROWS, T), F32),
                        pltpu.VMEM((T, T), F32), pltpu.VMEM((T, T), F32),
                        pltpu.VMEM((T, T), BF16), pltpu.VMEM((T, T), BF16),
                        pltpu.VMEM((1, T), F32), pltpu.VMEM((1, T), F32)],
        compiler_params=_params("parallel", "parallel"),
        name="fox_attention",
    )(pairs, proj, proj, proj, proj, c_rows)


def _ffn_up_kernel(x_ref, wg_ref, wv_ref, cg_ref, cv_ref, o_ref, ug_ref, uv_ref, wgb_ref, wvb_ref,
                   *, seq_steps, d_ff):
    rows = x_ref.shape[0]

    @pl.when(pl.program_id(1) == 0)
    def _():
        valid = d_ff - pl.program_id(0) * wg_ref.shape[1]
        col = lax.broadcasted_iota(jnp.int32, wg_ref.shape, 1)
        wgb_ref[...] = jnp.where(col < valid, wg_ref[...], 0.0).astype(BF16)
        wvb_ref[...] = jnp.where(col < valid, wv_ref[...], 0.0).astype(BF16)

    @pl.when(pl.program_id(1) % seq_steps == 0)
    def _():
        ug_ref[0:SUBLANES, :] = jnp.zeros((SUBLANES, ug_ref.shape[1]), F32)
        uv_ref[0:SUBLANES, :] = jnp.zeros((SUBLANES, uv_ref.shape[1]), F32)

    x = x_ref[...]

    def conv(wb_ref, c_ref, u_ref):
        u_ref[SUBLANES:SUBLANES + rows, :] = jnp.dot(x, wb_ref[...], preferred_element_type=F32)
        out = c_ref[CONV_WIDTH - 1:CONV_WIDTH, :] * u_ref[SUBLANES:SUBLANES + rows, :]
        for tap in range(CONV_WIDTH - 1):
            back = CONV_WIDTH - 1 - tap
            out = out + c_ref[tap:tap + 1, :] * u_ref[SUBLANES - back:SUBLANES - back + rows, :]
        carry = u_ref[rows:rows + SUBLANES, :]
        u_ref[0:SUBLANES, :] = carry
        return out

    gate = conv(wgb_ref, cg_ref, ug_ref)
    val = conv(wvb_ref, cv_ref, uv_ref)
    o_ref[...] = (_silu(gate) * val).astype(o_ref.dtype)


def _ffn_up(h, w_up, layer, conv_w, seq, d_ff, d_ff_pad):
    t, d = h.shape
    rows = min(MM_ROWS, seq)
    assert seq % rows == 0 and d_ff_pad % FFN_COLS == 0
    n_cols = d_ff_pad // FFN_COLS
    return pl.pallas_call(
        functools.partial(_ffn_up_kernel, seq_steps=seq // rows, d_ff=d_ff),
        grid=(n_cols, t // rows),
        in_specs=[pl.BlockSpec((rows, d), lambda j, i: (i, 0)),
                  pl.BlockSpec((None, d, FFN_COLS), lambda j, i: (layer, 0, j)),
                  pl.BlockSpec((None, pl.Element(d), pl.Element(FFN_COLS, (0, d_ff_pad - d_ff))),
                               lambda j, i: (layer, 0, pl.multiple_of(d_ff + j * FFN_COLS, LANES))),
                  pl.BlockSpec((CONV_WIDTH, FFN_COLS), lambda j, i: (0, j)),
                  pl.BlockSpec((CONV_WIDTH, FFN_COLS), lambda j, i: (0, j + n_cols))],
        out_specs=pl.BlockSpec((rows, FFN_COLS), lambda j, i: (i, j)),
        out_shape=jax.ShapeDtypeStruct((t, d_ff), BF16),
        scratch_shapes=[pltpu.VMEM((rows + SUBLANES, FFN_COLS), F32),
                        pltpu.VMEM((rows + SUBLANES, FFN_COLS), F32),
                        pltpu.VMEM((d, FFN_COLS), BF16),
                        pltpu.VMEM((d, FFN_COLS), BF16)],
        compiler_params=_params("parallel", "arbitrary"),
        name="ffn_up_conv",
    )(h, w_up, w_up, conv_w, conv_w)


def _pad_cols(w, width):
    return jnp.pad(w, ((0, 0), (0, width - w.shape[1])))


def kernel(x, attn_norm_w, ffn_norm_w, final_norm_w, even_w_in, hgrn_lb_logits, hgrn_norm_w, ret_norm_w,
           even_w_out, odd_w_in, odd_b_f, odd_w_out, ffn_w_up, ffn_conv_w, ffn_w_down):
    bsz, seq, d_model = x.shape
    depth = attn_norm_w.shape[0]
    tokens = bsz * seq
    d_hgrn = hgrn_norm_w.shape[1]
    d_fox = odd_w_out.shape[1]
    n_fox = odd_b_f.shape[1]
    d_ff = ffn_w_down.shape[1]
    d_ff_pad = FFN_COLS * math.ceil(d_ff / FFN_COLS)

    xf = x.reshape(tokens, d_model)
    h = _rmsnorm(xf, attn_norm_w[0], BF16)
    for layer in range(depth):
        j = layer // 2
        if layer % 2 == 0:
            proj = _matmul(h, even_w_in, j, F32, even_w_in.shape[2], cols=IN_COLS,
                           name="even_in").reshape(bsz, seq, -1)
            o_a = _hgrn(proj, hgrn_lb_logits, hgrn_norm_w[j], j)
            o_b = _retention(proj, ret_norm_w[j], first_group=4 * d_hgrn // ret_norm_w.shape[1])
            mixed = [o_a.reshape(tokens, -1), o_b.reshape(tokens, -1)]
            w_out = even_w_out[j]
        else:
            proj = _matmul(h, odd_w_in, j, BF16, 4 * d_fox, cols=IN_COLS, name="fox_in").reshape(bsz, seq, -1)
            f = _matmul(h, odd_w_in, j, F32, LANES, cols=LANES, first_col=4 * d_fox, name="fox_forget")
            c = _forget_scan(f.reshape(bsz, seq, LANES), _pad_cols(odd_b_f[j][None, :], LANES))
            mixed = [_fox(proj, c, n_fox).reshape(tokens, -1)]
            w_out = odd_w_out[j]
        xf, h = _matmul_residual_norm(mixed, w_out.astype(BF16), xf, ffn_norm_w[layer], BF16, rows=OUT_ROWS,
                                      name="mixer_out_norm")
        conv_w = ffn_conv_w[layer]
        conv_w = jnp.concatenate([_pad_cols(conv_w[:, :d_ff], d_ff_pad), _pad_cols(conv_w[:, d_ff:], d_ff_pad)],
                                 axis=1)
        act = _ffn_up(h, ffn_w_up, layer, conv_w, seq, d_ff, d_ff_pad)
        w_down = ffn_w_down[layer].astype(BF16)
        last = layer == depth - 1
        xf, h = _matmul_residual_norm([act], w_down, xf, final_norm_w if last else attn_norm_w[layer + 1],
                                      x.dtype if last else BF16, rows=DOWN_ROWS, keep_sum=not last,
                                      name="ffn_down_norm")
    return h.reshape(bsz, seq, d_model)
```

```python
import functools
import math

import jax
import jax.numpy as jnp
from jax import lax
from jax.experimental import pallas as pl
from jax.experimental.pallas import tpu as pltpu

F32 = jnp.float32
BF16 = jnp.bfloat16

HEAD_DIM = 128
ROPE_BASE = 10000.0
RET_DECAY_BASE = 5.0
EPS = 1e-6
LB_FLOOR = 1e-30
MASK_VALUE = -1e30
CONV_WIDTH = 3
LOG2_E = math.log2(math.e)

LANES = 128
SUBLANES = 8
VMEM_LIMIT_BYTES = 48 * 1024 * 1024

NORM_ROWS = 256
MM_ROWS = 1024
IN_COLS = 1024
OUT_ROWS = 512
DOWN_ROWS = 256
FFN_COLS = 512
MIX_ROWS = 512
HGRN_UNROLL = 4
RET_UNROLL = 2
HGRN_CHUNK = 64
HGRN_SUB = 16
HGRN_HEADS_PER_STEP = 8
RET_CHUNK = 128
RET_HEADS_PER_STEP = 4
FOX_BLOCK = 512
FOX_TICKS_PER_STEP = 8
FOX_ONES_ROWS = 16
SCAN_ROWS = 256

NT_DIMS = (((1,), (1,)), ((), ()))
TN_DIMS = (((0,), (0,)), ((), ()))


def _params(*semantics):
    return pltpu.CompilerParams(dimension_semantics=semantics, vmem_limit_bytes=VMEM_LIMIT_BYTES)


def _sigmoid(x):
    return 1.0 / (1.0 + jnp.exp(-x))


def _silu(x):
    return x * _sigmoid(x)


def _log_sigmoid(x):
    return jnp.minimum(x, 0.0) - jnp.log1p(jnp.exp(-jnp.abs(x)))


def _lower_tri(n):
    rows = lax.broadcasted_iota(jnp.int32, (n, n), 0)
    cols = lax.broadcasted_iota(jnp.int32, (n, n), 1)
    return (rows >= cols).astype(F32)


def _rmsnorm_kernel(x_ref, w_ref, o_ref):
    x = x_ref[...]
    ms = jnp.mean(x * x, axis=-1, keepdims=True)
    o_ref[...] = (x * lax.rsqrt(ms + EPS) * w_ref[...]).astype(o_ref.dtype)


def _rmsnorm(x, w, out_dtype):
    t, d = x.shape
    return pl.pallas_call(
        _rmsnorm_kernel,
        grid=(t // NORM_ROWS,),
        in_specs=[pl.BlockSpec((NORM_ROWS, d), lambda i: (i, 0)),
                  pl.BlockSpec((1, d), lambda i: (0, 0))],
        out_specs=pl.BlockSpec((NORM_ROWS, d), lambda i: (i, 0)),
        out_shape=jax.ShapeDtypeStruct((t, d), out_dtype),
        compiler_params=_params("parallel"),
        name="rmsnorm",
    )(x, w.reshape(1, d))


def _cast_weight_tile(w_ref, wb_ref, valid_cols):
    w = w_ref[...]
    if valid_cols < w.shape[1]:
        col = lax.broadcasted_iota(jnp.int32, w.shape, 1)
        w = jnp.where(col < valid_cols, w, 0.0)
    wb_ref[...] = w.astype(BF16)


def _matmul_kernel(x_ref, w_ref, o_ref, wb_ref):
    @pl.when(pl.program_id(1) == 0)
    def _():
        _cast_weight_tile(w_ref, wb_ref, w_ref.shape[1])

    o_ref[...] = jnp.dot(x_ref[...], wb_ref[...], preferred_element_type=F32).astype(o_ref.dtype)


def _matmul(x, w, layer, out_dtype, n_out, rows, cols, name):
    t, k = x.shape
    rows = min(rows, t)
    assert t % rows == 0 and n_out % cols == 0 and n_out <= w.shape[2]
    return pl.pallas_call(
        _matmul_kernel,
        grid=(n_out // cols, t // rows),
        in_specs=[pl.BlockSpec((rows, k), lambda j, i: (i, 0)),
                  pl.BlockSpec((None, k, cols), lambda j, i: (layer, 0, j))],
        out_specs=pl.BlockSpec((rows, cols), lambda j, i: (i, j)),
        out_shape=jax.ShapeDtypeStruct((t, n_out), out_dtype),
        scratch_shapes=[pltpu.VMEM((k, cols), BF16)],
        compiler_params=_params("parallel", "arbitrary"),
        name=name,
    )(x, w)


def _residual_norm_kernel(*refs, n_parts, keep_sum):
    x_refs, (w_ref, r_ref, nw_ref), out_refs = refs[:n_parts], refs[n_parts:n_parts + 3], refs[n_parts + 3:]
    y = r_ref[...]
    k0 = 0
    for x_ref in x_refs:
        k1 = k0 + x_ref.shape[1]
        y = y + jnp.dot(x_ref[...], w_ref[k0:k1, :], preferred_element_type=F32)
        k0 = k1
    if keep_sum:
        out_refs[0][...] = y
    h_ref = out_refs[-1]
    ms = jnp.mean(y * y, axis=-1, keepdims=True)
    h_ref[...] = (y * lax.rsqrt(ms + EPS) * nw_ref[...]).astype(h_ref.dtype)


def _matmul_residual_norm(xs, w, residual, norm_w, h_dtype, rows, keep_sum=True, name="matmul_residual_norm"):
    t = xs[0].shape[0]
    k, n = w.shape
    assert t % rows == 0 and sum(x.shape[1] for x in xs) == k
    row_block = pl.BlockSpec((rows, n), lambda i: (i, 0))
    out_shape = [jax.ShapeDtypeStruct((t, n), h_dtype)]
    if keep_sum:
        out_shape.insert(0, jax.ShapeDtypeStruct((t, n), F32))
    outs = pl.pallas_call(
        functools.partial(_residual_norm_kernel, n_parts=len(xs), keep_sum=keep_sum),
        grid=(t // rows,),
        in_specs=[pl.BlockSpec((rows, x.shape[1]), lambda i: (i, 0)) for x in xs]
        + [pl.BlockSpec((k, n), lambda i: (0, 0), pipeline_mode=pl.Buffered(1)),
           row_block,
           pl.BlockSpec((1, n), lambda i: (0, 0))],
        out_specs=[row_block] * len(out_shape),
        out_shape=out_shape,
        compiler_params=_params("parallel"),
        name=name,
    )(*xs, w, residual, norm_w.reshape(1, n))
    return (outs[0], outs[1]) if keep_sum else (None, outs[0])


def _hgrn_kernel(q_ref, f_ref, i_ref, g_ref, lbl_ref, nw_ref, tri3_ref, sel_ref, mask_ref, o_ref, state_ref,
                 *, layer_j):
    C, L = HGRN_CHUNK, HGRN_SUB
    n_sub = C // L

    @pl.when(pl.program_id(2) == 0)
    def _():
        state_ref[...] = jnp.zeros_like(state_ref)

    lbl = lbl_ref[...]
    e = jnp.exp(lbl - jnp.max(lbl, axis=0, keepdims=True))
    sm = e / jnp.sum(e, axis=0, keepdims=True)
    lb_all = jnp.zeros((1, lbl.shape[1]), F32)
    for r in range(1, layer_j + 1):
        lb_all = lb_all + sm[r:r + 1, :]

    def pair_products(q, k, b, s):
        first = (s // SUBLANES) * SUBLANES
        q3, k3, b3 = (x.reshape(n_sub, L, HEAD_DIM) for x in (q, k, b))
        decay = jnp.exp2(jnp.minimum(b3[:, first:, :] - b3[:, s:s + 1, :], 0.0))
        prod = q3[:, first:, :] * k3[:, s:s + 1, :] * decay
        if first:
            prod = jnp.concatenate([jnp.zeros((n_sub, first, HEAD_DIM), F32), prod], axis=1)
        return prod.reshape(C, HEAD_DIM).astype(BF16)

    def head_chunk(rows, lanes, head):
        lb_floor, one_m_lb = jnp.maximum(lb_all[:, lanes], LB_FLOOR), 1.0 - lb_all[:, lanes]
        z = f_ref[0, rows, lanes]
        q = _silu(q_ref[0, rows, lanes])
        v = i_ref[0, rows, lanes].astype(BF16)

        e_z = jnp.exp(-jnp.abs(z))
        inv = 1.0 / (1.0 + e_z)
        log_f = jnp.log(lb_floor + one_m_lb * (jnp.where(z >= 0.0, 1.0, e_z) * inv))
        k = one_m_lb * (jnp.where(z >= 0.0, e_z, 1.0) * inv)
        hi = log_f.astype(BF16)
        rest = log_f - hi.astype(F32)
        mid = rest.astype(BF16)
        low = (rest - mid.astype(F32)).astype(BF16)
        b = jnp.dot(tri3_ref[...], jnp.concatenate([hi, mid, low], axis=0),
                    preferred_element_type=F32) * LOG2_E
        b_last = b[C - 1:C, :]

        state = state_ref[head]
        o = lax.dot_general((q * jnp.exp2(b)).astype(BF16), state.astype(BF16), NT_DIMS,
                            preferred_element_type=F32)

        below = [jnp.zeros((L, C), F32)]
        for i in range(1, n_sub):
            lo = i * L
            edge = b[lo - 1:lo, :]
            q_edge = (q[lo:lo + L, :] * jnp.exp2(b[lo:lo + L, :] - edge)).astype(BF16)
            k_edge = (k[:lo, :] * jnp.exp2(edge - b[:lo, :])).astype(BF16)
            k_edge = jnp.concatenate([k_edge, jnp.zeros((C - lo, HEAD_DIM), BF16)], axis=0)
            below.append(lax.dot_general(q_edge, k_edge, NT_DIMS, preferred_element_type=F32))
        pair = jnp.concatenate([pair_products(q, k, b, s) for s in range(L)], axis=1)
        diag = jnp.dot(pair, sel_ref[...], preferred_element_type=F32)
        scores = jnp.concatenate(below, axis=0) + diag * mask_ref[...]
        o = o + jnp.dot(scores.astype(BF16), v, preferred_element_type=F32)

        k_end = (k * jnp.exp2(b_last - b)).astype(BF16)
        state_ref[head] = state * jnp.exp2(b_last) + lax.dot_general(v, k_end, TN_DIMS,
                                                                     preferred_element_type=F32)

        y = o * lax.rsqrt(jnp.mean(o * o, axis=-1, keepdims=True) + EPS) * nw_ref[:, lanes]
        o_ref[0, rows, lanes] = (y * _silu(g_ref[0, rows, lanes])).astype(o_ref.dtype)

    def chunk(c, carry):
        rows = pl.ds(pl.multiple_of(c * C, C), C)
        for head in range(HGRN_HEADS_PER_STEP):
            head_chunk(rows, slice(head * HEAD_DIM, (head + 1) * HEAD_DIM), head)
        return carry

    lax.fori_loop(0, MIX_ROWS // C, chunk, 0, unroll=HGRN_UNROLL)


def _hgrn(proj, lb_logits, norm_w, layer_j):
    bsz, seq, _ = proj.shape
    d_h = norm_w.shape[0]
    width = HGRN_HEADS_PER_STEP * HEAD_DIM
    n_groups = d_h // width
    n_layers = lb_logits.shape[0]

    C, L = HGRN_CHUNK, HGRN_SUB
    pos = jnp.arange(C)
    causal = pos[:, None] >= pos[None, :]
    tri3 = jnp.tile(causal.astype(BF16), (1, 3))
    pair_sel = (jnp.repeat(jnp.arange(L), HEAD_DIM)[:, None] == (pos % L)[None, :]).astype(BF16)
    diag_mask = (causal & (pos[:, None] // L == pos[None, :] // L)).astype(F32)

    def col(group):
        return pl.BlockSpec((1, MIX_ROWS, width), lambda b, h, s: (b, s, group * n_groups + h))

    def whole(a):
        return pl.BlockSpec(a.shape, lambda b, h, s: (0, 0))

    return pl.pallas_call(
        functools.partial(_hgrn_kernel, layer_j=layer_j),
        grid=(bsz, n_groups, seq // MIX_ROWS),
        in_specs=[col(0), col(1), col(2), col(3),
                  pl.BlockSpec((n_layers, width), lambda b, h, s: (0, h)),
                  pl.BlockSpec((1, width), lambda b, h, s: (0, h)),
                  whole(tri3), whole(pair_sel), whole(diag_mask)],
        out_specs=pl.BlockSpec((1, MIX_ROWS, width), lambda b, h, s: (b, s, h)),
        out_shape=jax.ShapeDtypeStruct((bsz, seq, d_h), BF16),
        scratch_shapes=[pltpu.VMEM((HGRN_HEADS_PER_STEP, HEAD_DIM, HEAD_DIM), F32)],
        compiler_params=_params("parallel", "parallel", "arbitrary"),
        name="hgrn2",
    )(proj, proj, proj, proj, lb_logits, norm_w.reshape(1, d_h), tri3, pair_sel, diag_mask)


def _retention_kernel(q_ref, k_ref, v_ref, g_ref, cos_ref, sin_ref, intra_ref, edge_ref, nw_ref, o_ref, state_ref):
    C = RET_CHUNK

    @pl.when(pl.program_id(2) == 0)
    def _():
        state_ref[...] = jnp.zeros_like(state_ref)

    def rope(x, cos, sin_signed):
        return x * cos + pltpu.roll(x, HEAD_DIM // 2, 1) * sin_signed

    def head_chunk(sl, lanes, head):
        cos, sin_signed = cos_ref[sl, :], sin_ref[sl, :]
        q = rope(q_ref[0, sl, lanes], cos, sin_signed)
        k = rope(k_ref[0, sl, lanes], cos, sin_signed) * (HEAD_DIM ** -0.5)
        v = v_ref[0, sl, lanes].astype(BF16)
        state = state_ref[head]
        w_start, w_end, decay_chunk = edge_ref[head, 0], edge_ref[head, 1], edge_ref[head, 2, 0:1, :]

        scores = lax.dot_general(q.astype(BF16), k.astype(BF16), NT_DIMS,
                                 preferred_element_type=F32) * intra_ref[head]
        o = jnp.dot(scores.astype(BF16), v, preferred_element_type=F32)
        o = o + lax.dot_general((q * w_start).astype(BF16), state.astype(BF16), NT_DIMS,
                                preferred_element_type=F32)
        state_ref[head] = decay_chunk * state + lax.dot_general(v, (k * w_end).astype(BF16), TN_DIMS,
                                                                preferred_element_type=F32)

        cen = o - jnp.mean(o, axis=-1, keepdims=True)
        y = cen * lax.rsqrt(jnp.mean(cen * cen, axis=-1, keepdims=True) + EPS) * nw_ref[:, lanes]
        o_ref[0, sl, lanes] = (y * _silu(g_ref[0, sl, lanes])).astype(o_ref.dtype)

    def chunk(c, carry):
        sl = pl.ds(pl.multiple_of(c * C, C), C)
        for head in range(RET_HEADS_PER_STEP):
            head_chunk(sl, slice(head * HEAD_DIM, (head + 1) * HEAD_DIM), head)
        return carry

    lax.fori_loop(0, MIX_ROWS // C, chunk, 0, unroll=RET_UNROLL)


def _retention(proj, norm_w, first_group):
    bsz, seq, _ = proj.shape
    d_r = norm_w.shape[0]
    n_heads = d_r // HEAD_DIM
    half = HEAD_DIM // 2

    inv = ROPE_BASE ** (-jnp.arange(0, HEAD_DIM, 2, dtype=F32) / HEAD_DIM)
    ang = jnp.arange(seq).astype(F32)[:, None] * inv[None, :]
    cos = jnp.concatenate([jnp.cos(ang), jnp.cos(ang)], axis=-1)
    sin_signed = jnp.concatenate([-jnp.sin(ang), jnp.sin(ang)], axis=-1)
    assert cos.shape == (seq, 2 * half)
    C = RET_CHUNK
    log_gamma = jnp.log1p(-jnp.exp2(-RET_DECAY_BASE - jnp.arange(n_heads, dtype=F32)))
    pos = jnp.arange(C, dtype=F32)
    rel = pos[:, None] - pos[None, :]
    intra = jnp.where(rel >= 0, jnp.exp(log_gamma[:, None, None] * jnp.maximum(rel, 0.0)), 0.0)
    w_start = jnp.exp(log_gamma[:, None] * (pos + 1.0)[None, :])
    w_end = jnp.exp(log_gamma[:, None] * (C - 1.0 - pos)[None, :])
    decay_chunk = jnp.broadcast_to(jnp.exp(log_gamma * C)[:, None], (n_heads, C))
    edge = jnp.broadcast_to(jnp.stack([w_start, w_end, decay_chunk], axis=1)[..., None],
                            (n_heads, 3, C, HEAD_DIM))

    per_step = RET_HEADS_PER_STEP
    width = per_step * HEAD_DIM
    n_groups = n_heads // per_step
    first = first_group * n_groups

    def col(group):
        return pl.BlockSpec((1, MIX_ROWS, width), lambda b, h, s: (b, s, first + group * n_groups + h))

    table = pl.BlockSpec((MIX_ROWS, HEAD_DIM), lambda b, h, s: (s, 0))
    return pl.pallas_call(
        _retention_kernel,
        grid=(bsz, n_groups, seq // MIX_ROWS),
        in_specs=[col(0), col(1), col(2), col(3), table, table,
                  pl.BlockSpec((per_step, C, C), lambda b, h, s: (h, 0, 0)),
                  pl.BlockSpec((per_step, 3, C, HEAD_DIM), lambda b, h, s: (h, 0, 0, 0)),
                  pl.BlockSpec((1, width), lambda b, h, s: (0, h))],
        out_specs=pl.BlockSpec((1, MIX_ROWS, width), lambda b, h, s: (b, s, h)),
        out_shape=jax.ShapeDtypeStruct((bsz, seq, d_r), BF16),
        scratch_shapes=[pltpu.VMEM((per_step, HEAD_DIM, HEAD_DIM), F32)],
        compiler_params=_params("parallel", "parallel", "arbitrary"),
        name="retention",
    )(proj, proj, proj, proj, cos, sin_signed, intra, edge, norm_w.reshape(1, d_r))


def _forget_scan_kernel(h_ref, w_ref, bias_ref, c_ref, wb_ref, carry_ref, *, valid_cols):
    @pl.when(pl.program_id(1) == 0)
    def _():
        _cast_weight_tile(w_ref, wb_ref, valid_cols)
        carry_ref[...] = jnp.zeros_like(carry_ref)

    f = jnp.dot(h_ref[0], wb_ref[...], preferred_element_type=F32)
    log_f = _log_sigmoid(f + bias_ref[...])
    c = jnp.dot(_lower_tri(SCAN_ROWS), log_f, precision=lax.Precision.HIGHEST,
                preferred_element_type=F32) + carry_ref[...]
    c_ref[0] = c
    carry_ref[...] = c[SCAN_ROWS - 1:SCAN_ROWS, :]


def _forget_scan(h, w_in, layer, first_col, bias):
    bsz, seq, d = h.shape
    assert first_col % LANES == 0 and w_in.shape[2] - first_col <= LANES
    return pl.pallas_call(
        functools.partial(_forget_scan_kernel, valid_cols=w_in.shape[2] - first_col),
        grid=(bsz, seq // SCAN_ROWS),
        in_specs=[pl.BlockSpec((1, SCAN_ROWS, d), lambda b, s: (b, s, 0)),
                  pl.BlockSpec((None, d, LANES), lambda b, s: (layer, 0, first_col // LANES)),
                  pl.BlockSpec((1, LANES), lambda b, s: (0, 0))],
        out_specs=pl.BlockSpec((1, SCAN_ROWS, LANES), lambda b, s: (b, s, 0)),
        out_shape=jax.ShapeDtypeStruct((bsz, seq, LANES), F32),
        scratch_shapes=[pltpu.VMEM((d, LANES), BF16), pltpu.VMEM((1, LANES), F32)],
        compiler_params=_params("parallel", "arbitrary"),
        name="forget_scan",
    )(h, w_in, bias)


def _split3(x):
    hi = x.astype(BF16).astype(F32)
    rest = x - hi
    mid = rest.astype(BF16).astype(F32)
    low = (rest - mid).astype(BF16).astype(F32)
    return hi, mid, low


def _fox_kernel(pairs_ref, q_ref, k_ref, v_ref, g_ref, ct_ref, o_ref,
                kaug_ref, vt_ref, qaug_ref, m_ref, acc_ref, s0_ref, s1_ref, p0_ref, p1_ref, a0_ref, a1_ref):
    T = FOX_BLOCK
    n_blocks = vt_ref.shape[0]
    n_pairs = n_blocks * (n_blocks + 1) // 2
    aug_row = lax.broadcasted_iota(jnp.int32, (HEAD_DIM, T), 0)

    def bias_rows(c_row, for_queries):
        hi, mid, low = _split3(c_row)
        if for_queries:
            pieces = jnp.where(aug_row == 3, hi, jnp.where(aug_row == 4, mid, jnp.where(aug_row == 5, low, 0.0)))
            return jnp.where(aug_row < 3, 1.0, pieces)
        pieces = jnp.where(aug_row == 0, -hi, jnp.where(aug_row == 1, -mid, jnp.where(aug_row == 2, -low, 0.0)))
        return jnp.where((aug_row >= 3) & (aug_row < 6), 1.0, pieces)

    for j in range(n_blocks):
        rows = slice(j * T, (j + 1) * T)
        c_j = ct_ref[0, 0, j:j + 1, :] * LOG2_E
        kaug_ref[rows, 0:HEAD_DIM] = k_ref[0, rows, :]
        kaug_ref[rows, HEAD_DIM:2 * HEAD_DIM] = bias_rows(c_j, False).T.astype(BF16)
        vt_ref[j, 0:HEAD_DIM, :] = v_ref[0, rows, :].astype(F32).T.astype(BF16)
        vt_ref[j, HEAD_DIM:, :] = jnp.ones((FOX_ONES_ROWS, T), BF16)
        qaug_ref[j, 0:HEAD_DIM, :] = (q_ref[0, rows, :].astype(F32) * (HEAD_DIM ** -0.5 * LOG2_E)).T.astype(BF16)
        qaug_ref[j, HEAD_DIM:2 * HEAD_DIM, :] = bias_rows(c_j, True).astype(BF16)
    m_ref[...] = jnp.full(m_ref.shape, MASK_VALUE, F32)
    acc_ref[...] = jnp.zeros(acc_ref.shape, F32)
    for p_ref, a_ref in ((p0_ref, a0_ref), (p1_ref, a1_ref)):
        p_ref[...] = jnp.zeros((T, T), BF16)
        a_ref[...] = jnp.ones((1, T), F32)

    key_id = lax.broadcasted_iota(jnp.int32, (T, T), 0)
    query_id = lax.broadcasted_iota(jnp.int32, (T, T), 1)

    def form_logits(pair, s_ref):
        qb, kb = pair
        keys = pl.ds(pl.multiple_of(kb * T, T), T)
        s_ref[...] = jnp.dot(kaug_ref[keys, :], qaug_ref[qb], preferred_element_type=F32)

    def softmax_update(pair, valid, diagonal, s_ref, p_ref, a_ref):
        qb, _ = pair
        s = s_ref[...]
        if diagonal:
            s = jnp.where(key_id <= query_id, s, MASK_VALUE)
        m = m_ref[qb]
        m_new = jnp.maximum(m, jnp.where(valid, jnp.max(s, axis=0, keepdims=True), MASK_VALUE))
        p_ref[...] = jnp.exp2(s - jnp.where(valid, m_new, -MASK_VALUE)).astype(BF16)
        a_ref[...] = jnp.exp2(m - m_new)
        m_ref[qb] = m_new

    def accumulate(pair, p_ref, a_ref):
        qb, kb = pair
        acc_ref[qb] = a_ref[...] * acc_ref[qb] + jnp.dot(vt_ref[kb], p_ref[...], preferred_element_type=F32)

    def pair_at(n):
        n = jnp.clip(n, 0, n_pairs - 1)
        return pairs_ref[0, n], pairs_ref[1, n]

    def tick(n, diagonal, s_now, p_now, a_now, s_next, p_prev, a_prev):
        form_logits(pair_at(n + 1), s_next)
        softmax_update(pair_at(n), n < n_pairs, diagonal, s_now, p_now, a_now)
        accumulate(pair_at(n - 1), p_prev, a_prev)

    def step(i, is_diagonal):
        for k in range(0, FOX_TICKS_PER_STEP, 2):
            n = FOX_TICKS_PER_STEP * i + k
            tick(n, is_diagonal(n), s0_ref, p0_ref, a0_ref, s1_ref, p1_ref, a1_ref)
            tick(n + 1, is_diagonal(n + 1), s1_ref, p1_ref, a1_ref, s0_ref, p0_ref, a0_ref)

    def diagonal_step(i, carry):
        step(i, lambda n: True)
        return carry

    def below_step(i, carry):
        step(i, lambda n: False)
        return carry

    n_steps = -(-(n_pairs + 2) // FOX_TICKS_PER_STEP)
    diagonal_steps = n_blocks // FOX_TICKS_PER_STEP
    form_logits(pair_at(0), s0_ref)
    lax.fori_loop(0, diagonal_steps, diagonal_step, 0)
    if n_blocks % FOX_TICKS_PER_STEP:
        step(diagonal_steps, lambda n: n < n_blocks)
        diagonal_steps += 1
    lax.fori_loop(diagonal_steps, n_steps, below_step, 0)

    for j in range(n_blocks):
        rows = slice(j * T, (j + 1) * T)
        acc = acc_ref[j]
        o = (acc[:HEAD_DIM, :] / acc[HEAD_DIM:HEAD_DIM + 1, :]).T
        o_ref[0, rows, :] = (o * _sigmoid(g_ref[0, rows, :].astype(F32))).astype(o_ref.dtype)


def _fox(proj, c, n_heads):
    bsz, seq, _ = proj.shape
    T = FOX_BLOCK
    n_blocks = seq // T
    c_rows = jnp.transpose(c[:, :, :n_heads], (0, 2, 1)).reshape(bsz, n_heads, n_blocks, T)
    causal_pairs = ([(qb, qb) for qb in range(n_blocks)]
                    + [(qb, kb) for qb in range(n_blocks) for kb in range(qb)])
    pairs = jnp.asarray(causal_pairs, jnp.int32).T

    def head_cols(group):
        return pl.BlockSpec((1, seq, HEAD_DIM), lambda b, h: (b, 0, group * n_heads + h))

    return pl.pallas_call(
        _fox_kernel,
        grid=(bsz, n_heads),
        in_specs=[pl.BlockSpec(memory_space=pltpu.SMEM),
                  head_cols(0), head_cols(1), head_cols(2), head_cols(3),
                  pl.BlockSpec((1, 1, n_blocks, T), lambda b, h: (b, h, 0, 0))],
        out_specs=pl.BlockSpec((1, seq, HEAD_DIM), lambda b, h: (b, 0, h)),
        out_shape=jax.ShapeDtypeStruct((bsz, seq, n_heads * HEAD_DIM), BF16),
        scratch_shapes=[pltpu.VMEM((seq, 2 * HEAD_DIM), BF16),
                        pltpu.VMEM((n_blocks, HEAD_DIM + FOX_ONES_ROWS, T), BF16),
                        pltpu.VMEM((n_blocks, 2 * HEAD_DIM, T), BF16),
                        pltpu.VMEM((n_blocks, 1, T), F32),
                        pltpu.VMEM((n_blocks, HEAD_DIM + FOX_ONES_ROWS, T), F32),
                        pltpu.VMEM((T, T), F32), pltpu.VMEM((T, T), F32),
                        pltpu.VMEM((T, T), BF16), pltpu.VMEM((T, T), BF16),
                        pltpu.VMEM((1, T), F32), pltpu.VMEM((1, T), F32)],
        compiler_params=_params("parallel", "parallel"),
        name="fox_attention",
    )(pairs, proj, proj, proj, proj, c_rows)


def _ffn_up_kernel(x_ref, wg_ref, wv_ref, cg_ref, cv_ref, o_ref, ug_ref, uv_ref, wgb_ref, wvb_ref,
                   *, seq_steps, d_ff):
    rows = x_ref.shape[0]

    @pl.when(pl.program_id(1) == 0)
    def _():
        valid = d_ff - pl.program_id(0) * wg_ref.shape[1]
        col = lax.broadcasted_iota(jnp.int32, wg_ref.shape, 1)
        wgb_ref[...] = jnp.where(col < valid, wg_ref[...], 0.0).astype(BF16)
        wvb_ref[...] = jnp.where(col < valid, wv_ref[...], 0.0).astype(BF16)

    @pl.when(pl.program_id(1) % seq_steps == 0)
    def _():
        ug_ref[0:SUBLANES, :] = jnp.zeros((SUBLANES, ug_ref.shape[1]), F32)
        uv_ref[0:SUBLANES, :] = jnp.zeros((SUBLANES, uv_ref.shape[1]), F32)

    x = x_ref[...]

    def conv(wb_ref, c_ref, u_ref):
        u_ref[SUBLANES:SUBLANES + rows, :] = jnp.dot(x, wb_ref[...], preferred_element_type=F32)
        out = c_ref[CONV_WIDTH - 1:CONV_WIDTH, :] * u_ref[SUBLANES:SUBLANES + rows, :]
        for tap in range(CONV_WIDTH - 1):
            back = CONV_WIDTH - 1 - tap
            out = out + c_ref[tap:tap + 1, :] * u_ref[SUBLANES - back:SUBLANES - back + rows, :]
        carry = u_ref[rows:rows + SUBLANES, :]
        u_ref[0:SUBLANES, :] = carry
        return out

    gate = conv(wgb_ref, cg_ref, ug_ref)
    val = conv(wvb_ref, cv_ref, uv_ref)
    o_ref[...] = (_silu(gate) * val).astype(o_ref.dtype)


def _ffn_up(h, w_up, layer, conv_w, seq, d_ff, d_ff_pad):
    t, d = h.shape
    rows = min(MM_ROWS, seq)
    assert seq % rows == 0 and d_ff_pad % FFN_COLS == 0
    n_cols = d_ff_pad // FFN_COLS
    return pl.pallas_call(
        functools.partial(_ffn_up_kernel, seq_steps=seq // rows, d_ff=d_ff),
        grid=(n_cols, t // rows),
        in_specs=[pl.BlockSpec((rows, d), lambda j, i: (i, 0)),
                  pl.BlockSpec((None, d, FFN_COLS), lambda j, i: (layer, 0, j)),
                  pl.BlockSpec((None, pl.Element(d), pl.Element(FFN_COLS, (0, d_ff_pad - d_ff))),
                               lambda j, i: (layer, 0, pl.multiple_of(d_ff + j * FFN_COLS, LANES))),
                  pl.BlockSpec((CONV_WIDTH, FFN_COLS), lambda j, i: (0, j)),
                  pl.BlockSpec((CONV_WIDTH, FFN_COLS), lambda j, i: (0, j + n_cols))],
        out_specs=pl.BlockSpec((rows, FFN_COLS), lambda j, i: (i, j)),
        out_shape=jax.ShapeDtypeStruct((t, d_ff), BF16),
        scratch_shapes=[pltpu.VMEM((rows + SUBLANES, FFN_COLS), F32),
                        pltpu.VMEM((rows + SUBLANES, FFN_COLS), F32),
                        pltpu.VMEM((d, FFN_COLS), BF16),
                        pltpu.VMEM((d, FFN_COLS), BF16)],
        compiler_params=_params("parallel", "arbitrary"),
        name="ffn_up_conv",
    )(h, w_up, w_up, conv_w, conv_w)


def _pad_cols(w, width):
    return jnp.pad(w, ((0, 0), (0, width - w.shape[1])))


def kernel(x, attn_norm_w, ffn_norm_w, final_norm_w, even_w_in, hgrn_lb_logits, hgrn_norm_w, ret_norm_w,
           even_w_out, odd_w_in, odd_b_f, odd_w_out, ffn_w_up, ffn_conv_w, ffn_w_down):
    bsz, seq, d_model = x.shape
    depth = attn_norm_w.shape[0]
    tokens = bsz * seq
    d_hgrn = hgrn_norm_w.shape[1]
    d_fox = odd_w_out.shape[1]
    n_fox = odd_b_f.shape[1]
    d_ff = ffn_w_down.shape[1]
    d_ff_pad = FFN_COLS * math.ceil(d_ff / FFN_COLS)

    xf = x.reshape(tokens, d_model)
    h = _rmsnorm(xf, attn_norm_w[0], BF16)
    for layer in range(depth):
        j = layer // 2
        if layer % 2 == 0:
            proj = _matmul(h, even_w_in, j, F32, even_w_in.shape[2], rows=MM_ROWS, cols=IN_COLS,
                           name="even_in").reshape(bsz, seq, -1)
            o_a = _hgrn(proj, hgrn_lb_logits, hgrn_norm_w[j], j)
            o_b = _retention(proj, ret_norm_w[j], first_group=4 * d_hgrn // ret_norm_w.shape[1])
            mixed = [o_a.reshape(tokens, -1), o_b.reshape(tokens, -1)]
            w_out = even_w_out[j]
        else:
            proj = _matmul(h, odd_w_in, j, BF16, 4 * d_fox, rows=MM_ROWS, cols=IN_COLS,
                           name="fox_in").reshape(bsz, seq, -1)
            c = _forget_scan(h.reshape(bsz, seq, d_model), odd_w_in, j, 4 * d_fox,
                             _pad_cols(odd_b_f[j][None, :], LANES))
            mixed = [_fox(proj, c, n_fox).reshape(tokens, -1)]
            w_out = odd_w_out[j]
        xf, h = _matmul_residual_norm(mixed, w_out.astype(BF16), xf, ffn_norm_w[layer], BF16, rows=OUT_ROWS,
                                      name="mixer_out_norm")
        conv_w = ffn_conv_w[layer]
        conv_w = jnp.concatenate([_pad_cols(conv_w[:, :d_ff], d_ff_pad), _pad_cols(conv_w[:, d_ff:], d_ff_pad)],
                                 axis=1)
        act = _ffn_up(h, ffn_w_up, layer, conv_w, seq, d_ff, d_ff_pad)
        w_down = ffn_w_down[layer].astype(BF16)
        last = layer == depth - 1
        xf, h = _matmul_residual_norm([act], w_down, xf, final_norm_w if last else attn_norm_w[layer + 1],
                                      x.dtype if last else BF16, rows=DOWN_ROWS, keep_sum=not last,
                                      name="ffn_down_norm")
    return h.reshape(bsz, seq, d_model)
```

```python
import functools
import math

import jax
import jax.numpy as jnp
from jax import lax
from jax.experimental import pallas as pl
from jax.experimental.pallas import tpu as pltpu

F32 = jnp.float32
BF16 = jnp.bfloat16

HEAD_DIM = 128
ROPE_BASE = 10000.0
RET_DECAY_BASE = 5.0
EPS = 1e-6
LB_FLOOR = 1e-30
MASK_VALUE = -1e30
CONV_WIDTH = 3
LOG2_E = math.log2(math.e)

LANES = 128
SUBLANES = 8
VMEM_LIMIT_BYTES = 48 * 1024 * 1024

NORM_ROWS = 256
MM_ROWS = 1024
IN_COLS = 1024
OUT_ROWS = 512
DOWN_ROWS = 256
FFN_COLS = 512
MIX_ROWS = 512
HGRN_UNROLL = 4
RET_UNROLL = 2
HGRN_CHUNK = 64
HGRN_SUB = 16
HGRN_HEADS_PER_STEP = 8
RET_CHUNK = 128
RET_HEADS_PER_STEP = 4
FOX_BLOCK = 512
FOX_TICKS_PER_STEP = 20
FOX_ONES_ROWS = 16
SCAN_ROWS = 256

NT_DIMS = (((1,), (1,)), ((), ()))
TN_DIMS = (((0,), (0,)), ((), ()))


def _params(*semantics):
    return pltpu.CompilerParams(dimension_semantics=semantics, vmem_limit_bytes=VMEM_LIMIT_BYTES)


def _sigmoid(x):
    return 1.0 / (1.0 + jnp.exp(-x))


def _silu(x):
    return x * _sigmoid(x)


def _log_sigmoid(x):
    return jnp.minimum(x, 0.0) - jnp.log1p(jnp.exp(-jnp.abs(x)))


def _lower_tri(n):
    rows = lax.broadcasted_iota(jnp.int32, (n, n), 0)
    cols = lax.broadcasted_iota(jnp.int32, (n, n), 1)
    return (rows >= cols).astype(F32)


def _rmsnorm_kernel(x_ref, w_ref, o_ref):
    x = x_ref[...]
    ms = jnp.mean(x * x, axis=-1, keepdims=True)
    o_ref[...] = (x * lax.rsqrt(ms + EPS) * w_ref[...]).astype(o_ref.dtype)


def _rmsnorm(x, w, out_dtype):
    t, d = x.shape
    return pl.pallas_call(
        _rmsnorm_kernel,
        grid=(t // NORM_ROWS,),
        in_specs=[pl.BlockSpec((NORM_ROWS, d), lambda i: (i, 0)),
                  pl.BlockSpec((1, d), lambda i: (0, 0))],
        out_specs=pl.BlockSpec((NORM_ROWS, d), lambda i: (i, 0)),
        out_shape=jax.ShapeDtypeStruct((t, d), out_dtype),
        compiler_params=_params("parallel"),
        name="rmsnorm",
    )(x, w.reshape(1, d))


def _cast_weight_tile(w_ref, wb_ref, valid_cols):
    w = w_ref[...]
    if valid_cols < w.shape[1]:
        col = lax.broadcasted_iota(jnp.int32, w.shape, 1)
        w = jnp.where(col < valid_cols, w, 0.0)
    wb_ref[...] = w.astype(BF16)


def _matmul_kernel(x_ref, w_ref, o_ref, wb_ref):
    @pl.when(pl.program_id(1) == 0)
    def _():
        _cast_weight_tile(w_ref, wb_ref, w_ref.shape[1])

    o_ref[...] = jnp.dot(x_ref[...], wb_ref[...], preferred_element_type=F32).astype(o_ref.dtype)


def _matmul(x, w, layer, out_dtype, n_out, rows, cols, name):
    t, k = x.shape
    rows = min(rows, t)
    assert t % rows == 0 and n_out % cols == 0 and n_out <= w.shape[2]
    return pl.pallas_call(
        _matmul_kernel,
        grid=(n_out // cols, t // rows),
        in_specs=[pl.BlockSpec((rows, k), lambda j, i: (i, 0)),
                  pl.BlockSpec((None, k, cols), lambda j, i: (layer, 0, j))],
        out_specs=pl.BlockSpec((rows, cols), lambda j, i: (i, j)),
        out_shape=jax.ShapeDtypeStruct((t, n_out), out_dtype),
        scratch_shapes=[pltpu.VMEM((k, cols), BF16)],
        compiler_params=_params("parallel", "arbitrary"),
        name=name,
    )(x, w)


def _residual_norm_kernel(*refs, n_parts, keep_sum):
    x_refs, (w_ref, r_ref, nw_ref), out_refs = refs[:n_parts], refs[n_parts:n_parts + 3], refs[n_parts + 3:]
    y = r_ref[...]
    k0 = 0
    for x_ref in x_refs:
        k1 = k0 + x_ref.shape[1]
        y = y + jnp.dot(x_ref[...], w_ref[k0:k1, :], preferred_element_type=F32)
        k0 = k1
    if keep_sum:
        out_refs[0][...] = y
    h_ref = out_refs[-1]
    ms = jnp.mean(y * y, axis=-1, keepdims=True)
    h_ref[...] = (y * lax.rsqrt(ms + EPS) * nw_ref[...]).astype(h_ref.dtype)


def _matmul_residual_norm(xs, w, residual, norm_w, h_dtype, rows, keep_sum=True, name="matmul_residual_norm"):
    t = xs[0].shape[0]
    k, n = w.shape
    assert t % rows == 0 and sum(x.shape[1] for x in xs) == k
    row_block = pl.BlockSpec((rows, n), lambda i: (i, 0))
    out_shape = [jax.ShapeDtypeStruct((t, n), h_dtype)]
    if keep_sum:
        out_shape.insert(0, jax.ShapeDtypeStruct((t, n), F32))
    outs = pl.pallas_call(
        functools.partial(_residual_norm_kernel, n_parts=len(xs), keep_sum=keep_sum),
        grid=(t // rows,),
        in_specs=[pl.BlockSpec((rows, x.shape[1]), lambda i: (i, 0)) for x in xs]
        + [pl.BlockSpec((k, n), lambda i: (0, 0), pipeline_mode=pl.Buffered(1)),
           row_block,
           pl.BlockSpec((1, n), lambda i: (0, 0))],
        out_specs=[row_block] * len(out_shape),
        out_shape=out_shape,
        compiler_params=_params("parallel"),
        name=name,
    )(*xs, w, residual, norm_w.reshape(1, n))
    return (outs[0], outs[1]) if keep_sum else (None, outs[0])


def _hgrn_kernel(q_ref, f_ref, i_ref, g_ref, lbl_ref, nw_ref, tri3_ref, sel_ref, mask_ref, o_ref, state_ref,
                 *, layer_j):
    C, L = HGRN_CHUNK, HGRN_SUB
    n_sub = C // L

    @pl.when(pl.program_id(2) == 0)
    def _():
        state_ref[...] = jnp.zeros_like(state_ref)

    lbl = lbl_ref[...]
    e = jnp.exp(lbl - jnp.max(lbl, axis=0, keepdims=True))
    sm = e / jnp.sum(e, axis=0, keepdims=True)
    lb_all = jnp.zeros((1, lbl.shape[1]), F32)
    for r in range(1, layer_j + 1):
        lb_all = lb_all + sm[r:r + 1, :]

    def pair_products(q, k, b, s):
        first = (s // SUBLANES) * SUBLANES
        q3, k3, b3 = (x.reshape(n_sub, L, HEAD_DIM) for x in (q, k, b))
        decay = jnp.exp2(jnp.minimum(b3[:, first:, :] - b3[:, s:s + 1, :], 0.0))
        prod = q3[:, first:, :] * k3[:, s:s + 1, :] * decay
        if first:
            prod = jnp.concatenate([jnp.zeros((n_sub, first, HEAD_DIM), F32), prod], axis=1)
        return prod.reshape(C, HEAD_DIM).astype(BF16)

    def head_chunk(rows, lanes, head):
        lb_floor, one_m_lb = jnp.maximum(lb_all[:, lanes], LB_FLOOR), 1.0 - lb_all[:, lanes]
        z = f_ref[0, rows, lanes]
        q = _silu(q_ref[0, rows, lanes])
        v = i_ref[0, rows, lanes].astype(BF16)

        e_z = jnp.exp(-jnp.abs(z))
        inv = 1.0 / (1.0 + e_z)
        log_f = jnp.log(lb_floor + one_m_lb * (jnp.where(z >= 0.0, 1.0, e_z) * inv))
        k = one_m_lb * (jnp.where(z >= 0.0, e_z, 1.0) * inv)
        hi = log_f.astype(BF16)
        rest = log_f - hi.astype(F32)
        mid = rest.astype(BF16)
        low = (rest - mid.astype(F32)).astype(BF16)
        b = jnp.dot(tri3_ref[...], jnp.concatenate([hi, mid, low], axis=0),
                    preferred_element_type=F32) * LOG2_E
        b_last = b[C - 1:C, :]

        state = state_ref[head]
        o = lax.dot_general((q * jnp.exp2(b)).astype(BF16), state.astype(BF16), NT_DIMS,
                            preferred_element_type=F32)

        below = [jnp.zeros((L, C), F32)]
        for i in range(1, n_sub):
            lo = i * L
            edge = b[lo - 1:lo, :]
            q_edge = (q[lo:lo + L, :] * jnp.exp2(b[lo:lo + L, :] - edge)).astype(BF16)
            k_edge = (k[:lo, :] * jnp.exp2(edge - b[:lo, :])).astype(BF16)
            k_edge = jnp.concatenate([k_edge, jnp.zeros((C - lo, HEAD_DIM), BF16)], axis=0)
            below.append(lax.dot_general(q_edge, k_edge, NT_DIMS, preferred_element_type=F32))
        pair = jnp.concatenate([pair_products(q, k, b, s) for s in range(L)], axis=1)
        diag = jnp.dot(pair, sel_ref[...], preferred_element_type=F32)
        scores = jnp.concatenate(below, axis=0) + diag * mask_ref[...]
        o = o + jnp.dot(scores.astype(BF16), v, preferred_element_type=F32)

        k_end = (k * jnp.exp2(b_last - b)).astype(BF16)
        state_ref[head] = state * jnp.exp2(b_last) + lax.dot_general(v, k_end, TN_DIMS,
                                                                     preferred_element_type=F32)

        y = o * lax.rsqrt(jnp.mean(o * o, axis=-1, keepdims=True) + EPS) * nw_ref[:, lanes]
        o_ref[0, rows, lanes] = (y * _silu(g_ref[0, rows, lanes])).astype(o_ref.dtype)

    def chunk(c, carry):
        rows = pl.ds(pl.multiple_of(c * C, C), C)
        for head in range(HGRN_HEADS_PER_STEP):
            head_chunk(rows, slice(head * HEAD_DIM, (head + 1) * HEAD_DIM), head)
        return carry

    lax.fori_loop(0, MIX_ROWS // C, chunk, 0, unroll=HGRN_UNROLL)


def _hgrn(proj, lb_logits, norm_w, layer_j):
    bsz, seq, _ = proj.shape
    d_h = norm_w.shape[0]
    width = HGRN_HEADS_PER_STEP * HEAD_DIM
    n_groups = d_h // width
    n_layers = lb_logits.shape[0]

    C, L = HGRN_CHUNK, HGRN_SUB
    pos = jnp.arange(C)
    causal = pos[:, None] >= pos[None, :]
    tri3 = jnp.tile(causal.astype(BF16), (1, 3))
    pair_sel = (jnp.repeat(jnp.arange(L), HEAD_DIM)[:, None] == (pos % L)[None, :]).astype(BF16)
    diag_mask = (causal & (pos[:, None] // L == pos[None, :] // L)).astype(F32)

    def col(group):
        return pl.BlockSpec((1, MIX_ROWS, width), lambda b, h, s: (b, s, group * n_groups + h))

    def whole(a):
        return pl.BlockSpec(a.shape, lambda b, h, s: (0, 0))

    return pl.pallas_call(
        functools.partial(_hgrn_kernel, layer_j=layer_j),
        grid=(bsz, n_groups, seq // MIX_ROWS),
        in_specs=[col(0), col(1), col(2), col(3),
                  pl.BlockSpec((n_layers, width), lambda b, h, s: (0, h)),
                  pl.BlockSpec((1, width), lambda b, h, s: (0, h)),
                  whole(tri3), whole(pair_sel), whole(diag_mask)],
        out_specs=pl.BlockSpec((1, MIX_ROWS, width), lambda b, h, s: (b, s, h)),
        out_shape=jax.ShapeDtypeStruct((bsz, seq, d_h), BF16),
        scratch_shapes=[pltpu.VMEM((HGRN_HEADS_PER_STEP, HEAD_DIM, HEAD_DIM), F32)],
        compiler_params=_params("parallel", "parallel", "arbitrary"),
        name="hgrn2",
    )(proj, proj, proj, proj, lb_logits, norm_w.reshape(1, d_h), tri3, pair_sel, diag_mask)


def _retention_kernel(q_ref, k_ref, v_ref, g_ref, cos_ref, sin_ref, intra_ref, edge_ref, nw_ref, o_ref, state_ref):
    C = RET_CHUNK

    @pl.when(pl.program_id(2) == 0)
    def _():
        state_ref[...] = jnp.zeros_like(state_ref)

    def rope(x, cos, sin_signed):
        return x * cos + pltpu.roll(x, HEAD_DIM // 2, 1) * sin_signed

    def head_chunk(sl, lanes, head):
        cos, sin_signed = cos_ref[sl, :], sin_ref[sl, :]
        q = rope(q_ref[0, sl, lanes], cos, sin_signed)
        k = rope(k_ref[0, sl, lanes], cos, sin_signed) * (HEAD_DIM ** -0.5)
        v = v_ref[0, sl, lanes].astype(BF16)
        state = state_ref[head]
        w_start, w_end, decay_chunk = edge_ref[head, 0], edge_ref[head, 1], edge_ref[head, 2, 0:1, :]

        scores = lax.dot_general(q.astype(BF16), k.astype(BF16), NT_DIMS,
                                 preferred_element_type=F32) * intra_ref[head]
        o = jnp.dot(scores.astype(BF16), v, preferred_element_type=F32)
        o = o + lax.dot_general((q * w_start).astype(BF16), state.astype(BF16), NT_DIMS,
                                preferred_element_type=F32)
        state_ref[head] = decay_chunk * state + lax.dot_general(v, (k * w_end).astype(BF16), TN_DIMS,
                                                                preferred_element_type=F32)

        cen = o - jnp.mean(o, axis=-1, keepdims=True)
        y = cen * lax.rsqrt(jnp.mean(cen * cen, axis=-1, keepdims=True) + EPS) * nw_ref[:, lanes]
        o_ref[0, sl, lanes] = (y * _silu(g_ref[0, sl, lanes])).astype(o_ref.dtype)

    def chunk(c, carry):
        sl = pl.ds(pl.multiple_of(c * C, C), C)
        for head in range(RET_HEADS_PER_STEP):
            head_chunk(sl, slice(head * HEAD_DIM, (head + 1) * HEAD_DIM), head)
        return carry

    lax.fori_loop(0, MIX_ROWS // C, chunk, 0, unroll=RET_UNROLL)


def _retention(proj, norm_w, first_group):
    bsz, seq, _ = proj.shape
    d_r = norm_w.shape[0]
    n_heads = d_r // HEAD_DIM
    half = HEAD_DIM // 2

    inv = ROPE_BASE ** (-jnp.arange(0, HEAD_DIM, 2, dtype=F32) / HEAD_DIM)
    ang = jnp.arange(seq).astype(F32)[:, None] * inv[None, :]
    cos = jnp.concatenate([jnp.cos(ang), jnp.cos(ang)], axis=-1)
    sin_signed = jnp.concatenate([-jnp.sin(ang), jnp.sin(ang)], axis=-1)
    assert cos.shape == (seq, 2 * half)
    C = RET_CHUNK
    log_gamma = jnp.log1p(-jnp.exp2(-RET_DECAY_BASE - jnp.arange(n_heads, dtype=F32)))
    pos = jnp.arange(C, dtype=F32)
    rel = pos[:, None] - pos[None, :]
    intra = jnp.where(rel >= 0, jnp.exp(log_gamma[:, None, None] * jnp.maximum(rel, 0.0)), 0.0)
    w_start = jnp.exp(log_gamma[:, None] * (pos + 1.0)[None, :])
    w_end = jnp.exp(log_gamma[:, None] * (C - 1.0 - pos)[None, :])
    decay_chunk = jnp.broadcast_to(jnp.exp(log_gamma * C)[:, None], (n_heads, C))
    edge = jnp.broadcast_to(jnp.stack([w_start, w_end, decay_chunk], axis=1)[..., None],
                            (n_heads, 3, C, HEAD_DIM))

    per_step = RET_HEADS_PER_STEP
    width = per_step * HEAD_DIM
    n_groups = n_heads // per_step
    first = first_group * n_groups

    def col(group):
        return pl.BlockSpec((1, MIX_ROWS, width), lambda b, h, s: (b, s, first + group * n_groups + h))

    table = pl.BlockSpec((MIX_ROWS, HEAD_DIM), lambda b, h, s: (s, 0))
    return pl.pallas_call(
        _retention_kernel,
        grid=(bsz, n_groups, seq // MIX_ROWS),
        in_specs=[col(0), col(1), col(2), col(3), table, table,
                  pl.BlockSpec((per_step, C, C), lambda b, h, s: (h, 0, 0)),
                  pl.BlockSpec((per_step, 3, C, HEAD_DIM), lambda b, h, s: (h, 0, 0, 0)),
                  pl.BlockSpec((1, width), lambda b, h, s: (0, h))],
        out_specs=pl.BlockSpec((1, MIX_ROWS, width), lambda b, h, s: (b, s, h)),
        out_shape=jax.ShapeDtypeStruct((bsz, seq, d_r), BF16),
        scratch_shapes=[pltpu.VMEM((per_step, HEAD_DIM, HEAD_DIM), F32)],
        compiler_params=_params("parallel", "parallel", "arbitrary"),
        name="retention",
    )(proj, proj, proj, proj, cos, sin_signed, intra, edge, norm_w.reshape(1, d_r))


def _forget_scan_kernel(h_ref, w_ref, bias_ref, c_ref, wb_ref, carry_ref, *, valid_cols):
    @pl.when(pl.program_id(1) == 0)
    def _():
        _cast_weight_tile(w_ref, wb_ref, valid_cols)
        carry_ref[...] = jnp.zeros_like(carry_ref)

    f = jnp.dot(h_ref[0], wb_ref[...], preferred_element_type=F32)
    log_f = _log_sigmoid(f + bias_ref[...])
    c = jnp.dot(_lower_tri(SCAN_ROWS), log_f, precision=lax.Precision.HIGHEST,
                preferred_element_type=F32) + carry_ref[...]
    c_ref[0] = c
    carry_ref[...] = c[SCAN_ROWS - 1:SCAN_ROWS, :]


def _forget_scan(h, w_in, layer, first_col, bias):
    bsz, seq, d = h.shape
    assert first_col % LANES == 0 and w_in.shape[2] - first_col <= LANES
    return pl.pallas_call(
        functools.partial(_forget_scan_kernel, valid_cols=w_in.shape[2] - first_col),
        grid=(bsz, seq // SCAN_ROWS),
        in_specs=[pl.BlockSpec((1, SCAN_ROWS, d), lambda b, s: (b, s, 0)),
                  pl.BlockSpec((None, d, LANES), lambda b, s: (layer, 0, first_col // LANES)),
                  pl.BlockSpec((1, LANES), lambda b, s: (0, 0))],
        out_specs=pl.BlockSpec((1, SCAN_ROWS, LANES), lambda b, s: (b, s, 0)),
        out_shape=jax.ShapeDtypeStruct((bsz, seq, LANES), F32),
        scratch_shapes=[pltpu.VMEM((d, LANES), BF16), pltpu.VMEM((1, LANES), F32)],
        compiler_params=_params("parallel", "arbitrary"),
        name="forget_scan",
    )(h, w_in, bias)


def _split3(x):
    hi = x.astype(BF16).astype(F32)
    rest = x - hi
    mid = rest.astype(BF16).astype(F32)
    low = (rest - mid).astype(BF16).astype(F32)
    return hi, mid, low


def _fox_kernel(pairs_ref, q_ref, k_ref, v_ref, g_ref, ct_ref, o_ref,
                kaug_ref, vt_ref, qaug_ref, m_ref, acc_ref, s0_ref, s1_ref, p0_ref, p1_ref, a0_ref, a1_ref):
    T = FOX_BLOCK
    n_blocks = vt_ref.shape[0]
    n_pairs = n_blocks * (n_blocks + 1) // 2
    aug_row = lax.broadcasted_iota(jnp.int32, (HEAD_DIM, T), 0)

    def bias_rows(c_row, for_queries):
        hi, mid, low = _split3(c_row)
        if for_queries:
            pieces = jnp.where(aug_row == 3, hi, jnp.where(aug_row == 4, mid, jnp.where(aug_row == 5, low, 0.0)))
            return jnp.where(aug_row < 3, 1.0, pieces)
        pieces = jnp.where(aug_row == 0, -hi, jnp.where(aug_row == 1, -mid, jnp.where(aug_row == 2, -low, 0.0)))
        return jnp.where((aug_row >= 3) & (aug_row < 6), 1.0, pieces)

    for j in range(n_blocks):
        rows = slice(j * T, (j + 1) * T)
        c_j = ct_ref[0, 0, j:j + 1, :] * LOG2_E
        kaug_ref[rows, 0:HEAD_DIM] = k_ref[0, rows, :]
        kaug_ref[rows, HEAD_DIM:2 * HEAD_DIM] = bias_rows(c_j, False).T.astype(BF16)
        vt_ref[j, 0:HEAD_DIM, :] = v_ref[0, rows, :].astype(F32).T.astype(BF16)
        vt_ref[j, HEAD_DIM:, :] = jnp.ones((FOX_ONES_ROWS, T), BF16)
        qaug_ref[j, 0:HEAD_DIM, :] = (q_ref[0, rows, :].astype(F32) * (HEAD_DIM ** -0.5 * LOG2_E)).T.astype(BF16)
        qaug_ref[j, HEAD_DIM:2 * HEAD_DIM, :] = bias_rows(c_j, True).astype(BF16)
    m_ref[...] = jnp.full(m_ref.shape, MASK_VALUE, F32)
    acc_ref[...] = jnp.zeros(acc_ref.shape, F32)
    for p_ref, a_ref in ((p0_ref, a0_ref), (p1_ref, a1_ref)):
        p_ref[...] = jnp.zeros((T, T), BF16)
        a_ref[...] = jnp.ones((1, T), F32)

    key_id = lax.broadcasted_iota(jnp.int32, (T, T), 0)
    query_id = lax.broadcasted_iota(jnp.int32, (T, T), 1)

    def form_logits(pair, s_ref):
        qb, kb = pair
        keys = pl.ds(pl.multiple_of(kb * T, T), T)
        s_ref[...] = jnp.dot(kaug_ref[keys, :], qaug_ref[qb], preferred_element_type=F32)

    def softmax_update(pair, valid, diagonal, s_ref, p_ref, a_ref):
        qb, _ = pair
        s = s_ref[...]
        if diagonal:
            s = jnp.where(key_id <= query_id, s, MASK_VALUE)
        m = m_ref[qb]
        m_new = jnp.maximum(m, jnp.where(valid, jnp.max(s, axis=0, keepdims=True), MASK_VALUE))
        p_ref[...] = jnp.exp2(s - jnp.where(valid, m_new, -MASK_VALUE)).astype(BF16)
        a_ref[...] = jnp.exp2(m - m_new)
        m_ref[qb] = m_new

    def accumulate(pair, p_ref, a_ref):
        qb, kb = pair
        acc_ref[qb] = a_ref[...] * acc_ref[qb] + jnp.dot(vt_ref[kb], p_ref[...], preferred_element_type=F32)

    def pair_at(n):
        n = jnp.clip(n, 0, n_pairs - 1)
        return pairs_ref[0, n], pairs_ref[1, n]

    def tick(n, diagonal, s_now, p_now, a_now, s_next, p_prev, a_prev):
        form_logits(pair_at(n + 1), s_next)
        softmax_update(pair_at(n), n < n_pairs, diagonal, s_now, p_now, a_now)
        accumulate(pair_at(n - 1), p_prev, a_prev)

    def step(i, is_diagonal):
        for k in range(0, FOX_TICKS_PER_STEP, 2):
            n = FOX_TICKS_PER_STEP * i + k
            tick(n, is_diagonal(n), s0_ref, p0_ref, a0_ref, s1_ref, p1_ref, a1_ref)
            tick(n + 1, is_diagonal(n + 1), s1_ref, p1_ref, a1_ref, s0_ref, p0_ref, a0_ref)

    def diagonal_step(i, carry):
        step(i, lambda n: True)
        return carry

    def below_step(i, carry):
        step(i, lambda n: False)
        return carry

    n_steps = -(-(n_pairs + 2) // FOX_TICKS_PER_STEP)
    diagonal_steps = n_blocks // FOX_TICKS_PER_STEP
    form_logits(pair_at(0), s0_ref)
    lax.fori_loop(0, diagonal_steps, diagonal_step, 0)
    if n_blocks % FOX_TICKS_PER_STEP:
        step(diagonal_steps, lambda n: n < n_blocks)
        diagonal_steps += 1
    lax.fori_loop(diagonal_steps, n_steps, below_step, 0)

    for j in range(n_blocks):
        rows = slice(j * T, (j + 1) * T)
        acc = acc_ref[j]
        o = (acc[:HEAD_DIM, :] / acc[HEAD_DIM:HEAD_DIM + 1, :]).T
        o_ref[0, rows, :] = (o * _sigmoid(g_ref[0, rows, :].astype(F32))).astype(o_ref.dtype)


def _fox(proj, c, n_heads):
    bsz, seq, _ = proj.shape
    T = FOX_BLOCK
    n_blocks = seq // T
    c_rows = jnp.transpose(c[:, :, :n_heads], (0, 2, 1)).reshape(bsz, n_heads, n_blocks, T)
    causal_pairs = ([(qb, qb) for qb in range(n_blocks)]
                    + [(qb, kb) for qb in range(n_blocks) for kb in range(qb)])
    pairs = jnp.asarray(causal_pairs, jnp.int32).T

    def head_cols(group):
        return pl.BlockSpec((1, seq, HEAD_DIM), lambda b, h: (b, 0, group * n_heads + h))

    return pl.pallas_call(
        _fox_kernel,
        grid=(bsz, n_heads),
        in_specs=[pl.BlockSpec(memory_space=pltpu.SMEM),
                  head_cols(0), head_cols(1), head_cols(2), head_cols(3),
                  pl.BlockSpec((1, 1, n_blocks, T), lambda b, h: (b, h, 0, 0))],
        out_specs=pl.BlockSpec((1, seq, HEAD_DIM), lambda b, h: (b, 0, h)),
        out_shape=jax.ShapeDtypeStruct((bsz, seq, n_heads * HEAD_DIM), BF16),
        scratch_shapes=[pltpu.VMEM((seq, 2 * HEAD_DIM), BF16),
                        pltpu.VMEM((n_blocks, HEAD_DIM + FOX_ONES_ROWS, T), BF16),
                        pltpu.VMEM((n_blocks, 2 * HEAD_DIM, T), BF16),
                        pltpu.VMEM((n_blocks, 1, T), F32),
                        pltpu.VMEM((n_blocks, HEAD_DIM + FOX_ONES_ROWS, T), F32),
                        pltpu.VMEM((T, T), F32), pltpu.VMEM((T, T), F32),
                        pltpu.VMEM((T, T), BF16), pltpu.VMEM((T, T), BF16),
                        pltpu.VMEM((1, T), F32), pltpu.VMEM((1, T), F32)],
        compiler_params=_params("parallel", "parallel"),
        name="fox_attention",
    )(pairs, proj, proj, proj, proj, c_rows)


def _ffn_up_kernel(x_ref, wg_ref, wv_ref, cg_ref, cv_ref, o_ref, ug_ref, uv_ref, wgb_ref, wvb_ref,
                   *, seq_steps, d_ff):
    rows = x_ref.shape[0]

    @pl.when(pl.program_id(1) == 0)
    def _():
        valid = d_ff - pl.program_id(0) * wg_ref.shape[1]
        col = lax.broadcasted_iota(jnp.int32, wg_ref.shape, 1)
        wgb_ref[...] = jnp.where(col < valid, wg_ref[...], 0.0).astype(BF16)
        wvb_ref[...] = jnp.where(col < valid, wv_ref[...], 0.0).astype(BF16)

    @pl.when(pl.program_id(1) % seq_steps == 0)
    def _():
        ug_ref[0:SUBLANES, :] = jnp.zeros((SUBLANES, ug_ref.shape[1]), F32)
        uv_ref[0:SUBLANES, :] = jnp.zeros((SUBLANES, uv_ref.shape[1]), F32)

    x = x_ref[...]

    def conv(wb_ref, c_ref, u_ref):
        u_ref[SUBLANES:SUBLANES + rows, :] = jnp.dot(x, wb_ref[...], preferred_element_type=F32)
        out = c_ref[CONV_WIDTH - 1:CONV_WIDTH, :] * u_ref[SUBLANES:SUBLANES + rows, :]
        for tap in range(CONV_WIDTH - 1):
            back = CONV_WIDTH - 1 - tap
            out = out + c_ref[tap:tap + 1, :] * u_ref[SUBLANES - back:SUBLANES - back + rows, :]
        carry = u_ref[rows:rows + SUBLANES, :]
        u_ref[0:SUBLANES, :] = carry
        return out

    gate = conv(wgb_ref, cg_ref, ug_ref)
    val = conv(wvb_ref, cv_ref, uv_ref)
    o_ref[...] = (_silu(gate) * val).astype(o_ref.dtype)


def _ffn_up(h, w_up, layer, conv_w, seq, d_ff, d_ff_pad):
    t, d = h.shape
    rows = min(MM_ROWS, seq)
    assert seq % rows == 0 and d_ff_pad % FFN_COLS == 0
    n_cols = d_ff_pad // FFN_COLS
    return pl.pallas_call(
        functools.partial(_ffn_up_kernel, seq_steps=seq // rows, d_ff=d_ff),
        grid=(n_cols, t // rows),
        in_specs=[pl.BlockSpec((rows, d), lambda j, i: (i, 0)),
                  pl.BlockSpec((None, d, FFN_COLS), lambda j, i: (layer, 0, j)),
                  pl.BlockSpec((None, pl.Element(d), pl.Element(FFN_COLS, (0, d_ff_pad - d_ff))),
                               lambda j, i: (layer, 0, pl.multiple_of(d_ff + j * FFN_COLS, LANES))),
                  pl.BlockSpec((CONV_WIDTH, FFN_COLS), lambda j, i: (0, j)),
                  pl.BlockSpec((CONV_WIDTH, FFN_COLS), lambda j, i: (0, j + n_cols))],
        out_specs=pl.BlockSpec((rows, FFN_COLS), lambda j, i: (i, j)),
        out_shape=jax.ShapeDtypeStruct((t, d_ff), BF16),
        scratch_shapes=[pltpu.VMEM((rows + SUBLANES, FFN_COLS), F32),
                        pltpu.VMEM((rows + SUBLANES, FFN_COLS), F32),
                        pltpu.VMEM((d, FFN_COLS), BF16),
                        pltpu.VMEM((d, FFN_COLS), BF16)],
        compiler_params=_params("parallel", "arbitrary"),
        name="ffn_up_conv",
    )(h, w_up, w_up, conv_w, conv_w)


def _pad_cols(w, width):
    return jnp.pad(w, ((0, 0), (0, width - w.shape[1])))


def kernel(x, attn_norm_w, ffn_norm_w, final_norm_w, even_w_in, hgrn_lb_logits, hgrn_norm_w, ret_norm_w,
           even_w_out, odd_w_in, odd_b_f, odd_w_out, ffn_w_up, ffn_conv_w, ffn_w_down):
    bsz, seq, d_model = x.shape
    depth = attn_norm_w.shape[0]
    tokens = bsz * seq
    d_hgrn = hgrn_norm_w.shape[1]
    d_fox = odd_w_out.shape[1]
    n_fox = odd_b_f.shape[1]
    d_ff = ffn_w_down.shape[1]
    d_ff_pad = FFN_COLS * math.ceil(d_ff / FFN_COLS)

    xf = x.reshape(tokens, d_model)
    h = _rmsnorm(xf, attn_norm_w[0], BF16)
    for layer in range(depth):
        j = layer // 2
        if layer % 2 == 0:
            proj = _matmul(h, even_w_in, j, F32, even_w_in.shape[2], rows=MM_ROWS, cols=IN_COLS,
                           name="even_in").reshape(bsz, seq, -1)
            o_a = _hgrn(proj, hgrn_lb_logits, hgrn_norm_w[j], j)
            o_b = _retention(proj, ret_norm_w[j], first_group=4 * d_hgrn // ret_norm_w.shape[1])
            mixed = [o_a.reshape(tokens, -1), o_b.reshape(tokens, -1)]
            w_out = even_w_out[j]
        else:
            proj = _matmul(h, odd_w_in, j, BF16, 4 * d_fox, rows=MM_ROWS, cols=IN_COLS,
                           name="fox_in").reshape(bsz, seq, -1)
            c = _forget_scan(h.reshape(bsz, seq, d_model), odd_w_in, j, 4 * d_fox,
                             _pad_cols(odd_b_f[j][None, :], LANES))
            mixed = [_fox(proj, c, n_fox).reshape(tokens, -1)]
            w_out = odd_w_out[j]
        xf, h = _matmul_residual_norm(mixed, w_out.astype(BF16), xf, ffn_norm_w[layer], BF16, rows=OUT_ROWS,
                                      name="mixer_out_norm")
        conv_w = ffn_conv_w[layer]
        conv_w = jnp.concatenate([_pad_cols(conv_w[:, :d_ff], d_ff_pad), _pad_cols(conv_w[:, d_ff:], d_ff_pad)],
                                 axis=1)
        act = _ffn_up(h, ffn_w_up, layer, conv_w, seq, d_ff, d_ff_pad)
        w_down = ffn_w_down[layer].astype(BF16)
        last = layer == depth - 1
        xf, h = _matmul_residual_norm([act], w_down, xf, final_norm_w if last else attn_norm_w[layer + 1],
                                      x.dtype if last else BF16, rows=DOWN_ROWS, keep_sum=not last,
                                      name="ffn_down_norm")
    return h.reshape(bsz, seq, d_model)
```

```python
import functools
import math

import jax
import jax.numpy as jnp
from jax import lax
from jax.experimental import pallas as pl
from jax.experimental.pallas import tpu as pltpu

F32 = jnp.float32
BF16 = jnp.bfloat16

HEAD_DIM = 128
ROPE_BASE = 10000.0
RET_DECAY_BASE = 5.0
EPS = 1e-6
LB_FLOOR = 1e-30
MASK_VALUE = -1e30
CONV_WIDTH = 3
LOG2_E = math.log2(math.e)

LANES = 128
SUBLANES = 8
VMEM_LIMIT_BYTES = 48 * 1024 * 1024

NORM_ROWS = 256
MM_ROWS = 1024
IN_COLS = 1024
OUT_ROWS = 512
DOWN_ROWS = 256
FFN_COLS = 512
MIX_ROWS = 512
HGRN_UNROLL = 4
RET_UNROLL = 2
HGRN_CHUNK = 64
HGRN_SUB = 16
HGRN_HEADS_PER_STEP = 8
RET_CHUNK = 128
RET_HEADS_PER_STEP = 4
FOX_BLOCK = 512
FOX_ONES_ROWS = 16
SCAN_ROWS = 256

NT_DIMS = (((1,), (1,)), ((), ()))
TN_DIMS = (((0,), (0,)), ((), ()))


def _params(*semantics):
    return pltpu.CompilerParams(dimension_semantics=semantics, vmem_limit_bytes=VMEM_LIMIT_BYTES)


def _sigmoid(x):
    return 1.0 / (1.0 + jnp.exp(-x))


def _silu(x):
    return x * _sigmoid(x)


def _log_sigmoid(x):
    return jnp.minimum(x, 0.0) - jnp.log1p(jnp.exp(-jnp.abs(x)))


def _lower_tri(n):
    rows = lax.broadcasted_iota(jnp.int32, (n, n), 0)
    cols = lax.broadcasted_iota(jnp.int32, (n, n), 1)
    return (rows >= cols).astype(F32)


def _rmsnorm_kernel(x_ref, w_ref, o_ref):
    x = x_ref[...]
    ms = jnp.mean(x * x, axis=-1, keepdims=True)
    o_ref[...] = (x * lax.rsqrt(ms + EPS) * w_ref[...]).astype(o_ref.dtype)


def _rmsnorm(x, w, out_dtype):
    t, d = x.shape
    return pl.pallas_call(
        _rmsnorm_kernel,
        grid=(t // NORM_ROWS,),
        in_specs=[pl.BlockSpec((NORM_ROWS, d), lambda i: (i, 0)),
                  pl.BlockSpec((1, d), lambda i: (0, 0))],
        out_specs=pl.BlockSpec((NORM_ROWS, d), lambda i: (i, 0)),
        out_shape=jax.ShapeDtypeStruct((t, d), out_dtype),
        compiler_params=_params("parallel"),
        name="rmsnorm",
    )(x, w.reshape(1, d))


def _cast_weight_tile(w_ref, wb_ref, valid_cols):
    w = w_ref[...]
    if valid_cols < w.shape[1]:
        col = lax.broadcasted_iota(jnp.int32, w.shape, 1)
        w = jnp.where(col < valid_cols, w, 0.0)
    wb_ref[...] = w.astype(BF16)


def _matmul_kernel(x_ref, w_ref, o_ref, wb_ref):
    @pl.when(pl.program_id(1) == 0)
    def _():
        _cast_weight_tile(w_ref, wb_ref, w_ref.shape[1])

    o_ref[...] = jnp.dot(x_ref[...], wb_ref[...], preferred_element_type=F32).astype(o_ref.dtype)


def _matmul(x, w, layer, out_dtype, n_out, rows, cols, name):
    t, k = x.shape
    rows = min(rows, t)
    assert t % rows == 0 and n_out % cols == 0 and n_out <= w.shape[2]
    return pl.pallas_call(
        _matmul_kernel,
        grid=(n_out // cols, t // rows),
        in_specs=[pl.BlockSpec((rows, k), lambda j, i: (i, 0)),
                  pl.BlockSpec((None, k, cols), lambda j, i: (layer, 0, j))],
        out_specs=pl.BlockSpec((rows, cols), lambda j, i: (i, j)),
        out_shape=jax.ShapeDtypeStruct((t, n_out), out_dtype),
        scratch_shapes=[pltpu.VMEM((k, cols), BF16)],
        compiler_params=_params("parallel", "arbitrary"),
        name=name,
    )(x, w)


def _residual_norm_kernel(*refs, n_parts, keep_sum):
    x_refs, (w_ref, r_ref, nw_ref), out_refs = refs[:n_parts], refs[n_parts:n_parts + 3], refs[n_parts + 3:]
    y = r_ref[...]
    k0 = 0
    for x_ref in x_refs:
        k1 = k0 + x_ref.shape[1]
        y = y + jnp.dot(x_ref[...], w_ref[k0:k1, :], preferred_element_type=F32)
        k0 = k1
    if keep_sum:
        out_refs[0][...] = y
    h_ref = out_refs[-1]
    ms = jnp.mean(y * y, axis=-1, keepdims=True)
    h_ref[...] = (y * lax.rsqrt(ms + EPS) * nw_ref[...]).astype(h_ref.dtype)


def _matmul_residual_norm(xs, w, residual, norm_w, h_dtype, rows, keep_sum=True, name="matmul_residual_norm"):
    t = xs[0].shape[0]
    k, n = w.shape
    assert t % rows == 0 and sum(x.shape[1] for x in xs) == k
    row_block = pl.BlockSpec((rows, n), lambda i: (i, 0))
    out_shape = [jax.ShapeDtypeStruct((t, n), h_dtype)]
    if keep_sum:
        out_shape.insert(0, jax.ShapeDtypeStruct((t, n), F32))
    outs = pl.pallas_call(
        functools.partial(_residual_norm_kernel, n_parts=len(xs), keep_sum=keep_sum),
        grid=(t // rows,),
        in_specs=[pl.BlockSpec((rows, x.shape[1]), lambda i: (i, 0)) for x in xs]
        + [pl.BlockSpec((k, n), lambda i: (0, 0), pipeline_mode=pl.Buffered(1)),
           row_block,
           pl.BlockSpec((1, n), lambda i: (0, 0))],
        out_specs=[row_block] * len(out_shape),
        out_shape=out_shape,
        compiler_params=_params("parallel"),
        name=name,
    )(*xs, w, residual, norm_w.reshape(1, n))
    return (outs[0], outs[1]) if keep_sum else (None, outs[0])


def _hgrn_kernel(q_ref, f_ref, i_ref, g_ref, lbl_ref, nw_ref, tri3_ref, sel_ref, mask_ref, o_ref, state_ref,
                 *, layer_j):
    C, L = HGRN_CHUNK, HGRN_SUB
    n_sub = C // L

    @pl.when(pl.program_id(2) == 0)
    def _():
        state_ref[...] = jnp.zeros_like(state_ref)

    lbl = lbl_ref[...]
    e = jnp.exp(lbl - jnp.max(lbl, axis=0, keepdims=True))
    sm = e / jnp.sum(e, axis=0, keepdims=True)
    lb_all = jnp.zeros((1, lbl.shape[1]), F32)
    for r in range(1, layer_j + 1):
        lb_all = lb_all + sm[r:r + 1, :]

    def pair_products(q, k, b, s):
        first = (s // SUBLANES) * SUBLANES
        q3, k3, b3 = (x.reshape(n_sub, L, HEAD_DIM) for x in (q, k, b))
        decay = jnp.exp2(jnp.minimum(b3[:, first:, :] - b3[:, s:s + 1, :], 0.0))
        prod = q3[:, first:, :] * k3[:, s:s + 1, :] * decay
        if first:
            prod = jnp.concatenate([jnp.zeros((n_sub, first, HEAD_DIM), F32), prod], axis=1)
        return prod.reshape(C, HEAD_DIM).astype(BF16)

    def head_chunk(rows, lanes, head):
        lb_floor, one_m_lb = jnp.maximum(lb_all[:, lanes], LB_FLOOR), 1.0 - lb_all[:, lanes]
        z = f_ref[0, rows, lanes]
        q = _silu(q_ref[0, rows, lanes])
        v = i_ref[0, rows, lanes].astype(BF16)

        e_z = jnp.exp(-jnp.abs(z))
        inv = 1.0 / (1.0 + e_z)
        log_f = jnp.log(lb_floor + one_m_lb * (jnp.where(z >= 0.0, 1.0, e_z) * inv))
        k = one_m_lb * (jnp.where(z >= 0.0, e_z, 1.0) * inv)
        hi = log_f.astype(BF16)
        rest = log_f - hi.astype(F32)
        mid = rest.astype(BF16)
        low = (rest - mid.astype(F32)).astype(BF16)
        b = jnp.dot(tri3_ref[...], jnp.concatenate([hi, mid, low], axis=0),
                    preferred_element_type=F32) * LOG2_E
        b_last = b[C - 1:C, :]

        state = state_ref[head]
        o = lax.dot_general((q * jnp.exp2(b)).astype(BF16), state.astype(BF16), NT_DIMS,
                            preferred_element_type=F32)

        below = [jnp.zeros((L, C), F32)]
        for i in range(1, n_sub):
            lo = i * L
            edge = b[lo - 1:lo, :]
            q_edge = (q[lo:lo + L, :] * jnp.exp2(b[lo:lo + L, :] - edge)).astype(BF16)
            k_edge = (k[:lo, :] * jnp.exp2(edge - b[:lo, :])).astype(BF16)
            k_edge = jnp.concatenate([k_edge, jnp.zeros((C - lo, HEAD_DIM), BF16)], axis=0)
            below.append(lax.dot_general(q_edge, k_edge, NT_DIMS, preferred_element_type=F32))
        pair = jnp.concatenate([pair_products(q, k, b, s) for s in range(L)], axis=1)
        diag = jnp.dot(pair, sel_ref[...], preferred_element_type=F32)
        scores = jnp.concatenate(below, axis=0) + diag * mask_ref[...]
        o = o + jnp.dot(scores.astype(BF16), v, preferred_element_type=F32)

        k_end = (k * jnp.exp2(b_last - b)).astype(BF16)
        state_ref[head] = state * jnp.exp2(b_last) + lax.dot_general(v, k_end, TN_DIMS,
                                                                     preferred_element_type=F32)

        y = o * lax.rsqrt(jnp.mean(o * o, axis=-1, keepdims=True) + EPS) * nw_ref[:, lanes]
        o_ref[0, rows, lanes] = (y * _silu(g_ref[0, rows, lanes])).astype(o_ref.dtype)

    def chunk(c, carry):
        rows = pl.ds(pl.multiple_of(c * C, C), C)
        for head in range(HGRN_HEADS_PER_STEP):
            head_chunk(rows, slice(head * HEAD_DIM, (head + 1) * HEAD_DIM), head)
        return carry

    lax.fori_loop(0, MIX_ROWS // C, chunk, 0, unroll=HGRN_UNROLL)


def _hgrn(proj, lb_logits, norm_w, layer_j):
    bsz, seq, _ = proj.shape
    d_h = norm_w.shape[0]
    width = HGRN_HEADS_PER_STEP * HEAD_DIM
    n_groups = d_h // width
    n_layers = lb_logits.shape[0]

    C, L = HGRN_CHUNK, HGRN_SUB
    pos = jnp.arange(C)
    causal = pos[:, None] >= pos[None, :]
    tri3 = jnp.tile(causal.astype(BF16), (1, 3))
    pair_sel = (jnp.repeat(jnp.arange(L), HEAD_DIM)[:, None] == (pos % L)[None, :]).astype(BF16)
    diag_mask = (causal & (pos[:, None] // L == pos[None, :] // L)).astype(F32)

    def col(group):
        return pl.BlockSpec((1, MIX_ROWS, width), lambda b, h, s: (b, s, group * n_groups + h))

    def whole(a):
        return pl.BlockSpec(a.shape, lambda b, h, s: (0, 0))

    return pl.pallas_call(
        functools.partial(_hgrn_kernel, layer_j=layer_j),
        grid=(bsz, n_groups, seq // MIX_ROWS),
        in_specs=[col(0), col(1), col(2), col(3),
                  pl.BlockSpec((n_layers, width), lambda b, h, s: (0, h)),
                  pl.BlockSpec((1, width), lambda b, h, s: (0, h)),
                  whole(tri3), whole(pair_sel), whole(diag_mask)],
        out_specs=pl.BlockSpec((1, MIX_ROWS, width), lambda b, h, s: (b, s, h)),
        out_shape=jax.ShapeDtypeStruct((bsz, seq, d_h), BF16),
        scratch_shapes=[pltpu.VMEM((HGRN_HEADS_PER_STEP, HEAD_DIM, HEAD_DIM), F32)],
        compiler_params=_params("parallel", "parallel", "arbitrary"),
        name="hgrn2",
    )(proj, proj, proj, proj, lb_logits, norm_w.reshape(1, d_h), tri3, pair_sel, diag_mask)


def _retention_kernel(q_ref, k_ref, v_ref, g_ref, cos_ref, sin_ref, intra_ref, edge_ref, nw_ref, o_ref, state_ref):
    C = RET_CHUNK

    @pl.when(pl.program_id(2) == 0)
    def _():
        state_ref[...] = jnp.zeros_like(state_ref)

    def rope(x, cos, sin_signed):
        return x * cos + pltpu.roll(x, HEAD_DIM // 2, 1) * sin_signed

    def head_chunk(sl, lanes, head):
        cos, sin_signed = cos_ref[sl, :], sin_ref[sl, :]
        q = rope(q_ref[0, sl, lanes], cos, sin_signed)
        k = rope(k_ref[0, sl, lanes], cos, sin_signed) * (HEAD_DIM ** -0.5)
        v = v_ref[0, sl, lanes].astype(BF16)
        state = state_ref[head]
        w_start, w_end, decay_chunk = edge_ref[head, 0], edge_ref[head, 1], edge_ref[head, 2, 0:1, :]

        scores = lax.dot_general(q.astype(BF16), k.astype(BF16), NT_DIMS,
                                 preferred_element_type=F32) * intra_ref[head]
        o = jnp.dot(scores.astype(BF16), v, preferred_element_type=F32)
        o = o + lax.dot_general((q * w_start).astype(BF16), state.astype(BF16), NT_DIMS,
                                preferred_element_type=F32)
        state_ref[head] = decay_chunk * state + lax.dot_general(v, (k * w_end).astype(BF16), TN_DIMS,
                                                                preferred_element_type=F32)

        cen = o - jnp.mean(o, axis=-1, keepdims=True)
        y = cen * lax.rsqrt(jnp.mean(cen * cen, axis=-1, keepdims=True) + EPS) * nw_ref[:, lanes]
        o_ref[0, sl, lanes] = (y * _silu(g_ref[0, sl, lanes])).astype(o_ref.dtype)

    def chunk(c, carry):
        sl = pl.ds(pl.multiple_of(c * C, C), C)
        for head in range(RET_HEADS_PER_STEP):
            head_chunk(sl, slice(head * HEAD_DIM, (head + 1) * HEAD_DIM), head)
        return carry

    lax.fori_loop(0, MIX_ROWS // C, chunk, 0, unroll=RET_UNROLL)


def _retention(proj, norm_w, first_group):
    bsz, seq, _ = proj.shape
    d_r = norm_w.shape[0]
    n_heads = d_r // HEAD_DIM
    half = HEAD_DIM // 2

    inv = ROPE_BASE ** (-jnp.arange(0, HEAD_DIM, 2, dtype=F32) / HEAD_DIM)
    ang = jnp.arange(seq).astype(F32)[:, None] * inv[None, :]
    cos = jnp.concatenate([jnp.cos(ang), jnp.cos(ang)], axis=-1)
    sin_signed = jnp.concatenate([-jnp.sin(ang), jnp.sin(ang)], axis=-1)
    assert cos.shape == (seq, 2 * half)
    C = RET_CHUNK
    log_gamma = jnp.log1p(-jnp.exp2(-RET_DECAY_BASE - jnp.arange(n_heads, dtype=F32)))
    pos = jnp.arange(C, dtype=F32)
    rel = pos[:, None] - pos[None, :]
    intra = jnp.where(rel >= 0, jnp.exp(log_gamma[:, None, None] * jnp.maximum(rel, 0.0)), 0.0)
    w_start = jnp.exp(log_gamma[:, None] * (pos + 1.0)[None, :])
    w_end = jnp.exp(log_gamma[:, None] * (C - 1.0 - pos)[None, :])
    decay_chunk = jnp.broadcast_to(jnp.exp(log_gamma * C)[:, None], (n_heads, C))
    edge = jnp.broadcast_to(jnp.stack([w_start, w_end, decay_chunk], axis=1)[..., None],
                            (n_heads, 3, C, HEAD_DIM))

    per_step = RET_HEADS_PER_STEP
    width = per_step * HEAD_DIM
    n_groups = n_heads // per_step
    first = first_group * n_groups

    def col(group):
        return pl.BlockSpec((1, MIX_ROWS, width), lambda b, h, s: (b, s, first + group * n_groups + h))

    table = pl.BlockSpec((MIX_ROWS, HEAD_DIM), lambda b, h, s: (s, 0))
    return pl.pallas_call(
        _retention_kernel,
        grid=(bsz, n_groups, seq // MIX_ROWS),
        in_specs=[col(0), col(1), col(2), col(3), table, table,
                  pl.BlockSpec((per_step, C, C), lambda b, h, s: (h, 0, 0)),
                  pl.BlockSpec((per_step, 3, C, HEAD_DIM), lambda b, h, s: (h, 0, 0, 0)),
                  pl.BlockSpec((1, width), lambda b, h, s: (0, h))],
        out_specs=pl.BlockSpec((1, MIX_ROWS, width), lambda b, h, s: (b, s, h)),
        out_shape=jax.ShapeDtypeStruct((bsz, seq, d_r), BF16),
        scratch_shapes=[pltpu.VMEM((per_step, HEAD_DIM, HEAD_DIM), F32)],
        compiler_params=_params("parallel", "parallel", "arbitrary"),
        name="retention",
    )(proj, proj, proj, proj, cos, sin_signed, intra, edge, norm_w.reshape(1, d_r))


def _forget_scan_kernel(h_ref, w_ref, bias_ref, c_ref, wb_ref, carry_ref, *, valid_cols):
    @pl.when(pl.program_id(1) == 0)
    def _():
        _cast_weight_tile(w_ref, wb_ref, valid_cols)
        carry_ref[...] = jnp.zeros_like(carry_ref)

    f = jnp.dot(h_ref[0], wb_ref[...], preferred_element_type=F32)
    log_f = _log_sigmoid(f + bias_ref[...])
    c = jnp.dot(_lower_tri(SCAN_ROWS), log_f, precision=lax.Precision.HIGHEST,
                preferred_element_type=F32) + carry_ref[...]
    c_ref[0] = c
    carry_ref[...] = c[SCAN_ROWS - 1:SCAN_ROWS, :]


def _forget_scan(h, w_in, layer, first_col, bias):
    bsz, seq, d = h.shape
    assert first_col % LANES == 0 and w_in.shape[2] - first_col <= LANES
    return pl.pallas_call(
        functools.partial(_forget_scan_kernel, valid_cols=w_in.shape[2] - first_col),
        grid=(bsz, seq // SCAN_ROWS),
        in_specs=[pl.BlockSpec((1, SCAN_ROWS, d), lambda b, s: (b, s, 0)),
                  pl.BlockSpec((None, d, LANES), lambda b, s: (layer, 0, first_col // LANES)),
                  pl.BlockSpec((1, LANES), lambda b, s: (0, 0))],
        out_specs=pl.BlockSpec((1, SCAN_ROWS, LANES), lambda b, s: (b, s, 0)),
        out_shape=jax.ShapeDtypeStruct((bsz, seq, LANES), F32),
        scratch_shapes=[pltpu.VMEM((d, LANES), BF16), pltpu.VMEM((1, LANES), F32)],
        compiler_params=_params("parallel", "arbitrary"),
        name="forget_scan",
    )(h, w_in, bias)


def _split3(x):
    hi = x.astype(BF16).astype(F32)
    rest = x - hi
    mid = rest.astype(BF16).astype(F32)
    low = (rest - mid).astype(BF16).astype(F32)
    return hi, mid, low


def _fox_kernel(q_ref, k_ref, v_ref, g_ref, ct_ref, o_ref,
                kaug_ref, vt_ref, qaug_ref, m_ref, acc_ref, s0_ref, s1_ref, p0_ref, p1_ref, a0_ref, a1_ref):
    T = FOX_BLOCK
    n_blocks = vt_ref.shape[0]
    aug_row = lax.broadcasted_iota(jnp.int32, (HEAD_DIM, T), 0)

    def bias_rows(c_row, for_queries):
        hi, mid, low = _split3(c_row)
        if for_queries:
            pieces = jnp.where(aug_row == 3, hi, jnp.where(aug_row == 4, mid, jnp.where(aug_row == 5, low, 0.0)))
            return jnp.where(aug_row < 3, 1.0, pieces)
        pieces = jnp.where(aug_row == 0, -hi, jnp.where(aug_row == 1, -mid, jnp.where(aug_row == 2, -low, 0.0)))
        return jnp.where((aug_row >= 3) & (aug_row < 6), 1.0, pieces)

    for j in range(n_blocks):
        rows = slice(j * T, (j + 1) * T)
        c_j = ct_ref[0, 0, j:j + 1, :] * LOG2_E
        kaug_ref[rows, 0:HEAD_DIM] = k_ref[0, rows, :]
        kaug_ref[rows, HEAD_DIM:2 * HEAD_DIM] = bias_rows(c_j, False).T.astype(BF16)
        vt_ref[j, 0:HEAD_DIM, :] = v_ref[0, rows, :].astype(F32).T.astype(BF16)
        vt_ref[j, HEAD_DIM:, :] = jnp.ones((FOX_ONES_ROWS, T), BF16)
        qaug_ref[j, 0:HEAD_DIM, :] = (q_ref[0, rows, :].astype(F32) * (HEAD_DIM ** -0.5 * LOG2_E)).T.astype(BF16)
        qaug_ref[j, HEAD_DIM:2 * HEAD_DIM, :] = bias_rows(c_j, True).astype(BF16)
    key_id = lax.broadcasted_iota(jnp.int32, (T, T), 0)
    query_id = lax.broadcasted_iota(jnp.int32, (T, T), 1)

    def form_logits(qb, kb, s_ref):
        s_ref[...] = jnp.dot(kaug_ref[kb * T:(kb + 1) * T, :], qaug_ref[qb],
                             preferred_element_type=F32)

    def softmax_update(qb, kb, s_ref, p_ref, a_ref):
        s = s_ref[...]
        if qb == kb:
            s = jnp.where(key_id <= query_id, s, MASK_VALUE)
            m_new = jnp.max(s, axis=0, keepdims=True)
        else:
            m = m_ref[qb]
            m_new = jnp.maximum(m, jnp.max(s, axis=0, keepdims=True))
            a_ref[...] = jnp.exp2(m - m_new)
        p_ref[...] = jnp.exp2(s - m_new).astype(BF16)
        m_ref[qb] = m_new

    def accumulate(qb, kb, p_ref, a_ref):
        update = jnp.dot(vt_ref[kb], p_ref[...], preferred_element_type=F32)
        acc_ref[qb] = update if qb == kb else a_ref[...] * acc_ref[qb] + update

    pairs = [(qb, qb) for qb in range(n_blocks)] + [(qb, kb) for qb in range(n_blocks) for kb in range(qb)]
    buffers = ((s0_ref, p0_ref, a0_ref), (s1_ref, p1_ref, a1_ref))
    form_logits(*pairs[0], s0_ref)
    for n in range(len(pairs) + 1):
        s_now, p_now, a_now = buffers[n % 2]
        s_next, p_prev, a_prev = buffers[(n + 1) % 2]
        if n + 1 < len(pairs):
            form_logits(*pairs[n + 1], s_next)
        if n < len(pairs):
            softmax_update(*pairs[n], s_now, p_now, a_now)
        if n >= 1:
            accumulate(*pairs[n - 1], p_prev, a_prev)

    for j in range(n_blocks):
        rows = slice(j * T, (j + 1) * T)
        acc = acc_ref[j]
        o = (acc[:HEAD_DIM, :] / acc[HEAD_DIM:HEAD_DIM + 1, :]).T
        o_ref[0, rows, :] = (o * _sigmoid(g_ref[0, rows, :].astype(F32))).astype(o_ref.dtype)


def _fox(proj, c, n_heads):
    bsz, seq, _ = proj.shape
    T = FOX_BLOCK
    n_blocks = seq // T
    c_rows = jnp.transpose(c[:, :, :n_heads], (0, 2, 1)).reshape(bsz, n_heads, n_blocks, T)

    def head_cols(group):
        return pl.BlockSpec((1, seq, HEAD_DIM), lambda b, h: (b, 0, group * n_heads + h))

    return pl.pallas_call(
        _fox_kernel,
        grid=(bsz, n_heads),
        in_specs=[head_cols(0), head_cols(1), head_cols(2), head_cols(3),
                  pl.BlockSpec((1, 1, n_blocks, T), lambda b, h: (b, h, 0, 0))],
        out_specs=pl.BlockSpec((1, seq, HEAD_DIM), lambda b, h: (b, 0, h)),
        out_shape=jax.ShapeDtypeStruct((bsz, seq, n_heads * HEAD_DIM), BF16),
        scratch_shapes=[pltpu.VMEM((seq, 2 * HEAD_DIM), BF16),
                        pltpu.VMEM((n_blocks, HEAD_DIM + FOX_ONES_ROWS, T), BF16),
                        pltpu.VMEM((n_blocks, 2 * HEAD_DIM, T), BF16),
                        pltpu.VMEM((n_blocks, 1, T), F32),
                        pltpu.VMEM((n_blocks, HEAD_DIM + FOX_ONES_ROWS, T), F32),
                        pltpu.VMEM((T, T), F32), pltpu.VMEM((T, T), F32),
                        pltpu.VMEM((T, T), BF16), pltpu.VMEM((T, T), BF16),
                        pltpu.VMEM((1, T), F32), pltpu.VMEM((1, T), F32)],
        compiler_params=_params("parallel", "parallel"),
        name="fox_attention",
    )(proj, proj, proj, proj, c_rows)


def _ffn_up_kernel(x_ref, wg_ref, wv_ref, cg_ref, cv_ref, o_ref, ug_ref, uv_ref, wgb_ref, wvb_ref,
                   *, seq_steps, d_ff):
    rows = x_ref.shape[0]

    @pl.when(pl.program_id(1) == 0)
    def _():
        valid = d_ff - pl.program_id(0) * wg_ref.shape[1]
        col = lax.broadcasted_iota(jnp.int32, wg_ref.shape, 1)
        wgb_ref[...] = jnp.where(col < valid, wg_ref[...], 0.0).astype(BF16)
        wvb_ref[...] = jnp.where(col < valid, wv_ref[...], 0.0).astype(BF16)

    @pl.when(pl.program_id(1) % seq_steps == 0)
    def _():
        ug_ref[0:SUBLANES, :] = jnp.zeros((SUBLANES, ug_ref.shape[1]), F32)
        uv_ref[0:SUBLANES, :] = jnp.zeros((SUBLANES, uv_ref.shape[1]), F32)

    x = x_ref[...]

    def conv(wb_ref, c_ref, u_ref):
        u_ref[SUBLANES:SUBLANES + rows, :] = jnp.dot(x, wb_ref[...], preferred_element_type=F32)
        out = c_ref[CONV_WIDTH - 1:CONV_WIDTH, :] * u_ref[SUBLANES:SUBLANES + rows, :]
        for tap in range(CONV_WIDTH - 1):
            back = CONV_WIDTH - 1 - tap
            out = out + c_ref[tap:tap + 1, :] * u_ref[SUBLANES - back:SUBLANES - back + rows, :]
        carry = u_ref[rows:rows + SUBLANES, :]
        u_ref[0:SUBLANES, :] = carry
        return out

    gate = conv(wgb_ref, cg_ref, ug_ref)
    val = conv(wvb_ref, cv_ref, uv_ref)
    o_ref[...] = (_silu(gate) * val).astype(o_ref.dtype)


def _ffn_up(h, w_up, layer, conv_w, seq, d_ff, d_ff_pad):
    t, d = h.shape
    rows = min(MM_ROWS, seq)
    assert seq % rows == 0 and d_ff_pad % FFN_COLS == 0
    n_cols = d_ff_pad // FFN_COLS
    return pl.pallas_call(
        functools.partial(_ffn_up_kernel, seq_steps=seq // rows, d_ff=d_ff),
        grid=(n_cols, t // rows),
        in_specs=[pl.BlockSpec((rows, d), lambda j, i: (i, 0)),
                  pl.BlockSpec((None, d, FFN_COLS), lambda j, i: (layer, 0, j)),
                  pl.BlockSpec((None, pl.Element(d), pl.Element(FFN_COLS, (0, d_ff_pad - d_ff))),
                               lambda j, i: (layer, 0, pl.multiple_of(d_ff + j * FFN_COLS, LANES))),
                  pl.BlockSpec((CONV_WIDTH, FFN_COLS), lambda j, i: (0, j)),
                  pl.BlockSpec((CONV_WIDTH, FFN_COLS), lambda j, i: (0, j + n_cols))],
        out_specs=pl.BlockSpec((rows, FFN_COLS), lambda j, i: (i, j)),
        out_shape=jax.ShapeDtypeStruct((t, d_ff), BF16),
        scratch_shapes=[pltpu.VMEM((rows + SUBLANES, FFN_COLS), F32),
                        pltpu.VMEM((rows + SUBLANES, FFN_COLS), F32),
                        pltpu.VMEM((d, FFN_COLS), BF16),
                        pltpu.VMEM((d, FFN_COLS), BF16)],
        compiler_params=_params("parallel", "arbitrary"),
        name="ffn_up_conv",
    )(h, w_up, w_up, conv_w, conv_w)


def _pad_cols(w, width):
    return jnp.pad(w, ((0, 0), (0, width - w.shape[1])))


def kernel(x, attn_norm_w, ffn_norm_w, final_norm_w, even_w_in, hgrn_lb_logits, hgrn_norm_w, ret_norm_w,
           even_w_out, odd_w_in, odd_b_f, odd_w_out, ffn_w_up, ffn_conv_w, ffn_w_down):
    bsz, seq, d_model = x.shape
    depth = attn_norm_w.shape[0]
    tokens = bsz * seq
    d_hgrn = hgrn_norm_w.shape[1]
    d_fox = odd_w_out.shape[1]
    n_fox = odd_b_f.shape[1]
    d_ff = ffn_w_down.shape[1]
    d_ff_pad = FFN_COLS * math.ceil(d_ff / FFN_COLS)

    xf = x.reshape(tokens, d_model)
    h = _rmsnorm(xf, attn_norm_w[0], BF16)
    for layer in range(depth):
        j = layer // 2
        if layer % 2 == 0:
            proj = _matmul(h, even_w_in, j, F32, even_w_in.shape[2], rows=MM_ROWS, cols=IN_COLS,
                           name="even_in").reshape(bsz, seq, -1)
            o_a = _hgrn(proj, hgrn_lb_logits, hgrn_norm_w[j], j)
            o_b = _retention(proj, ret_norm_w[j], first_group=4 * d_hgrn // ret_norm_w.shape[1])
            mixed = [o_a.reshape(tokens, -1), o_b.reshape(tokens, -1)]
            w_out = even_w_out[j]
        else:
            proj = _matmul(h, odd_w_in, j, BF16, 4 * d_fox, rows=MM_ROWS, cols=IN_COLS,
                           name="fox_in").reshape(bsz, seq, -1)
            c = _forget_scan(h.reshape(bsz, seq, d_model), odd_w_in, j, 4 * d_fox,
                             _pad_cols(odd_b_f[j][None, :], LANES))
            mixed = [_fox(proj, c, n_fox).reshape(tokens, -1)]
            w_out = odd_w_out[j]
        xf, h = _matmul_residual_norm(mixed, w_out.astype(BF16), xf, ffn_norm_w[layer], BF16, rows=OUT_ROWS,
                                      name="mixer_out_norm")
        conv_w = ffn_conv_w[layer]
        conv_w = jnp.concatenate([_pad_cols(conv_w[:, :d_ff], d_ff_pad), _pad_cols(conv_w[:, d_ff:], d_ff_pad)],
                                 axis=1)
        act = _ffn_up(h, ffn_w_up, layer, conv_w, seq, d_ff, d_ff_pad)
        w_down = ffn_w_down[layer].astype(BF16)
        last = layer == depth - 1
        xf, h = _matmul_residual_norm([act], w_down, xf, final_norm_w if last else attn_norm_w[layer + 1],
                                      x.dtype if last else BF16, rows=DOWN_ROWS, keep_sum=not last,
                                      name="ffn_down_norm")
    return h.reshape(bsz, seq, d_model)
```

```python
import functools
import math

import jax
import jax.numpy as jnp
from jax import lax
from jax.experimental import pallas as pl
from jax.experimental.pallas import tpu as pltpu

F32 = jnp.float32
BF16 = jnp.bfloat16

HEAD_DIM = 128
ROPE_BASE = 10000.0
RET_DECAY_BASE = 5.0
EPS = 1e-6
LB_FLOOR = 1e-30
MASK_VALUE = -1e30
CONV_WIDTH = 3
LOG2_E = math.log2(math.e)

LANES = 128
SUBLANES = 8
VMEM_LIMIT_BYTES = 48 * 1024 * 1024

NORM_ROWS = 256
MM_ROWS = 1024
IN_COLS = 1024
OUT_ROWS = 512
DOWN_ROWS = 256
FFN_COLS = 512
MIX_ROWS = 512
HGRN_UNROLL = 4
RET_UNROLL = 2
HGRN_CHUNK = 64
HGRN_SUB = 16
HGRN_HEADS_PER_STEP = 8
RET_CHUNK = 128
RET_HEADS_PER_STEP = 4
FOX_BLOCK = 512
FOX_ONES_ROWS = 16
SCAN_ROWS = 256

NT_DIMS = (((1,), (1,)), ((), ()))
TN_DIMS = (((0,), (0,)), ((), ()))


def _params(*semantics):
    return pltpu.CompilerParams(dimension_semantics=semantics, vmem_limit_bytes=VMEM_LIMIT_BYTES)


def _sigmoid(x):
    return 1.0 / (1.0 + jnp.exp(-x))


def _silu(x):
    return x * _sigmoid(x)


def _log_sigmoid(x):
    return jnp.minimum(x, 0.0) - jnp.log1p(jnp.exp(-jnp.abs(x)))


def _lower_tri(n):
    rows = lax.broadcasted_iota(jnp.int32, (n, n), 0)
    cols = lax.broadcasted_iota(jnp.int32, (n, n), 1)
    return (rows >= cols).astype(F32)


def _rmsnorm_kernel(x_ref, w_ref, o_ref):
    x = x_ref[...]
    ms = jnp.mean(x * x, axis=-1, keepdims=True)
    o_ref[...] = (x * lax.rsqrt(ms + EPS) * w_ref[...]).astype(o_ref.dtype)


def _rmsnorm(x, w, out_dtype):
    t, d = x.shape
    return pl.pallas_call(
        _rmsnorm_kernel,
        grid=(t // NORM_ROWS,),
        in_specs=[pl.BlockSpec((NORM_ROWS, d), lambda i: (i, 0)),
                  pl.BlockSpec((1, d), lambda i: (0, 0))],
        out_specs=pl.BlockSpec((NORM_ROWS, d), lambda i: (i, 0)),
        out_shape=jax.ShapeDtypeStruct((t, d), out_dtype),
        compiler_params=_params("parallel"),
        name="rmsnorm",
    )(x, w.reshape(1, d))


def _cast_weight_tile(w_ref, wb_ref, valid_cols):
    w = w_ref[...]
    if valid_cols < w.shape[1]:
        col = lax.broadcasted_iota(jnp.int32, w.shape, 1)
        w = jnp.where(col < valid_cols, w, 0.0)
    wb_ref[...] = w.astype(BF16)


def _matmul_kernel(x_ref, w_ref, o_ref, wb_ref):
    @pl.when(pl.program_id(1) == 0)
    def _():
        _cast_weight_tile(w_ref, wb_ref, w_ref.shape[1])

    o_ref[...] = jnp.dot(x_ref[...], wb_ref[...], preferred_element_type=F32).astype(o_ref.dtype)


def _matmul(x, w, layer, out_dtype, n_out, rows, cols, name):
    t, k = x.shape
    rows = min(rows, t)
    assert t % rows == 0 and n_out % cols == 0 and n_out <= w.shape[2]
    return pl.pallas_call(
        _matmul_kernel,
        grid=(n_out // cols, t // rows),
        in_specs=[pl.BlockSpec((rows, k), lambda j, i: (i, 0)),
                  pl.BlockSpec((k, cols), lambda j, i: (layer, j))],
        out_specs=pl.BlockSpec((rows, cols), lambda j, i: (i, j)),
        out_shape=jax.ShapeDtypeStruct((t, n_out), out_dtype),
        scratch_shapes=[pltpu.VMEM((k, cols), BF16)],
        compiler_params=_params("parallel", "arbitrary"),
        name=name,
    )(x, w.reshape(-1, w.shape[2]))


def _residual_norm_kernel(*refs, n_parts, keep_sum):
    x_refs, (w_ref, r_ref, nw_ref), out_refs = refs[:n_parts], refs[n_parts:n_parts + 3], refs[n_parts + 3:]
    y = r_ref[...]
    k0 = 0
    for x_ref in x_refs:
        k1 = k0 + x_ref.shape[1]
        y = y + jnp.dot(x_ref[...], w_ref[k0:k1, :], preferred_element_type=F32)
        k0 = k1
    if keep_sum:
        out_refs[0][...] = y
    h_ref = out_refs[-1]
    ms = jnp.mean(y * y, axis=-1, keepdims=True)
    h_ref[...] = (y * lax.rsqrt(ms + EPS) * nw_ref[...]).astype(h_ref.dtype)


def _matmul_residual_norm(xs, w, residual, norm_w, h_dtype, rows, keep_sum=True, name="matmul_residual_norm"):
    t = xs[0].shape[0]
    k, n = w.shape
    assert t % rows == 0 and sum(x.shape[1] for x in xs) == k
    row_block = pl.BlockSpec((rows, n), lambda i: (i, 0))
    out_shape = [jax.ShapeDtypeStruct((t, n), h_dtype)]
    if keep_sum:
        out_shape.insert(0, jax.ShapeDtypeStruct((t, n), F32))
    outs = pl.pallas_call(
        functools.partial(_residual_norm_kernel, n_parts=len(xs), keep_sum=keep_sum),
        grid=(t // rows,),
        in_specs=[pl.BlockSpec((rows, x.shape[1]), lambda i: (i, 0)) for x in xs]
        + [pl.BlockSpec((k, n), lambda i: (0, 0), pipeline_mode=pl.Buffered(1)),
           row_block,
           pl.BlockSpec((1, n), lambda i: (0, 0))],
        out_specs=[row_block] * len(out_shape),
        out_shape=out_shape,
        compiler_params=_params("parallel"),
        name=name,
    )(*xs, w, residual, norm_w.reshape(1, n))
    return (outs[0], outs[1]) if keep_sum else (None, outs[0])


def _hgrn_kernel(q_ref, f_ref, i_ref, g_ref, lbl_ref, nw_ref, tri3_ref, sel_ref, mask_ref, o_ref, state_ref,
                 *, layer_j):
    C, L = HGRN_CHUNK, HGRN_SUB
    n_sub = C // L

    @pl.when(pl.program_id(2) == 0)
    def _():
        state_ref[...] = jnp.zeros_like(state_ref)

    lbl = lbl_ref[...]
    e = jnp.exp(lbl - jnp.max(lbl, axis=0, keepdims=True))
    sm = e / jnp.sum(e, axis=0, keepdims=True)
    lb_all = jnp.zeros((1, lbl.shape[1]), F32)
    for r in range(1, layer_j + 1):
        lb_all = lb_all + sm[r:r + 1, :]

    def pair_products(q, k, b, s):
        first = (s // SUBLANES) * SUBLANES
        q3, k3, b3 = (x.reshape(n_sub, L, HEAD_DIM) for x in (q, k, b))
        decay = jnp.exp2(jnp.minimum(b3[:, first:, :] - b3[:, s:s + 1, :], 0.0))
        prod = q3[:, first:, :] * k3[:, s:s + 1, :] * decay
        if first:
            prod = jnp.concatenate([jnp.zeros((n_sub, first, HEAD_DIM), F32), prod], axis=1)
        return prod.reshape(C, HEAD_DIM).astype(BF16)

    def head_chunk(rows, lanes, head):
        lb_floor, one_m_lb = jnp.maximum(lb_all[:, lanes], LB_FLOOR), 1.0 - lb_all[:, lanes]
        z = f_ref[0, rows, lanes]
        q = _silu(q_ref[0, rows, lanes])
        v = i_ref[0, rows, lanes].astype(BF16)

        e_z = jnp.exp(-jnp.abs(z))
        inv = 1.0 / (1.0 + e_z)
        log_f = jnp.log(lb_floor + one_m_lb * (jnp.where(z >= 0.0, 1.0, e_z) * inv))
        k = one_m_lb * (jnp.where(z >= 0.0, e_z, 1.0) * inv)
        hi = log_f.astype(BF16)
        rest = log_f - hi.astype(F32)
        mid = rest.astype(BF16)
        low = (rest - mid.astype(F32)).astype(BF16)
        b = jnp.dot(tri3_ref[...], jnp.concatenate([hi, mid, low], axis=0),
                    preferred_element_type=F32) * LOG2_E
        b_last = b[C - 1:C, :]

        state = state_ref[head]
        o = lax.dot_general((q * jnp.exp2(b)).astype(BF16), state.astype(BF16), NT_DIMS,
                            preferred_element_type=F32)

        below = [jnp.zeros((L, C), F32)]
        for i in range(1, n_sub):
            lo = i * L
            edge = b[lo - 1:lo, :]
            q_edge = (q[lo:lo + L, :] * jnp.exp2(b[lo:lo + L, :] - edge)).astype(BF16)
            k_edge = (k[:lo, :] * jnp.exp2(edge - b[:lo, :])).astype(BF16)
            k_edge = jnp.concatenate([k_edge, jnp.zeros((C - lo, HEAD_DIM), BF16)], axis=0)
            below.append(lax.dot_general(q_edge, k_edge, NT_DIMS, preferred_element_type=F32))
        pair = jnp.concatenate([pair_products(q, k, b, s) for s in range(L)], axis=1)
        diag = jnp.dot(pair, sel_ref[...], preferred_element_type=F32)
        scores = jnp.concatenate(below, axis=0) + diag * mask_ref[...]
        o = o + jnp.dot(scores.astype(BF16), v, preferred_element_type=F32)

        k_end = (k * jnp.exp2(b_last - b)).astype(BF16)
        state_ref[head] = state * jnp.exp2(b_last) + lax.dot_general(v, k_end, TN_DIMS,
                                                                     preferred_element_type=F32)

        y = o * lax.rsqrt(jnp.mean(o * o, axis=-1, keepdims=True) + EPS) * nw_ref[:, lanes]
        o_ref[0, rows, lanes] = (y * _silu(g_ref[0, rows, lanes])).astype(o_ref.dtype)

    def chunk(c, carry):
        rows = pl.ds(pl.multiple_of(c * C, C), C)
        for head in range(HGRN_HEADS_PER_STEP):
            head_chunk(rows, slice(head * HEAD_DIM, (head + 1) * HEAD_DIM), head)
        return carry

    lax.fori_loop(0, MIX_ROWS // C, chunk, 0, unroll=HGRN_UNROLL)


def _hgrn(proj, lb_logits, norm_w, layer_j):
    bsz, seq, _ = proj.shape
    d_h = norm_w.shape[0]
    width = HGRN_HEADS_PER_STEP * HEAD_DIM
    n_groups = d_h // width
    n_layers = lb_logits.shape[0]

    C, L = HGRN_CHUNK, HGRN_SUB
    pos = jnp.arange(C)
    causal = pos[:, None] >= pos[None, :]
    tri3 = jnp.tile(causal.astype(BF16), (1, 3))
    pair_sel = (jnp.repeat(jnp.arange(L), HEAD_DIM)[:, None] == (pos % L)[None, :]).astype(BF16)
    diag_mask = (causal & (pos[:, None] // L == pos[None, :] // L)).astype(F32)

    def col(group):
        return pl.BlockSpec((1, MIX_ROWS, width), lambda b, h, s: (b, s, group * n_groups + h))

    def whole(a):
        return pl.BlockSpec(a.shape, lambda b, h, s: (0, 0))

    return pl.pallas_call(
        functools.partial(_hgrn_kernel, layer_j=layer_j),
        grid=(bsz, n_groups, seq // MIX_ROWS),
        in_specs=[col(0), col(1), col(2), col(3),
                  pl.BlockSpec((n_layers, width), lambda b, h, s: (0, h)),
                  pl.BlockSpec((1, width), lambda b, h, s: (0, h)),
                  whole(tri3), whole(pair_sel), whole(diag_mask)],
        out_specs=pl.BlockSpec((1, MIX_ROWS, width), lambda b, h, s: (b, s, h)),
        out_shape=jax.ShapeDtypeStruct((bsz, seq, d_h), BF16),
        scratch_shapes=[pltpu.VMEM((HGRN_HEADS_PER_STEP, HEAD_DIM, HEAD_DIM), F32)],
        compiler_params=_params("parallel", "parallel", "arbitrary"),
        name="hgrn2",
    )(proj, proj, proj, proj, lb_logits, norm_w.reshape(1, d_h), tri3, pair_sel, diag_mask)


def _retention_kernel(q_ref, k_ref, v_ref, g_ref, cos_ref, sin_ref, intra_ref, edge_ref, nw_ref, o_ref, state_ref):
    C = RET_CHUNK

    @pl.when(pl.program_id(2) == 0)
    def _():
        state_ref[...] = jnp.zeros_like(state_ref)

    def rope(x, cos, sin_signed):
        return x * cos + pltpu.roll(x, HEAD_DIM // 2, 1) * sin_signed

    def head_chunk(sl, lanes, head):
        cos, sin_signed = cos_ref[sl, :], sin_ref[sl, :]
        q = rope(q_ref[0, sl, lanes], cos, sin_signed)
        k = rope(k_ref[0, sl, lanes], cos, sin_signed) * (HEAD_DIM ** -0.5)
        v = v_ref[0, sl, lanes].astype(BF16)
        state = state_ref[head]
        w_start, w_end, decay_chunk = edge_ref[head, 0], edge_ref[head, 1], edge_ref[head, 2, 0:1, :]

        scores = lax.dot_general(q.astype(BF16), k.astype(BF16), NT_DIMS,
                                 preferred_element_type=F32) * intra_ref[head]
        o = jnp.dot(scores.astype(BF16), v, preferred_element_type=F32)
        o = o + lax.dot_general((q * w_start).astype(BF16), state.astype(BF16), NT_DIMS,
                                preferred_element_type=F32)
        state_ref[head] = decay_chunk * state + lax.dot_general(v, (k * w_end).astype(BF16), TN_DIMS,
                                                                preferred_element_type=F32)

        cen = o - jnp.mean(o, axis=-1, keepdims=True)
        y = cen * lax.rsqrt(jnp.mean(cen * cen, axis=-1, keepdims=True) + EPS) * nw_ref[:, lanes]
        o_ref[0, sl, lanes] = (y * _silu(g_ref[0, sl, lanes])).astype(o_ref.dtype)

    def chunk(c, carry):
        sl = pl.ds(pl.multiple_of(c * C, C), C)
        for head in range(RET_HEADS_PER_STEP):
            head_chunk(sl, slice(head * HEAD_DIM, (head + 1) * HEAD_DIM), head)
        return carry

    lax.fori_loop(0, MIX_ROWS // C, chunk, 0, unroll=RET_UNROLL)


def _retention(proj, norm_w, first_group):
    bsz, seq, _ = proj.shape
    d_r = norm_w.shape[0]
    n_heads = d_r // HEAD_DIM
    half = HEAD_DIM // 2

    inv = ROPE_BASE ** (-jnp.arange(0, HEAD_DIM, 2, dtype=F32) / HEAD_DIM)
    ang = jnp.arange(seq).astype(F32)[:, None] * inv[None, :]
    cos = jnp.concatenate([jnp.cos(ang), jnp.cos(ang)], axis=-1)
    sin_signed = jnp.concatenate([-jnp.sin(ang), jnp.sin(ang)], axis=-1)
    assert cos.shape == (seq, 2 * half)
    C = RET_CHUNK
    log_gamma = jnp.log1p(-jnp.exp2(-RET_DECAY_BASE - jnp.arange(n_heads, dtype=F32)))
    pos = jnp.arange(C, dtype=F32)
    rel = pos[:, None] - pos[None, :]
    intra = jnp.where(rel >= 0, jnp.exp(log_gamma[:, None, None] * jnp.maximum(rel, 0.0)), 0.0)
    w_start = jnp.exp(log_gamma[:, None] * (pos + 1.0)[None, :])
    w_end = jnp.exp(log_gamma[:, None] * (C - 1.0 - pos)[None, :])
    decay_chunk = jnp.broadcast_to(jnp.exp(log_gamma * C)[:, None], (n_heads, C))
    edge = jnp.broadcast_to(jnp.stack([w_start, w_end, decay_chunk], axis=1)[..., None],
                            (n_heads, 3, C, HEAD_DIM))

    per_step = RET_HEADS_PER_STEP
    width = per_step * HEAD_DIM
    n_groups = n_heads // per_step
    first = first_group * n_groups

    def col(group):
        return pl.BlockSpec((1, MIX_ROWS, width), lambda b, h, s: (b, s, first + group * n_groups + h))

    table = pl.BlockSpec((MIX_ROWS, HEAD_DIM), lambda b, h, s: (s, 0))
    return pl.pallas_call(
        _retention_kernel,
        grid=(bsz, n_groups, seq // MIX_ROWS),
        in_specs=[col(0), col(1), col(2), col(3), table, table,
                  pl.BlockSpec((per_step, C, C), lambda b, h, s: (h, 0, 0)),
                  pl.BlockSpec((per_step, 3, C, HEAD_DIM), lambda b, h, s: (h, 0, 0, 0)),
                  pl.BlockSpec((1, width), lambda b, h, s: (0, h))],
        out_specs=pl.BlockSpec((1, MIX_ROWS, width), lambda b, h, s: (b, s, h)),
        out_shape=jax.ShapeDtypeStruct((bsz, seq, d_r), BF16),
        scratch_shapes=[pltpu.VMEM((per_step, HEAD_DIM, HEAD_DIM), F32)],
        compiler_params=_params("parallel", "parallel", "arbitrary"),
        name="retention",
    )(proj, proj, proj, proj, cos, sin_signed, intra, edge, norm_w.reshape(1, d_r))


def _forget_scan_kernel(h_ref, w_ref, bias_ref, c_ref, wb_ref, carry_ref, *, valid_cols):
    @pl.when(pl.program_id(1) == 0)
    def _():
        _cast_weight_tile(w_ref, wb_ref, valid_cols)
        carry_ref[...] = jnp.zeros_like(carry_ref)

    f = jnp.dot(h_ref[0], wb_ref[...], preferred_element_type=F32)
    log_f = _log_sigmoid(f + bias_ref[...])
    c = jnp.dot(_lower_tri(SCAN_ROWS), log_f, precision=lax.Precision.HIGHEST,
                preferred_element_type=F32) + carry_ref[...]
    c_ref[0] = c
    carry_ref[...] = c[SCAN_ROWS - 1:SCAN_ROWS, :]


def _forget_scan(h, w_in, layer, first_col, bias):
    bsz, seq, d = h.shape
    assert first_col % LANES == 0 and w_in.shape[2] - first_col <= LANES
    return pl.pallas_call(
        functools.partial(_forget_scan_kernel, valid_cols=w_in.shape[2] - first_col),
        grid=(bsz, seq // SCAN_ROWS),
        in_specs=[pl.BlockSpec((1, SCAN_ROWS, d), lambda b, s: (b, s, 0)),
                  pl.BlockSpec((d, LANES), lambda b, s: (layer, first_col // LANES)),
                  pl.BlockSpec((1, LANES), lambda b, s: (0, 0))],
        out_specs=pl.BlockSpec((1, SCAN_ROWS, LANES), lambda b, s: (b, s, 0)),
        out_shape=jax.ShapeDtypeStruct((bsz, seq, LANES), F32),
        scratch_shapes=[pltpu.VMEM((d, LANES), BF16), pltpu.VMEM((1, LANES), F32)],
        compiler_params=_params("parallel", "arbitrary"),
        name="forget_scan",
    )(h, w_in.reshape(-1, w_in.shape[2]), bias)


def _split3(x):
    hi = x.astype(BF16).astype(F32)
    rest = x - hi
    mid = rest.astype(BF16).astype(F32)
    low = (rest - mid).astype(BF16).astype(F32)
    return hi, mid, low


def _fox_kernel(q_ref, k_ref, v_ref, g_ref, ct_ref, o_ref,
                kaug_ref, vt_ref, qaug_ref, m_ref, acc_ref, s0_ref, s1_ref, p0_ref, p1_ref, a0_ref, a1_ref):
    T = FOX_BLOCK
    n_blocks = vt_ref.shape[0]
    aug_row = lax.broadcasted_iota(jnp.int32, (HEAD_DIM, T), 0)

    def bias_rows(c_row, for_queries):
        hi, mid, low = _split3(c_row)
        if for_queries:
            pieces = jnp.where(aug_row == 3, hi, jnp.where(aug_row == 4, mid, jnp.where(aug_row == 5, low, 0.0)))
            return jnp.where(aug_row < 3, 1.0, pieces)
        pieces = jnp.where(aug_row == 0, -hi, jnp.where(aug_row == 1, -mid, jnp.where(aug_row == 2, -low, 0.0)))
        return jnp.where((aug_row >= 3) & (aug_row < 6), 1.0, pieces)

    for j in range(n_blocks):
        rows = slice(j * T, (j + 1) * T)
        c_j = ct_ref[0, 0, j:j + 1, :] * LOG2_E
        kaug_ref[rows, 0:HEAD_DIM] = k_ref[0, rows, :]
        kaug_ref[rows, HEAD_DIM:2 * HEAD_DIM] = bias_rows(c_j, False).T.astype(BF16)
        vt_ref[j, 0:HEAD_DIM, :] = v_ref[0, rows, :].astype(F32).T.astype(BF16)
        vt_ref[j, HEAD_DIM:, :] = jnp.ones((FOX_ONES_ROWS, T), BF16)
        qaug_ref[j, 0:HEAD_DIM, :] = (q_ref[0, rows, :].astype(F32) * (HEAD_DIM ** -0.5 * LOG2_E)).T.astype(BF16)
        qaug_ref[j, HEAD_DIM:2 * HEAD_DIM, :] = bias_rows(c_j, True).astype(BF16)
    key_id = lax.broadcasted_iota(jnp.int32, (T, T), 0)
    query_id = lax.broadcasted_iota(jnp.int32, (T, T), 1)

    def form_logits(qb, kb, s_ref):
        s_ref[...] = jnp.dot(kaug_ref[kb * T:(kb + 1) * T, :], qaug_ref[qb],
                             preferred_element_type=F32)

    def softmax_update(qb, kb, s_ref, p_ref, a_ref):
        s = s_ref[...]
        if qb == kb:
            s = jnp.where(key_id <= query_id, s, MASK_VALUE)
            m_new = jnp.max(s, axis=0, keepdims=True)
        else:
            m = m_ref[qb]
            m_new = jnp.maximum(m, jnp.max(s, axis=0, keepdims=True))
            a_ref[...] = jnp.exp2(m - m_new)
        p_ref[...] = jnp.exp2(s - m_new).astype(BF16)
        m_ref[qb] = m_new

    def accumulate(qb, kb, p_ref, a_ref):
        update = jnp.dot(vt_ref[kb], p_ref[...], preferred_element_type=F32)
        acc_ref[qb] = update if qb == kb else a_ref[...] * acc_ref[qb] + update

    pairs = [(qb, qb) for qb in range(n_blocks)] + [(qb, kb) for qb in range(n_blocks) for kb in range(qb)]
    buffers = ((s0_ref, p0_ref, a0_ref), (s1_ref, p1_ref, a1_ref))
    form_logits(*pairs[0], s0_ref)
    for n in range(len(pairs) + 1):
        s_now, p_now, a_now = buffers[n % 2]
        s_next, p_prev, a_prev = buffers[(n + 1) % 2]
        if n + 1 < len(pairs):
            form_logits(*pairs[n + 1], s_next)
        if n < len(pairs):
            softmax_update(*pairs[n], s_now, p_now, a_now)
        if n >= 1:
            accumulate(*pairs[n - 1], p_prev, a_prev)

    for j in range(n_blocks):
        rows = slice(j * T, (j + 1) * T)
        acc = acc_ref[j]
        o = (acc[:HEAD_DIM, :] / acc[HEAD_DIM:HEAD_DIM + 1, :]).T
        o_ref[0, rows, :] = (o * _sigmoid(g_ref[0, rows, :].astype(F32))).astype(o_ref.dtype)


def _fox(proj, c, n_heads):
    bsz, seq, _ = proj.shape
    T = FOX_BLOCK
    n_blocks = seq // T
    c_rows = jnp.transpose(c[:, :, :n_heads], (0, 2, 1)).reshape(bsz, n_heads, n_blocks, T)

    def head_cols(group):
        return pl.BlockSpec((1, seq, HEAD_DIM), lambda b, h: (b, 0, group * n_heads + h))

    return pl.pallas_call(
        _fox_kernel,
        grid=(bsz, n_heads),
        in_specs=[head_cols(0), head_cols(1), head_cols(2), head_cols(3),
                  pl.BlockSpec((1, 1, n_blocks, T), lambda b, h: (b, h, 0, 0))],
        out_specs=pl.BlockSpec((1, seq, HEAD_DIM), lambda b, h: (b, 0, h)),
        out_shape=jax.ShapeDtypeStruct((bsz, seq, n_heads * HEAD_DIM), BF16),
        scratch_shapes=[pltpu.VMEM((seq, 2 * HEAD_DIM), BF16),
                        pltpu.VMEM((n_blocks, HEAD_DIM + FOX_ONES_ROWS, T), BF16),
                        pltpu.VMEM((n_blocks, 2 * HEAD_DIM, T), BF16),
                        pltpu.VMEM((n_blocks, 1, T), F32),
                        pltpu.VMEM((n_blocks, HEAD_DIM + FOX_ONES_ROWS, T), F32),
                        pltpu.VMEM((T, T), F32), pltpu.VMEM((T, T), F32),
                        pltpu.VMEM((T, T), BF16), pltpu.VMEM((T, T), BF16),
                        pltpu.VMEM((1, T), F32), pltpu.VMEM((1, T), F32)],
        compiler_params=_params("parallel", "parallel"),
        name="fox_attention",
    )(proj, proj, proj, proj, c_rows)


def _ffn_up_kernel(x_ref, wg_ref, wv_ref, cg_ref, cv_ref, o_ref, ug_ref, uv_ref, wgb_ref, wvb_ref,
                   *, seq_steps, d_ff):
    rows = x_ref.shape[0]

    @pl.when(pl.program_id(1) == 0)
    def _():
        valid = d_ff - pl.program_id(0) * wg_ref.shape[1]
        col = lax.broadcasted_iota(jnp.int32, wg_ref.shape, 1)
        wgb_ref[...] = jnp.where(col < valid, wg_ref[...], 0.0).astype(BF16)
        wvb_ref[...] = jnp.where(col < valid, wv_ref[...], 0.0).astype(BF16)

    @pl.when(pl.program_id(1) % seq_steps == 0)
    def _():
        ug_ref[0:SUBLANES, :] = jnp.zeros((SUBLANES, ug_ref.shape[1]), F32)
        uv_ref[0:SUBLANES, :] = jnp.zeros((SUBLANES, uv_ref.shape[1]), F32)

    x = x_ref[...]

    def conv(wb_ref, c_ref, u_ref):
        u_ref[SUBLANES:SUBLANES + rows, :] = jnp.dot(x, wb_ref[...], preferred_element_type=F32)
        out = c_ref[CONV_WIDTH - 1:CONV_WIDTH, :] * u_ref[SUBLANES:SUBLANES + rows, :]
        for tap in range(CONV_WIDTH - 1):
            back = CONV_WIDTH - 1 - tap
            out = out + c_ref[tap:tap + 1, :] * u_ref[SUBLANES - back:SUBLANES - back + rows, :]
        carry = u_ref[rows:rows + SUBLANES, :]
        u_ref[0:SUBLANES, :] = carry
        return out

    gate = conv(wgb_ref, cg_ref, ug_ref)
    val = conv(wvb_ref, cv_ref, uv_ref)
    o_ref[...] = (_silu(gate) * val).astype(o_ref.dtype)


def _ffn_up(h, w_up, layer, conv_w, seq, d_ff, d_ff_pad):
    t, d = h.shape
    rows = min(MM_ROWS, seq)
    assert seq % rows == 0 and d_ff_pad % FFN_COLS == 0
    n_cols = d_ff_pad // FFN_COLS
    return pl.pallas_call(
        functools.partial(_ffn_up_kernel, seq_steps=seq // rows, d_ff=d_ff),
        grid=(n_cols, t // rows),
        in_specs=[pl.BlockSpec((rows, d), lambda j, i: (i, 0)),
                  pl.BlockSpec((None, d, FFN_COLS), lambda j, i: (layer, 0, j)),
                  pl.BlockSpec((None, pl.Element(d), pl.Element(FFN_COLS, (0, d_ff_pad - d_ff))),
                               lambda j, i: (layer, 0, pl.multiple_of(d_ff + j * FFN_COLS, LANES))),
                  pl.BlockSpec((CONV_WIDTH, FFN_COLS), lambda j, i: (0, j)),
                  pl.BlockSpec((CONV_WIDTH, FFN_COLS), lambda j, i: (0, j + n_cols))],
        out_specs=pl.BlockSpec((rows, FFN_COLS), lambda j, i: (i, j)),
        out_shape=jax.ShapeDtypeStruct((t, d_ff), BF16),
        scratch_shapes=[pltpu.VMEM((rows + SUBLANES, FFN_COLS), F32),
                        pltpu.VMEM((rows + SUBLANES, FFN_COLS), F32),
                        pltpu.VMEM((d, FFN_COLS), BF16),
                        pltpu.VMEM((d, FFN_COLS), BF16)],
        compiler_params=_params("parallel", "arbitrary"),
        name="ffn_up_conv",
    )(h, w_up, w_up, conv_w, conv_w)


def _pad_cols(w, width):
    return jnp.pad(w, ((0, 0), (0, width - w.shape[1])))


def kernel(x, attn_norm_w, ffn_norm_w, final_norm_w, even_w_in, hgrn_lb_logits, hgrn_norm_w, ret_norm_w,
           even_w_out, odd_w_in, odd_b_f, odd_w_out, ffn_w_up, ffn_conv_w, ffn_w_down):
    bsz, seq, d_model = x.shape
    depth = attn_norm_w.shape[0]
    tokens = bsz * seq
    d_hgrn = hgrn_norm_w.shape[1]
    d_fox = odd_w_out.shape[1]
    n_fox = odd_b_f.shape[1]
    d_ff = ffn_w_down.shape[1]
    d_ff_pad = FFN_COLS * math.ceil(d_ff / FFN_COLS)

    xf = x.reshape(tokens, d_model)
    h = _rmsnorm(xf, attn_norm_w[0], BF16)
    for layer in range(depth):
        j = layer // 2
        if layer % 2 == 0:
            proj = _matmul(h, even_w_in, j, F32, even_w_in.shape[2], rows=MM_ROWS, cols=IN_COLS,
                           name="even_in").reshape(bsz, seq, -1)
            o_a = _hgrn(proj, hgrn_lb_logits, hgrn_norm_w[j], j)
            o_b = _retention(proj, ret_norm_w[j], first_group=4 * d_hgrn // ret_norm_w.shape[1])
            mixed = [o_a.reshape(tokens, -1), o_b.reshape(tokens, -1)]
            w_out = even_w_out[j]
        else:
            proj = _matmul(h, odd_w_in, j, BF16, 4 * d_fox, rows=MM_ROWS, cols=IN_COLS,
                           name="fox_in").reshape(bsz, seq, -1)
            c = _forget_scan(h.reshape(bsz, seq, d_model), odd_w_in, j, 4 * d_fox,
                             _pad_cols(odd_b_f[j][None, :], LANES))
            mixed = [_fox(proj, c, n_fox).reshape(tokens, -1)]
            w_out = odd_w_out[j]
        xf, h = _matmul_residual_norm(mixed, w_out.astype(BF16), xf, ffn_norm_w[layer], BF16, rows=OUT_ROWS,
                                      name="mixer_out_norm")
        conv_w = ffn_conv_w[layer]
        conv_w = jnp.concatenate([_pad_cols(conv_w[:, :d_ff], d_ff_pad), _pad_cols(conv_w[:, d_ff:], d_ff_pad)],
                                 axis=1)
        act = _ffn_up(h, ffn_w_up, layer, conv_w, seq, d_ff, d_ff_pad)
        w_down = ffn_w_down[layer].astype(BF16)
        last = layer == depth - 1
        xf, h = _matmul_residual_norm([act], w_down, xf, final_norm_w if last else attn_norm_w[layer + 1],
                                      x.dtype if last else BF16, rows=DOWN_ROWS, keep_sum=not last,
                                      name="ffn_down_norm")
    return h.reshape(bsz, seq, d_model)
```

```python
import functools
import math

import jax
import jax.numpy as jnp
from jax import lax
from jax.experimental import pallas as pl
from jax.experimental.pallas import tpu as pltpu

F32 = jnp.float32
BF16 = jnp.bfloat16

HEAD_DIM = 128
ROPE_BASE = 10000.0
RET_DECAY_BASE = 5.0
EPS = 1e-6
LB_FLOOR = 1e-30
MASK_VALUE = -1e30
CONV_WIDTH = 3
LOG2_E = math.log2(math.e)

LANES = 128
SUBLANES = 8
VMEM_LIMIT_BYTES = 48 * 1024 * 1024

NORM_ROWS = 256
MM_ROWS = 1024
IN_COLS = 1024
OUT_ROWS = 512
DOWN_ROWS = 256
FFN_COLS = 512
MIX_ROWS = 512
HGRN_UNROLL = 4
RET_UNROLL = 2
HGRN_CHUNK = 64
HGRN_SUB = 16
HGRN_HEADS_PER_STEP = 8
RET_CHUNK = 128
RET_HEADS_PER_STEP = 4
FOX_BLOCK = 512
FOX_ONES_ROWS = 16
SCAN_ROWS = 256

NT_DIMS = (((1,), (1,)), ((), ()))
TN_DIMS = (((0,), (0,)), ((), ()))


def _params(*semantics):
    return pltpu.CompilerParams(dimension_semantics=semantics, vmem_limit_bytes=VMEM_LIMIT_BYTES)


def _sigmoid(x):
    return 1.0 / (1.0 + jnp.exp(-x))


def _silu(x):
    return x * _sigmoid(x)


def _log_sigmoid(x):
    return jnp.minimum(x, 0.0) - jnp.log1p(jnp.exp(-jnp.abs(x)))


def _lower_tri(n):
    rows = lax.broadcasted_iota(jnp.int32, (n, n), 0)
    cols = lax.broadcasted_iota(jnp.int32, (n, n), 1)
    return (rows >= cols).astype(F32)


def _rmsnorm_kernel(x_ref, w_ref, o_ref):
    x = x_ref[...]
    ms = jnp.mean(x * x, axis=-1, keepdims=True)
    o_ref[...] = (x * lax.rsqrt(ms + EPS) * w_ref[...]).astype(o_ref.dtype)


def _rmsnorm(x, w, out_dtype):
    t, d = x.shape
    return pl.pallas_call(
        _rmsnorm_kernel,
        grid=(t // NORM_ROWS,),
        in_specs=[pl.BlockSpec((NORM_ROWS, d), lambda i: (i, 0)),
                  pl.BlockSpec((1, d), lambda i: (0, 0))],
        out_specs=pl.BlockSpec((NORM_ROWS, d), lambda i: (i, 0)),
        out_shape=jax.ShapeDtypeStruct((t, d), out_dtype),
        compiler_params=_params("parallel"),
        name="rmsnorm",
    )(x, w.reshape(1, d))


def _cast_weight_tile(w_ref, wb_ref, valid_cols):
    w = w_ref[...]
    if valid_cols < w.shape[1]:
        col = lax.broadcasted_iota(jnp.int32, w.shape, 1)
        w = jnp.where(col < valid_cols, w, 0.0)
    wb_ref[...] = w.astype(BF16)


def _matmul_kernel(x_ref, w_ref, o_ref, wb_ref):
    @pl.when(pl.program_id(1) == 0)
    def _():
        _cast_weight_tile(w_ref, wb_ref, w_ref.shape[1])

    o_ref[...] = jnp.dot(x_ref[...], wb_ref[...], preferred_element_type=F32).astype(o_ref.dtype)


def _matmul(x, w, layer, out_dtype, n_out, rows, cols, name):
    t, k = x.shape
    rows = min(rows, t)
    assert t % rows == 0 and n_out % cols == 0 and n_out <= w.shape[2]
    return pl.pallas_call(
        _matmul_kernel,
        grid=(n_out // cols, t // rows),
        in_specs=[pl.BlockSpec((rows, k), lambda j, i: (i, 0)),
                  pl.BlockSpec((None, k, cols), lambda j, i: (layer, 0, j))],
        out_specs=pl.BlockSpec((rows, cols), lambda j, i: (i, j)),
        out_shape=jax.ShapeDtypeStruct((t, n_out), out_dtype),
        scratch_shapes=[pltpu.VMEM((k, cols), BF16)],
        compiler_params=_params("parallel", "arbitrary"),
        name=name,
    )(x, w)


def _residual_norm_kernel(*refs, n_parts, keep_sum):
    x_refs, (w_ref, r_ref, nw_ref), out_refs = refs[:n_parts], refs[n_parts:n_parts + 3], refs[n_parts + 3:]
    y = r_ref[...]
    k0 = 0
    for x_ref in x_refs:
        k1 = k0 + x_ref.shape[1]
        y = y + jnp.dot(x_ref[...], w_ref[k0:k1, :], preferred_element_type=F32)
        k0 = k1
    if keep_sum:
        out_refs[0][...] = y
    h_ref = out_refs[-1]
    ms = jnp.mean(y * y, axis=-1, keepdims=True)
    h_ref[...] = (y * lax.rsqrt(ms + EPS) * nw_ref[...]).astype(h_ref.dtype)


def _matmul_residual_norm(xs, w, residual, norm_w, h_dtype, rows, keep_sum=True, name="matmul_residual_norm"):
    t = xs[0].shape[0]
    k, n = w.shape
    assert t % rows == 0 and sum(x.shape[1] for x in xs) == k
    row_block = pl.BlockSpec((rows, n), lambda i: (i, 0))
    out_shape = [jax.ShapeDtypeStruct((t, n), h_dtype)]
    if keep_sum:
        out_shape.insert(0, jax.ShapeDtypeStruct((t, n), F32))
    outs = pl.pallas_call(
        functools.partial(_residual_norm_kernel, n_parts=len(xs), keep_sum=keep_sum),
        grid=(t // rows,),
        in_specs=[pl.BlockSpec((rows, x.shape[1]), lambda i: (i, 0)) for x in xs]
        + [pl.BlockSpec((k, n), lambda i: (0, 0), pipeline_mode=pl.Buffered(1)),
           row_block,
           pl.BlockSpec((1, n), lambda i: (0, 0))],
        out_specs=[row_block] * len(out_shape),
        out_shape=out_shape,
        compiler_params=_params("parallel"),
        name=name,
    )(*xs, w, residual, norm_w.reshape(1, n))
    return (outs[0], outs[1]) if keep_sum else (None, outs[0])


def _hgrn_kernel(q_ref, f_ref, i_ref, g_ref, lbl_ref, nw_ref, tri3_ref, sel_ref, mask_ref, o_ref, state_ref,
                 *, layer_j):
    C, L = HGRN_CHUNK, HGRN_SUB
    n_sub = C // L

    @pl.when(pl.program_id(2) == 0)
    def _():
        state_ref[...] = jnp.zeros_like(state_ref)

    lbl = lbl_ref[...]
    e = jnp.exp(lbl - jnp.max(lbl, axis=0, keepdims=True))
    sm = e / jnp.sum(e, axis=0, keepdims=True)
    lb_all = jnp.zeros((1, lbl.shape[1]), F32)
    for r in range(1, layer_j + 1):
        lb_all = lb_all + sm[r:r + 1, :]

    def pair_products(q, k, b, s):
        first = (s // SUBLANES) * SUBLANES
        q3, k3, b3 = (x.reshape(n_sub, L, HEAD_DIM) for x in (q, k, b))
        decay = jnp.exp2(jnp.minimum(b3[:, first:, :] - b3[:, s:s + 1, :], 0.0))
        prod = q3[:, first:, :] * k3[:, s:s + 1, :] * decay
        if first:
            prod = jnp.concatenate([jnp.zeros((n_sub, first, HEAD_DIM), F32), prod], axis=1)
        return prod.reshape(C, HEAD_DIM).astype(BF16)

    def head_chunk(rows, lanes, head):
        lb_floor, one_m_lb = jnp.maximum(lb_all[:, lanes], LB_FLOOR), 1.0 - lb_all[:, lanes]
        z = f_ref[0, rows, lanes]
        q = _silu(q_ref[0, rows, lanes])
        v = i_ref[0, rows, lanes].astype(BF16)

        e_z = jnp.exp(-jnp.abs(z))
        inv = 1.0 / (1.0 + e_z)
        log_f = jnp.log(lb_floor + one_m_lb * (jnp.where(z >= 0.0, 1.0, e_z) * inv))
        k = one_m_lb * (jnp.where(z >= 0.0, e_z, 1.0) * inv)
        hi = log_f.astype(BF16)
        rest = log_f - hi.astype(F32)
        mid = rest.astype(BF16)
        low = (rest - mid.astype(F32)).astype(BF16)
        b = jnp.dot(tri3_ref[...], jnp.concatenate([hi, mid, low], axis=0),
                    preferred_element_type=F32) * LOG2_E
        b_last = b[C - 1:C, :]

        state = state_ref[head]
        o = lax.dot_general((q * jnp.exp2(b)).astype(BF16), state.astype(BF16), NT_DIMS,
                            preferred_element_type=F32)

        below = [jnp.zeros((L, C), F32)]
        for i in range(1, n_sub):
            lo = i * L
            edge = b[lo - 1:lo, :]
            q_edge = (q[lo:lo + L, :] * jnp.exp2(b[lo:lo + L, :] - edge)).astype(BF16)
            k_edge = (k[:lo, :] * jnp.exp2(edge - b[:lo, :])).astype(BF16)
            k_edge = jnp.concatenate([k_edge, jnp.zeros((C - lo, HEAD_DIM), BF16)], axis=0)
            below.append(lax.dot_general(q_edge, k_edge, NT_DIMS, preferred_element_type=F32))
        pair = jnp.concatenate([pair_products(q, k, b, s) for s in range(L)], axis=1)
        diag = jnp.dot(pair, sel_ref[...], preferred_element_type=F32)
        scores = jnp.concatenate(below, axis=0) + diag * mask_ref[...]
        o = o + jnp.dot(scores.astype(BF16), v, preferred_element_type=F32)

        k_end = (k * jnp.exp2(b_last - b)).astype(BF16)
        state_ref[head] = state * jnp.exp2(b_last) + lax.dot_general(v, k_end, TN_DIMS,
                                                                     preferred_element_type=F32)

        y = o * lax.rsqrt(jnp.mean(o * o, axis=-1, keepdims=True) + EPS) * nw_ref[:, lanes]
        o_ref[0, rows, lanes] = (y * _silu(g_ref[0, rows, lanes])).astype(o_ref.dtype)

    def chunk(c, carry):
        rows = pl.ds(pl.multiple_of(c * C, C), C)
        for head in range(HGRN_HEADS_PER_STEP):
            head_chunk(rows, slice(head * HEAD_DIM, (head + 1) * HEAD_DIM), head)
        return carry

    lax.fori_loop(0, MIX_ROWS // C, chunk, 0, unroll=HGRN_UNROLL)


def _hgrn(proj, lb_logits, norm_w, layer_j):
    bsz, seq, _ = proj.shape
    d_h = norm_w.shape[0]
    width = HGRN_HEADS_PER_STEP * HEAD_DIM
    n_groups = d_h // width
    n_layers = lb_logits.shape[0]

    C, L = HGRN_CHUNK, HGRN_SUB
    pos = jnp.arange(C)
    causal = pos[:, None] >= pos[None, :]
    tri3 = jnp.tile(causal.astype(BF16), (1, 3))
    pair_sel = (jnp.repeat(jnp.arange(L), HEAD_DIM)[:, None] == (pos % L)[None, :]).astype(BF16)
    diag_mask = (causal & (pos[:, None] // L == pos[None, :] // L)).astype(F32)

    def col(group):
        return pl.BlockSpec((1, MIX_ROWS, width), lambda b, h, s: (b, s, group * n_groups + h))

    def whole(a):
        return pl.BlockSpec(a.shape, lambda b, h, s: (0, 0))

    return pl.pallas_call(
        functools.partial(_hgrn_kernel, layer_j=layer_j),
        grid=(bsz, n_groups, seq // MIX_ROWS),
        in_specs=[col(0), col(1), col(2), col(3),
                  pl.BlockSpec((n_layers, width), lambda b, h, s: (0, h)),
                  pl.BlockSpec((1, width), lambda b, h, s: (0, h)),
                  whole(tri3), whole(pair_sel), whole(diag_mask)],
        out_specs=pl.BlockSpec((1, MIX_ROWS, width), lambda b, h, s: (b, s, h)),
        out_shape=jax.ShapeDtypeStruct((bsz, seq, d_h), BF16),
        scratch_shapes=[pltpu.VMEM((HGRN_HEADS_PER_STEP, HEAD_DIM, HEAD_DIM), F32)],
        compiler_params=_params("parallel", "parallel", "arbitrary"),
        name="hgrn2",
    )(proj, proj, proj, proj, lb_logits, norm_w.reshape(1, d_h), tri3, pair_sel, diag_mask)


def _retention_kernel(q_ref, k_ref, v_ref, g_ref, cos_ref, sin_ref, intra_ref, edge_ref, nw_ref, o_ref, state_ref):
    C = RET_CHUNK

    @pl.when(pl.program_id(2) == 0)
    def _():
        state_ref[...] = jnp.zeros_like(state_ref)

    def rope(x, cos, sin_signed):
        return x * cos + pltpu.roll(x, HEAD_DIM // 2, 1) * sin_signed

    def head_chunk(sl, lanes, head):
        cos, sin_signed = cos_ref[sl, :], sin_ref[sl, :]
        q = rope(q_ref[0, sl, lanes], cos, sin_signed)
        k = rope(k_ref[0, sl, lanes], cos, sin_signed) * (HEAD_DIM ** -0.5)
        v = v_ref[0, sl, lanes].astype(BF16)
        state = state_ref[head]
        w_start, w_end, decay_chunk = edge_ref[head, 0], edge_ref[head, 1], edge_ref[head, 2, 0:1, :]

        scores = lax.dot_general(q.astype(BF16), k.astype(BF16), NT_DIMS,
                                 preferred_element_type=F32) * intra_ref[head]
        o = jnp.dot(scores.astype(BF16), v, preferred_element_type=F32)
        o = o + lax.dot_general((q * w_start).astype(BF16), state.astype(BF16), NT_DIMS,
                                preferred_element_type=F32)
        state_ref[head] = decay_chunk * state + lax.dot_general(v, (k * w_end).astype(BF16), TN_DIMS,
                                                                preferred_element_type=F32)

        cen = o - jnp.mean(o, axis=-1, keepdims=True)
        y = cen * lax.rsqrt(jnp.mean(cen * cen, axis=-1, keepdims=True) + EPS) * nw_ref[:, lanes]
        o_ref[0, sl, lanes] = (y * _silu(g_ref[0, sl, lanes])).astype(o_ref.dtype)

    def chunk(c, carry):
        sl = pl.ds(pl.multiple_of(c * C, C), C)
        for head in range(RET_HEADS_PER_STEP):
            head_chunk(sl, slice(head * HEAD_DIM, (head + 1) * HEAD_DIM), head)
        return carry

    lax.fori_loop(0, MIX_ROWS // C, chunk, 0, unroll=RET_UNROLL)


def _retention(proj, norm_w, first_group):
    bsz, seq, _ = proj.shape
    d_r = norm_w.shape[0]
    n_heads = d_r // HEAD_DIM
    half = HEAD_DIM // 2

    inv = ROPE_BASE ** (-jnp.arange(0, HEAD_DIM, 2, dtype=F32) / HEAD_DIM)
    ang = jnp.arange(seq).astype(F32)[:, None] * inv[None, :]
    cos = jnp.concatenate([jnp.cos(ang), jnp.cos(ang)], axis=-1)
    sin_signed = jnp.concatenate([-jnp.sin(ang), jnp.sin(ang)], axis=-1)
    assert cos.shape == (seq, 2 * half)
    C = RET_CHUNK
    log_gamma = jnp.log1p(-jnp.exp2(-RET_DECAY_BASE - jnp.arange(n_heads, dtype=F32)))
    pos = jnp.arange(C, dtype=F32)
    rel = pos[:, None] - pos[None, :]
    intra = jnp.where(rel >= 0, jnp.exp(log_gamma[:, None, None] * jnp.maximum(rel, 0.0)), 0.0)
    w_start = jnp.exp(log_gamma[:, None] * (pos + 1.0)[None, :])
    w_end = jnp.exp(log_gamma[:, None] * (C - 1.0 - pos)[None, :])
    decay_chunk = jnp.broadcast_to(jnp.exp(log_gamma * C)[:, None], (n_heads, C))
    edge = jnp.broadcast_to(jnp.stack([w_start, w_end, decay_chunk], axis=1)[..., None],
                            (n_heads, 3, C, HEAD_DIM))

    per_step = RET_HEADS_PER_STEP
    width = per_step * HEAD_DIM
    n_groups = n_heads // per_step
    first = first_group * n_groups

    def col(group):
        return pl.BlockSpec((1, MIX_ROWS, width), lambda b, h, s: (b, s, first + group * n_groups + h))

    table = pl.BlockSpec((MIX_ROWS, HEAD_DIM), lambda b, h, s: (s, 0))
    return pl.pallas_call(
        _retention_kernel,
        grid=(bsz, n_groups, seq // MIX_ROWS),
        in_specs=[col(0), col(1), col(2), col(3), table, table,
                  pl.BlockSpec((per_step, C, C), lambda b, h, s: (h, 0, 0)),
                  pl.BlockSpec((per_step, 3, C, HEAD_DIM), lambda b, h, s: (h, 0, 0, 0)),
                  pl.BlockSpec((1, width), lambda b, h, s: (0, h))],
        out_specs=pl.BlockSpec((1, MIX_ROWS, width), lambda b, h, s: (b, s, h)),
        out_shape=jax.ShapeDtypeStruct((bsz, seq, d_r), BF16),
        scratch_shapes=[pltpu.VMEM((per_step, HEAD_DIM, HEAD_DIM), F32)],
        compiler_params=_params("parallel", "parallel", "arbitrary"),
        name="retention",
    )(proj, proj, proj, proj, cos, sin_signed, intra, edge, norm_w.reshape(1, d_r))


def _forget_scan_kernel(h_ref, w_ref, bias_ref, c_ref, wb_ref, carry_ref, *, valid_cols):
    @pl.when(pl.program_id(1) == 0)
    def _():
        _cast_weight_tile(w_ref, wb_ref, valid_cols)
        carry_ref[...] = jnp.zeros_like(carry_ref)

    f = jnp.dot(h_ref[0], wb_ref[...], preferred_element_type=F32)
    log_f = _log_sigmoid(f + bias_ref[...])
    c = jnp.dot(_lower_tri(SCAN_ROWS), log_f, precision=lax.Precision.HIGHEST,
                preferred_element_type=F32) + carry_ref[...]
    c_ref[0] = c
    carry_ref[...] = c[SCAN_ROWS - 1:SCAN_ROWS, :]


def _forget_scan(h, w_in, layer, first_col, bias):
    bsz, seq, d = h.shape
    assert first_col % LANES == 0 and w_in.shape[2] - first_col <= LANES
    return pl.pallas_call(
        functools.partial(_forget_scan_kernel, valid_cols=w_in.shape[2] - first_col),
        grid=(bsz, seq // SCAN_ROWS),
        in_specs=[pl.BlockSpec((1, SCAN_ROWS, d), lambda b, s: (b, s, 0)),
                  pl.BlockSpec((None, d, LANES), lambda b, s: (layer, 0, first_col // LANES)),
                  pl.BlockSpec((1, LANES), lambda b, s: (0, 0))],
        out_specs=pl.BlockSpec((1, SCAN_ROWS, LANES), lambda b, s: (b, s, 0)),
        out_shape=jax.ShapeDtypeStruct((bsz, seq, LANES), F32),
        scratch_shapes=[pltpu.VMEM((d, LANES), BF16), pltpu.VMEM((1, LANES), F32)],
        compiler_params=_params("parallel", "arbitrary"),
        name="forget_scan",
    )(h, w_in, bias)


def _split3(x):
    hi = x.astype(BF16).astype(F32)
    rest = x - hi
    mid = rest.astype(BF16).astype(F32)
    low = (rest - mid).astype(BF16).astype(F32)
    return hi, mid, low


def _fox_kernel(q_ref, k_ref, v_ref, g_ref, ct_ref, o_ref,
                kaug_ref, vt_ref, qaug_ref, m_ref, acc_ref, s0_ref, s1_ref, p0_ref, p1_ref, a0_ref, a1_ref):
    T = FOX_BLOCK
    n_blocks = vt_ref.shape[0]
    aug_row = lax.broadcasted_iota(jnp.int32, (HEAD_DIM, T), 0)

    def bias_rows(c_row, for_queries):
        hi, mid, low = _split3(c_row)
        if for_queries:
            pieces = jnp.where(aug_row == 3, hi, jnp.where(aug_row == 4, mid, jnp.where(aug_row == 5, low, 0.0)))
            return jnp.where(aug_row < 3, 1.0, pieces)
        pieces = jnp.where(aug_row == 0, -hi, jnp.where(aug_row == 1, -mid, jnp.where(aug_row == 2, -low, 0.0)))
        return jnp.where((aug_row >= 3) & (aug_row < 6), 1.0, pieces)

    for j in range(n_blocks):
        rows = slice(j * T, (j + 1) * T)
        c_j = ct_ref[0, 0, j:j + 1, :] * LOG2_E
        kaug_ref[rows, 0:HEAD_DIM] = k_ref[0, rows, :]
        kaug_ref[rows, HEAD_DIM:2 * HEAD_DIM] = bias_rows(c_j, False).astype(BF16).T
        vt_ref[j, 0:HEAD_DIM, :] = v_ref[0, rows, :].T
        vt_ref[j, HEAD_DIM:, :] = jnp.ones((FOX_ONES_ROWS, T), BF16)
        qaug_ref[j, 0:HEAD_DIM, :] = (q_ref[0, rows, :].astype(F32) * (HEAD_DIM ** -0.5 * LOG2_E)).astype(BF16).T
        qaug_ref[j, HEAD_DIM:2 * HEAD_DIM, :] = bias_rows(c_j, True).astype(BF16)
    key_id = lax.broadcasted_iota(jnp.int32, (T, T), 0)
    query_id = lax.broadcasted_iota(jnp.int32, (T, T), 1)

    def form_logits(qb, kb, s_ref):
        s_ref[...] = jnp.dot(kaug_ref[kb * T:(kb + 1) * T, :], qaug_ref[qb],
                             preferred_element_type=F32)

    def softmax_update(qb, kb, s_ref, p_ref, a_ref):
        s = s_ref[...]
        if qb == kb:
            s = jnp.where(key_id <= query_id, s, MASK_VALUE)
            m_new = jnp.max(s, axis=0, keepdims=True)
        else:
            m = m_ref[qb]
            m_new = jnp.maximum(m, jnp.max(s, axis=0, keepdims=True))
            a_ref[...] = jnp.exp2(m - m_new)
        p_ref[...] = jnp.exp2(s - m_new).astype(BF16)
        m_ref[qb] = m_new

    def accumulate(qb, kb, p_ref, a_ref):
        update = jnp.dot(vt_ref[kb], p_ref[...], preferred_element_type=F32)
        acc_ref[qb] = update if qb == kb else a_ref[...] * acc_ref[qb] + update

    pairs = [(qb, qb) for qb in range(n_blocks)] + [(qb, kb) for qb in range(n_blocks) for kb in range(qb)]
    buffers = ((s0_ref, p0_ref, a0_ref), (s1_ref, p1_ref, a1_ref))
    form_logits(*pairs[0], s0_ref)
    for n in range(len(pairs) + 1):
        s_now, p_now, a_now = buffers[n % 2]
        s_next, p_prev, a_prev = buffers[(n + 1) % 2]
        if n + 1 < len(pairs):
            form_logits(*pairs[n + 1], s_next)
        if n < len(pairs):
            softmax_update(*pairs[n], s_now, p_now, a_now)
        if n >= 1:
            accumulate(*pairs[n - 1], p_prev, a_prev)

    for j in range(n_blocks):
        rows = slice(j * T, (j + 1) * T)
        acc = acc_ref[j]
        o = (acc[:HEAD_DIM, :] / acc[HEAD_DIM:HEAD_DIM + 1, :]).T
        o_ref[0, rows, :] = (o * _sigmoid(g_ref[0, rows, :].astype(F32))).astype(o_ref.dtype)


def _fox(proj, c, n_heads):
    bsz, seq, _ = proj.shape
    T = FOX_BLOCK
    n_blocks = seq // T
    c_rows = jnp.transpose(c[:, :, :n_heads], (0, 2, 1)).reshape(bsz, n_heads, n_blocks, T)

    def head_cols(group):
        return pl.BlockSpec((1, seq, HEAD_DIM), lambda b, h: (b, 0, group * n_heads + h))

    return pl.pallas_call(
        _fox_kernel,
        grid=(bsz, n_heads),
        in_specs=[head_cols(0), head_cols(1), head_cols(2), head_cols(3),
                  pl.BlockSpec((1, 1, n_blocks, T), lambda b, h: (b, h, 0, 0))],
        out_specs=pl.BlockSpec((1, seq, HEAD_DIM), lambda b, h: (b, 0, h)),
        out_shape=jax.ShapeDtypeStruct((bsz, seq, n_heads * HEAD_DIM), BF16),
        scratch_shapes=[pltpu.VMEM((seq, 2 * HEAD_DIM), BF16),
                        pltpu.VMEM((n_blocks, HEAD_DIM + FOX_ONES_ROWS, T), BF16),
                        pltpu.VMEM((n_blocks, 2 * HEAD_DIM, T), BF16),
                        pltpu.VMEM((n_blocks, 1, T), F32),
                        pltpu.VMEM((n_blocks, HEAD_DIM + FOX_ONES_ROWS, T), F32),
                        pltpu.VMEM((T, T), F32), pltpu.VMEM((T, T), F32),
                        pltpu.VMEM((T, T), BF16), pltpu.VMEM((T, T), BF16),
                        pltpu.VMEM((1, T), F32), pltpu.VMEM((1, T), F32)],
        compiler_params=_params("parallel", "parallel"),
        name="fox_attention",
    )(proj, proj, proj, proj, c_rows)


def _ffn_up_kernel(x_ref, wg_ref, wv_ref, cg_ref, cv_ref, o_ref, ug_ref, uv_ref, wgb_ref, wvb_ref,
                   *, seq_steps, d_ff):
    rows = x_ref.shape[0]

    @pl.when(pl.program_id(1) == 0)
    def _():
        valid = d_ff - pl.program_id(0) * wg_ref.shape[1]
        col = lax.broadcasted_iota(jnp.int32, wg_ref.shape, 1)
        wgb_ref[...] = jnp.where(col < valid, wg_ref[...], 0.0).astype(BF16)
        wvb_ref[...] = jnp.where(col < valid, wv_ref[...], 0.0).astype(BF16)

    @pl.when(pl.program_id(1) % seq_steps == 0)
    def _():
        ug_ref[0:SUBLANES, :] = jnp.zeros((SUBLANES, ug_ref.shape[1]), F32)
        uv_ref[0:SUBLANES, :] = jnp.zeros((SUBLANES, uv_ref.shape[1]), F32)

    x = x_ref[...]

    def conv(wb_ref, c_ref, u_ref):
        u_ref[SUBLANES:SUBLANES + rows, :] = jnp.dot(x, wb_ref[...], preferred_element_type=F32)
        out = c_ref[CONV_WIDTH - 1:CONV_WIDTH, :] * u_ref[SUBLANES:SUBLANES + rows, :]
        for tap in range(CONV_WIDTH - 1):
            back = CONV_WIDTH - 1 - tap
            out = out + c_ref[tap:tap + 1, :] * u_ref[SUBLANES - back:SUBLANES - back + rows, :]
        carry = u_ref[rows:rows + SUBLANES, :]
        u_ref[0:SUBLANES, :] = carry
        return out

    gate = conv(wgb_ref, cg_ref, ug_ref)
    val = conv(wvb_ref, cv_ref, uv_ref)
    o_ref[...] = (_silu(gate) * val).astype(o_ref.dtype)


def _ffn_up(h, w_up, layer, conv_w, seq, d_ff, d_ff_pad):
    t, d = h.shape
    rows = min(MM_ROWS, seq)
    assert seq % rows == 0 and d_ff_pad % FFN_COLS == 0
    n_cols = d_ff_pad // FFN_COLS
    return pl.pallas_call(
        functools.partial(_ffn_up_kernel, seq_steps=seq // rows, d_ff=d_ff),
        grid=(n_cols, t // rows),
        in_specs=[pl.BlockSpec((rows, d), lambda j, i: (i, 0)),
                  pl.BlockSpec((None, d, FFN_COLS), lambda j, i: (layer, 0, j)),
                  pl.BlockSpec((None, pl.Element(d), pl.Element(FFN_COLS, (0, d_ff_pad - d_ff))),
                               lambda j, i: (layer, 0, pl.multiple_of(d_ff + j * FFN_COLS, LANES))),
                  pl.BlockSpec((CONV_WIDTH, FFN_COLS), lambda j, i: (0, j)),
                  pl.BlockSpec((CONV_WIDTH, FFN_COLS), lambda j, i: (0, j + n_cols))],
        out_specs=pl.BlockSpec((rows, FFN_COLS), lambda j, i: (i, j)),
        out_shape=jax.ShapeDtypeStruct((t, d_ff), BF16),
        scratch_shapes=[pltpu.VMEM((rows + SUBLANES, FFN_COLS), F32),
                        pltpu.VMEM((rows + SUBLANES, FFN_COLS), F32),
                        pltpu.VMEM((d, FFN_COLS), BF16),
                        pltpu.VMEM((d, FFN_COLS), BF16)],
        compiler_params=_params("parallel", "arbitrary"),
        name="ffn_up_conv",
    )(h, w_up, w_up, conv_w, conv_w)


def _pad_cols(w, width):
    return jnp.pad(w, ((0, 0), (0, width - w.shape[1])))


def kernel(x, attn_norm_w, ffn_norm_w, final_norm_w, even_w_in, hgrn_lb_logits, hgrn_norm_w, ret_norm_w,
           even_w_out, odd_w_in, odd_b_f, odd_w_out, ffn_w_up, ffn_conv_w, ffn_w_down):
    bsz, seq, d_model = x.shape
    depth = attn_norm_w.shape[0]
    tokens = bsz * seq
    d_hgrn = hgrn_norm_w.shape[1]
    d_fox = odd_w_out.shape[1]
    n_fox = odd_b_f.shape[1]
    d_ff = ffn_w_down.shape[1]
    d_ff_pad = FFN_COLS * math.ceil(d_ff / FFN_COLS)

    xf = x.reshape(tokens, d_model)
    h = _rmsnorm(xf, attn_norm_w[0], BF16)
    for layer in range(depth):
        j = layer // 2
        if layer % 2 == 0:
            proj = _matmul(h, even_w_in, j, F32, even_w_in.shape[2], rows=MM_ROWS, cols=IN_COLS,
                           name="even_in").reshape(bsz, seq, -1)
            o_a = _hgrn(proj, hgrn_lb_logits, hgrn_norm_w[j], j)
            o_b = _retention(proj, ret_norm_w[j], first_group=4 * d_hgrn // ret_norm_w.shape[1])
            mixed = [o_a.reshape(tokens, -1), o_b.reshape(tokens, -1)]
            w_out = even_w_out[j]
        else:
            proj = _matmul(h, odd_w_in, j, BF16, 4 * d_fox, rows=MM_ROWS, cols=IN_COLS,
                           name="fox_in").reshape(bsz, seq, -1)
            c = _forget_scan(h.reshape(bsz, seq, d_model), odd_w_in, j, 4 * d_fox,
                             _pad_cols(odd_b_f[j][None, :], LANES))
            mixed = [_fox(proj, c, n_fox).reshape(tokens, -1)]
            w_out = odd_w_out[j]
        xf, h = _matmul_residual_norm(mixed, w_out.astype(BF16), xf, ffn_norm_w[layer], BF16, rows=OUT_ROWS,
                                      name="mixer_out_norm")
        conv_w = ffn_conv_w[layer]
        conv_w = jnp.concatenate([_pad_cols(conv_w[:, :d_ff], d_ff_pad), _pad_cols(conv_w[:, d_ff:], d_ff_pad)],
                                 axis=1)
        act = _ffn_up(h, ffn_w_up, layer, conv_w, seq, d_ff, d_ff_pad)
        w_down = ffn_w_down[layer].astype(BF16)
        last = layer == depth - 1
        xf, h = _matmul_residual_norm([act], w_down, xf, final_norm_w if last else attn_norm_w[layer + 1],
                                      x.dtype if last else BF16, rows=DOWN_ROWS, keep_sum=not last,
                                      name="ffn_down_norm")
    return h.reshape(bsz, seq, d_model)
```

```python
import functools
import math

import jax
import jax.numpy as jnp
from jax import lax
from jax.experimental import pallas as pl
from jax.experimental.pallas import tpu as pltpu

F32 = jnp.float32
BF16 = jnp.bfloat16

HEAD_DIM = 128
ROPE_BASE = 10000.0
RET_DECAY_BASE = 5.0
EPS = 1e-6
LB_FLOOR = 1e-30
MASK_VALUE = -1e30
CONV_WIDTH = 3
LOG2_E = math.log2(math.e)

LANES = 128
SUBLANES = 8
VMEM_LIMIT_BYTES = 48 * 1024 * 1024

NORM_ROWS = 256
MM_ROWS = 1024
IN_COLS = 1024
OUT_ROWS = 512
DOWN_ROWS = 256
FFN_COLS = 512
MIX_ROWS = 512
HGRN_UNROLL = 8
RET_UNROLL = 2
HGRN_CHUNK = 64
HGRN_SUB = 16
HGRN_HEADS_PER_STEP = 8
RET_CHUNK = 128
RET_HEADS_PER_STEP = 4
FOX_BLOCK = 512
FOX_ONES_ROWS = 16
SCAN_ROWS = 256

NT_DIMS = (((1,), (1,)), ((), ()))
TN_DIMS = (((0,), (0,)), ((), ()))


def _params(*semantics):
    return pltpu.CompilerParams(dimension_semantics=semantics, vmem_limit_bytes=VMEM_LIMIT_BYTES)


def _sigmoid(x):
    return 1.0 / (1.0 + jnp.exp(-x))


def _silu(x):
    return x * _sigmoid(x)


def _log_sigmoid(x):
    return jnp.minimum(x, 0.0) - jnp.log1p(jnp.exp(-jnp.abs(x)))


def _lower_tri(n):
    rows = lax.broadcasted_iota(jnp.int32, (n, n), 0)
    cols = lax.broadcasted_iota(jnp.int32, (n, n), 1)
    return (rows >= cols).astype(F32)


def _rmsnorm_kernel(x_ref, w_ref, o_ref):
    x = x_ref[...]
    ms = jnp.mean(x * x, axis=-1, keepdims=True)
    o_ref[...] = (x * lax.rsqrt(ms + EPS) * w_ref[...]).astype(o_ref.dtype)


def _rmsnorm(x, w, out_dtype):
    t, d = x.shape
    return pl.pallas_call(
        _rmsnorm_kernel,
        grid=(t // NORM_ROWS,),
        in_specs=[pl.BlockSpec((NORM_ROWS, d), lambda i: (i, 0)),
                  pl.BlockSpec((1, d), lambda i: (0, 0))],
        out_specs=pl.BlockSpec((NORM_ROWS, d), lambda i: (i, 0)),
        out_shape=jax.ShapeDtypeStruct((t, d), out_dtype),
        compiler_params=_params("parallel"),
        name="rmsnorm",
    )(x, w.reshape(1, d))


def _cast_weight_tile(w_ref, wb_ref, valid_cols):
    w = w_ref[...]
    if valid_cols < w.shape[1]:
        col = lax.broadcasted_iota(jnp.int32, w.shape, 1)
        w = jnp.where(col < valid_cols, w, 0.0)
    wb_ref[...] = w.astype(BF16)


def _matmul_kernel(x_ref, w_ref, o_ref, wb_ref):
    @pl.when(pl.program_id(1) == 0)
    def _():
        _cast_weight_tile(w_ref, wb_ref, w_ref.shape[1])

    o_ref[...] = jnp.dot(x_ref[...], wb_ref[...], preferred_element_type=F32).astype(o_ref.dtype)


def _matmul(x, w, layer, out_dtype, n_out, rows, cols, name):
    t, k = x.shape
    rows = min(rows, t)
    assert t % rows == 0 and n_out % cols == 0 and n_out <= w.shape[2]
    return pl.pallas_call(
        _matmul_kernel,
        grid=(n_out // cols, t // rows),
        in_specs=[pl.BlockSpec((rows, k), lambda j, i: (i, 0)),
                  pl.BlockSpec((None, k, cols), lambda j, i: (layer, 0, j))],
        out_specs=pl.BlockSpec((rows, cols), lambda j, i: (i, j)),
        out_shape=jax.ShapeDtypeStruct((t, n_out), out_dtype),
        scratch_shapes=[pltpu.VMEM((k, cols), BF16)],
        compiler_params=_params("parallel", "arbitrary"),
        name=name,
    )(x, w)


def _residual_norm_kernel(*refs, n_parts, keep_sum):
    x_refs, (w_ref, r_ref, nw_ref), out_refs = refs[:n_parts], refs[n_parts:n_parts + 3], refs[n_parts + 3:]
    y = r_ref[...]
    k0 = 0
    for x_ref in x_refs:
        k1 = k0 + x_ref.shape[1]
        y = y + jnp.dot(x_ref[...], w_ref[k0:k1, :], preferred_element_type=F32)
        k0 = k1
    if keep_sum:
        out_refs[0][...] = y
    h_ref = out_refs[-1]
    ms = jnp.mean(y * y, axis=-1, keepdims=True)
    h_ref[...] = (y * lax.rsqrt(ms + EPS) * nw_ref[...]).astype(h_ref.dtype)


def _matmul_residual_norm(xs, w, residual, norm_w, h_dtype, rows, keep_sum=True, name="matmul_residual_norm"):
    t = xs[0].shape[0]
    k, n = w.shape
    assert t % rows == 0 and sum(x.shape[1] for x in xs) == k
    row_block = pl.BlockSpec((rows, n), lambda i: (i, 0))
    out_shape = [jax.ShapeDtypeStruct((t, n), h_dtype)]
    if keep_sum:
        out_shape.insert(0, jax.ShapeDtypeStruct((t, n), F32))
    outs = pl.pallas_call(
        functools.partial(_residual_norm_kernel, n_parts=len(xs), keep_sum=keep_sum),
        grid=(t // rows,),
        in_specs=[pl.BlockSpec((rows, x.shape[1]), lambda i: (i, 0)) for x in xs]
        + [pl.BlockSpec((k, n), lambda i: (0, 0), pipeline_mode=pl.Buffered(1)),
           row_block,
           pl.BlockSpec((1, n), lambda i: (0, 0))],
        out_specs=[row_block] * len(out_shape),
        out_shape=out_shape,
        compiler_params=_params("parallel"),
        name=name,
    )(*xs, w, residual, norm_w.reshape(1, n))
    return (outs[0], outs[1]) if keep_sum else (None, outs[0])


def _hgrn_kernel(q_ref, f_ref, i_ref, g_ref, lbl_ref, nw_ref, tri3_ref, sel_ref, mask_ref, o_ref, state_ref,
                 *, layer_j):
    C, L = HGRN_CHUNK, HGRN_SUB
    n_sub = C // L

    @pl.when(pl.program_id(2) == 0)
    def _():
        state_ref[...] = jnp.zeros_like(state_ref)

    lbl = lbl_ref[...]
    e = jnp.exp(lbl - jnp.max(lbl, axis=0, keepdims=True))
    sm = e / jnp.sum(e, axis=0, keepdims=True)
    lb_all = jnp.zeros((1, lbl.shape[1]), F32)
    for r in range(1, layer_j + 1):
        lb_all = lb_all + sm[r:r + 1, :]

    def pair_products(q, k, b, s):
        first = (s // SUBLANES) * SUBLANES
        q3, k3, b3 = (x.reshape(n_sub, L, HEAD_DIM) for x in (q, k, b))
        decay = jnp.exp2(jnp.minimum(b3[:, first:, :] - b3[:, s:s + 1, :], 0.0))
        prod = q3[:, first:, :] * k3[:, s:s + 1, :] * decay
        if first:
            prod = jnp.concatenate([jnp.zeros((n_sub, first, HEAD_DIM), F32), prod], axis=1)
        return prod.reshape(C, HEAD_DIM).astype(BF16)

    def head_chunk(rows, lanes, head):
        lb_floor, one_m_lb = jnp.maximum(lb_all[:, lanes], LB_FLOOR), 1.0 - lb_all[:, lanes]
        z = f_ref[0, rows, lanes]
        q = _silu(q_ref[0, rows, lanes])
        v = i_ref[0, rows, lanes].astype(BF16)

        e_z = jnp.exp(-jnp.abs(z))
        inv = 1.0 / (1.0 + e_z)
        log_f = jnp.log(lb_floor + one_m_lb * (jnp.where(z >= 0.0, 1.0, e_z) * inv))
        k = one_m_lb * (jnp.where(z >= 0.0, e_z, 1.0) * inv)
        hi = log_f.astype(BF16)
        rest = log_f - hi.astype(F32)
        mid = rest.astype(BF16)
        low = (rest - mid.astype(F32)).astype(BF16)
        b = jnp.dot(tri3_ref[...], jnp.concatenate([hi, mid, low], axis=0),
                    preferred_element_type=F32) * LOG2_E
        b_last = b[C - 1:C, :]

        state = state_ref[head]
        o = lax.dot_general((q * jnp.exp2(b)).astype(BF16), state.astype(BF16), NT_DIMS,
                            preferred_element_type=F32)

        below = [jnp.zeros((L, C), F32)]
        for i in range(1, n_sub):
            lo = i * L
            edge = b[lo - 1:lo, :]
            q_edge = (q[lo:lo + L, :] * jnp.exp2(b[lo:lo + L, :] - edge)).astype(BF16)
            k_edge = (k[:lo, :] * jnp.exp2(edge - b[:lo, :])).astype(BF16)
            k_edge = jnp.concatenate([k_edge, jnp.zeros((C - lo, HEAD_DIM), BF16)], axis=0)
            below.append(lax.dot_general(q_edge, k_edge, NT_DIMS, preferred_element_type=F32))
        pair = jnp.concatenate([pair_products(q, k, b, s) for s in range(L)], axis=1)
        diag = jnp.dot(pair, sel_ref[...], preferred_element_type=F32)
        scores = jnp.concatenate(below, axis=0) + diag * mask_ref[...]
        o = o + jnp.dot(scores.astype(BF16), v, preferred_element_type=F32)

        k_end = (k * jnp.exp2(b_last - b)).astype(BF16)
        state_ref[head] = state * jnp.exp2(b_last) + lax.dot_general(v, k_end, TN_DIMS,
                                                                     preferred_element_type=F32)

        y = o * lax.rsqrt(jnp.mean(o * o, axis=-1, keepdims=True) + EPS) * nw_ref[:, lanes]
        o_ref[0, rows, lanes] = (y * _silu(g_ref[0, rows, lanes])).astype(o_ref.dtype)

    def chunk(c, carry):
        rows = pl.ds(pl.multiple_of(c * C, C), C)
        for head in range(HGRN_HEADS_PER_STEP):
            head_chunk(rows, slice(head * HEAD_DIM, (head + 1) * HEAD_DIM), head)
        return carry

    lax.fori_loop(0, MIX_ROWS // C, chunk, 0, unroll=HGRN_UNROLL)


def _hgrn(proj, lb_logits, norm_w, layer_j):
    bsz, seq, _ = proj.shape
    d_h = norm_w.shape[0]
    width = HGRN_HEADS_PER_STEP * HEAD_DIM
    n_groups = d_h // width
    n_layers = lb_logits.shape[0]

    C, L = HGRN_CHUNK, HGRN_SUB
    pos = jnp.arange(C)
    causal = pos[:, None] >= pos[None, :]
    tri3 = jnp.tile(causal.astype(BF16), (1, 3))
    pair_sel = (jnp.repeat(jnp.arange(L), HEAD_DIM)[:, None] == (pos % L)[None, :]).astype(BF16)
    diag_mask = (causal & (pos[:, None] // L == pos[None, :] // L)).astype(F32)

    def col(group):
        return pl.BlockSpec((1, MIX_ROWS, width), lambda b, h, s: (b, s, group * n_groups + h))

    def whole(a):
        return pl.BlockSpec(a.shape, lambda b, h, s: (0, 0))

    return pl.pallas_call(
        functools.partial(_hgrn_kernel, layer_j=layer_j),
        grid=(bsz, n_groups, seq // MIX_ROWS),
        in_specs=[col(0), col(1), col(2), col(3),
                  pl.BlockSpec((n_layers, width), lambda b, h, s: (0, h)),
                  pl.BlockSpec((1, width), lambda b, h, s: (0, h)),
                  whole(tri3), whole(pair_sel), whole(diag_mask)],
        out_specs=pl.BlockSpec((1, MIX_ROWS, width), lambda b, h, s: (b, s, h)),
        out_shape=jax.ShapeDtypeStruct((bsz, seq, d_h), BF16),
        scratch_shapes=[pltpu.VMEM((HGRN_HEADS_PER_STEP, HEAD_DIM, HEAD_DIM), F32)],
        compiler_params=_params("parallel", "parallel", "arbitrary"),
        name="hgrn2",
    )(proj, proj, proj, proj, lb_logits, norm_w.reshape(1, d_h), tri3, pair_sel, diag_mask)


def _retention_kernel(q_ref, k_ref, v_ref, g_ref, cos_ref, sin_ref, intra_ref, edge_ref, nw_ref, o_ref, state_ref):
    C = RET_CHUNK

    @pl.when(pl.program_id(2) == 0)
    def _():
        state_ref[...] = jnp.zeros_like(state_ref)

    def rope(x, cos, sin_signed):
        return x * cos + pltpu.roll(x, HEAD_DIM // 2, 1) * sin_signed

    def head_chunk(sl, lanes, head):
        cos, sin_signed = cos_ref[sl, :], sin_ref[sl, :]
        q = rope(q_ref[0, sl, lanes], cos, sin_signed)
        k = rope(k_ref[0, sl, lanes], cos, sin_signed) * (HEAD_DIM ** -0.5)
        v = v_ref[0, sl, lanes].astype(BF16)
        state = state_ref[head]
        w_start, w_end, decay_chunk = edge_ref[head, 0], edge_ref[head, 1], edge_ref[head, 2, 0:1, :]

        scores = lax.dot_general(q.astype(BF16), k.astype(BF16), NT_DIMS,
                                 preferred_element_type=F32) * intra_ref[head]
        o = jnp.dot(scores.astype(BF16), v, preferred_element_type=F32)
        o = o + lax.dot_general((q * w_start).astype(BF16), state.astype(BF16), NT_DIMS,
                                preferred_element_type=F32)
        state_ref[head] = decay_chunk * state + lax.dot_general(v, (k * w_end).astype(BF16), TN_DIMS,
                                                                preferred_element_type=F32)

        cen = o - jnp.mean(o, axis=-1, keepdims=True)
        y = cen * lax.rsqrt(jnp.mean(cen * cen, axis=-1, keepdims=True) + EPS) * nw_ref[:, lanes]
        o_ref[0, sl, lanes] = (y * _silu(g_ref[0, sl, lanes])).astype(o_ref.dtype)

    def chunk(c, carry):
        sl = pl.ds(pl.multiple_of(c * C, C), C)
        for head in range(RET_HEADS_PER_STEP):
            head_chunk(sl, slice(head * HEAD_DIM, (head + 1) * HEAD_DIM), head)
        return carry

    lax.fori_loop(0, MIX_ROWS // C, chunk, 0, unroll=RET_UNROLL)


def _retention(proj, norm_w, first_group):
    bsz, seq, _ = proj.shape
    d_r = norm_w.shape[0]
    n_heads = d_r // HEAD_DIM
    half = HEAD_DIM // 2

    inv = ROPE_BASE ** (-jnp.arange(0, HEAD_DIM, 2, dtype=F32) / HEAD_DIM)
    ang = jnp.arange(seq).astype(F32)[:, None] * inv[None, :]
    cos = jnp.concatenate([jnp.cos(ang), jnp.cos(ang)], axis=-1)
    sin_signed = jnp.concatenate([-jnp.sin(ang), jnp.sin(ang)], axis=-1)
    assert cos.shape == (seq, 2 * half)
    C = RET_CHUNK
    log_gamma = jnp.log1p(-jnp.exp2(-RET_DECAY_BASE - jnp.arange(n_heads, dtype=F32)))
    pos = jnp.arange(C, dtype=F32)
    rel = pos[:, None] - pos[None, :]
    intra = jnp.where(rel >= 0, jnp.exp(log_gamma[:, None, None] * jnp.maximum(rel, 0.0)), 0.0)
    w_start = jnp.exp(log_gamma[:, None] * (pos + 1.0)[None, :])
    w_end = jnp.exp(log_gamma[:, None] * (C - 1.0 - pos)[None, :])
    decay_chunk = jnp.broadcast_to(jnp.exp(log_gamma * C)[:, None], (n_heads, C))
    edge = jnp.broadcast_to(jnp.stack([w_start, w_end, decay_chunk], axis=1)[..., None],
                            (n_heads, 3, C, HEAD_DIM))

    per_step = RET_HEADS_PER_STEP
    width = per_step * HEAD_DIM
    n_groups = n_heads // per_step
    first = first_group * n_groups

    def col(group):
        return pl.BlockSpec((1, MIX_ROWS, width), lambda b, h, s: (b, s, first + group * n_groups + h))

    table = pl.BlockSpec((MIX_ROWS, HEAD_DIM), lambda b, h, s: (s, 0))
    return pl.pallas_call(
        _retention_kernel,
        grid=(bsz, n_groups, seq // MIX_ROWS),
        in_specs=[col(0), col(1), col(2), col(3), table, table,
                  pl.BlockSpec((per_step, C, C), lambda b, h, s: (h, 0, 0)),
                  pl.BlockSpec((per_step, 3, C, HEAD_DIM), lambda b, h, s: (h, 0, 0, 0)),
                  pl.BlockSpec((1, width), lambda b, h, s: (0, h))],
        out_specs=pl.BlockSpec((1, MIX_ROWS, width), lambda b, h, s: (b, s, h)),
        out_shape=jax.ShapeDtypeStruct((bsz, seq, d_r), BF16),
        scratch_shapes=[pltpu.VMEM((per_step, HEAD_DIM, HEAD_DIM), F32)],
        compiler_params=_params("parallel", "parallel", "arbitrary"),
        name="retention",
    )(proj, proj, proj, proj, cos, sin_signed, intra, edge, norm_w.reshape(1, d_r))


def _forget_scan_kernel(h_ref, w_ref, bias_ref, c_ref, wb_ref, carry_ref, *, valid_cols):
    @pl.when(pl.program_id(1) == 0)
    def _():
        _cast_weight_tile(w_ref, wb_ref, valid_cols)
        carry_ref[...] = jnp.zeros_like(carry_ref)

    f = jnp.dot(h_ref[0], wb_ref[...], preferred_element_type=F32)
    log_f = _log_sigmoid(f + bias_ref[...])
    c = jnp.dot(_lower_tri(SCAN_ROWS), log_f, precision=lax.Precision.HIGHEST,
                preferred_element_type=F32) + carry_ref[...]
    c_ref[0] = c
    carry_ref[...] = c[SCAN_ROWS - 1:SCAN_ROWS, :]


def _forget_scan(h, w_in, layer, first_col, bias):
    bsz, seq, d = h.shape
    assert first_col % LANES == 0 and w_in.shape[2] - first_col <= LANES
    return pl.pallas_call(
        functools.partial(_forget_scan_kernel, valid_cols=w_in.shape[2] - first_col),
        grid=(bsz, seq // SCAN_ROWS),
        in_specs=[pl.BlockSpec((1, SCAN_ROWS, d), lambda b, s: (b, s, 0)),
                  pl.BlockSpec((None, d, LANES), lambda b, s: (layer, 0, first_col // LANES)),
                  pl.BlockSpec((1, LANES), lambda b, s: (0, 0))],
        out_specs=pl.BlockSpec((1, SCAN_ROWS, LANES), lambda b, s: (b, s, 0)),
        out_shape=jax.ShapeDtypeStruct((bsz, seq, LANES), F32),
        scratch_shapes=[pltpu.VMEM((d, LANES), BF16), pltpu.VMEM((1, LANES), F32)],
        compiler_params=_params("parallel", "arbitrary"),
        name="forget_scan",
    )(h, w_in, bias)


def _split3(x):
    hi = x.astype(BF16).astype(F32)
    rest = x - hi
    mid = rest.astype(BF16).astype(F32)
    low = (rest - mid).astype(BF16).astype(F32)
    return hi, mid, low


def _fox_kernel(q_ref, k_ref, v_ref, g_ref, ct_ref, o_ref,
                kaug_ref, vt_ref, qaug_ref, m_ref, acc_ref, s0_ref, s1_ref, p0_ref, p1_ref, a0_ref, a1_ref):
    T = FOX_BLOCK
    n_blocks = vt_ref.shape[0]
    aug_row = lax.broadcasted_iota(jnp.int32, (HEAD_DIM, T), 0)

    def bias_rows(c_row, for_queries):
        hi, mid, low = _split3(c_row)
        if for_queries:
            pieces = jnp.where(aug_row == 3, hi, jnp.where(aug_row == 4, mid, jnp.where(aug_row == 5, low, 0.0)))
            return jnp.where(aug_row < 3, 1.0, pieces)
        pieces = jnp.where(aug_row == 0, -hi, jnp.where(aug_row == 1, -mid, jnp.where(aug_row == 2, -low, 0.0)))
        return jnp.where((aug_row >= 3) & (aug_row < 6), 1.0, pieces)

    for j in range(n_blocks):
        rows = slice(j * T, (j + 1) * T)
        c_j = ct_ref[0, 0, j:j + 1, :] * LOG2_E
        kaug_ref[rows, 0:HEAD_DIM] = k_ref[0, rows, :]
        kaug_ref[rows, HEAD_DIM:2 * HEAD_DIM] = bias_rows(c_j, False).astype(BF16).T
        vt_ref[j, 0:HEAD_DIM, :] = v_ref[0, rows, :].T
        vt_ref[j, HEAD_DIM:, :] = jnp.ones((FOX_ONES_ROWS, T), BF16)
        qaug_ref[j, 0:HEAD_DIM, :] = (q_ref[0, rows, :].astype(F32) * (HEAD_DIM ** -0.5 * LOG2_E)).astype(BF16).T
        qaug_ref[j, HEAD_DIM:2 * HEAD_DIM, :] = bias_rows(c_j, True).astype(BF16)
    key_id = lax.broadcasted_iota(jnp.int32, (T, T), 0)
    query_id = lax.broadcasted_iota(jnp.int32, (T, T), 1)

    def form_logits(qb, kb, s_ref):
        s_ref[...] = jnp.dot(kaug_ref[kb * T:(kb + 1) * T, :], qaug_ref[qb],
                             preferred_element_type=F32)

    def softmax_update(qb, kb, s_ref, p_ref, a_ref):
        s = s_ref[...]
        if qb == kb:
            s = jnp.where(key_id <= query_id, s, MASK_VALUE)
            m_new = jnp.max(s, axis=0, keepdims=True)
        else:
            m = m_ref[qb]
            m_new = jnp.maximum(m, jnp.max(s, axis=0, keepdims=True))
            a_ref[...] = jnp.exp2(m - m_new)
        p_ref[...] = jnp.exp2(s - m_new).astype(BF16)
        m_ref[qb] = m_new

    def accumulate(qb, kb, p_ref, a_ref):
        update = jnp.dot(vt_ref[kb], p_ref[...], preferred_element_type=F32)
        acc_ref[qb] = update if qb == kb else a_ref[...] * acc_ref[qb] + update

    pairs = [(qb, qb) for qb in range(n_blocks)] + [(qb, kb) for qb in range(n_blocks) for kb in range(qb)]
    buffers = ((s0_ref, p0_ref, a0_ref), (s1_ref, p1_ref, a1_ref))
    form_logits(*pairs[0], s0_ref)
    for n in range(len(pairs) + 1):
        s_now, p_now, a_now = buffers[n % 2]
        s_next, p_prev, a_prev = buffers[(n + 1) % 2]
        if n + 1 < len(pairs):
            form_logits(*pairs[n + 1], s_next)
        if n < len(pairs):
            softmax_update(*pairs[n], s_now, p_now, a_now)
        if n >= 1:
            accumulate(*pairs[n - 1], p_prev, a_prev)

    for j in range(n_blocks):
        rows = slice(j * T, (j + 1) * T)
        acc = acc_ref[j]
        o = (acc[:HEAD_DIM, :] / acc[HEAD_DIM:HEAD_DIM + 1, :]).T
        o_ref[0, rows, :] = (o * _sigmoid(g_ref[0, rows, :].astype(F32))).astype(o_ref.dtype)


def _fox(proj, c, n_heads):
    bsz, seq, _ = proj.shape
    T = FOX_BLOCK
    n_blocks = seq // T
    c_rows = jnp.transpose(c[:, :, :n_heads], (0, 2, 1)).reshape(bsz, n_heads, n_blocks, T)

    def head_cols(group):
        return pl.BlockSpec((1, seq, HEAD_DIM), lambda b, h: (b, 0, group * n_heads + h))

    return pl.pallas_call(
        _fox_kernel,
        grid=(bsz, n_heads),
        in_specs=[head_cols(0), head_cols(1), head_cols(2), head_cols(3),
                  pl.BlockSpec((1, 1, n_blocks, T), lambda b, h: (b, h, 0, 0))],
        out_specs=pl.BlockSpec((1, seq, HEAD_DIM), lambda b, h: (b, 0, h)),
        out_shape=jax.ShapeDtypeStruct((bsz, seq, n_heads * HEAD_DIM), BF16),
        scratch_shapes=[pltpu.VMEM((seq, 2 * HEAD_DIM), BF16),
                        pltpu.VMEM((n_blocks, HEAD_DIM + FOX_ONES_ROWS, T), BF16),
                        pltpu.VMEM((n_blocks, 2 * HEAD_DIM, T), BF16),
                        pltpu.VMEM((n_blocks, 1, T), F32),
                        pltpu.VMEM((n_blocks, HEAD_DIM + FOX_ONES_ROWS, T), F32),
                        pltpu.VMEM((T, T), F32), pltpu.VMEM((T, T), F32),
                        pltpu.VMEM((T, T), BF16), pltpu.VMEM((T, T), BF16),
                        pltpu.VMEM((1, T), F32), pltpu.VMEM((1, T), F32)],
        compiler_params=_params("parallel", "parallel"),
        name="fox_attention",
    )(proj, proj, proj, proj, c_rows)


def _ffn_up_kernel(x_ref, wg_ref, wv_ref, cg_ref, cv_ref, o_ref, ug_ref, uv_ref, wgb_ref, wvb_ref,
                   *, seq_steps, d_ff):
    rows = x_ref.shape[0]

    @pl.when(pl.program_id(1) == 0)
    def _():
        valid = d_ff - pl.program_id(0) * wg_ref.shape[1]
        col = lax.broadcasted_iota(jnp.int32, wg_ref.shape, 1)
        wgb_ref[...] = jnp.where(col < valid, wg_ref[...], 0.0).astype(BF16)
        wvb_ref[...] = jnp.where(col < valid, wv_ref[...], 0.0).astype(BF16)

    @pl.when(pl.program_id(1) % seq_steps == 0)
    def _():
        ug_ref[0:SUBLANES, :] = jnp.zeros((SUBLANES, ug_ref.shape[1]), F32)
        uv_ref[0:SUBLANES, :] = jnp.zeros((SUBLANES, uv_ref.shape[1]), F32)

    x = x_ref[...]

    def conv(wb_ref, c_ref, u_ref):
        u_ref[SUBLANES:SUBLANES + rows, :] = jnp.dot(x, wb_ref[...], preferred_element_type=F32)
        out = c_ref[CONV_WIDTH - 1:CONV_WIDTH, :] * u_ref[SUBLANES:SUBLANES + rows, :]
        for tap in range(CONV_WIDTH - 1):
            back = CONV_WIDTH - 1 - tap
            out = out + c_ref[tap:tap + 1, :] * u_ref[SUBLANES - back:SUBLANES - back + rows, :]
        carry = u_ref[rows:rows + SUBLANES, :]
        u_ref[0:SUBLANES, :] = carry
        return out

    gate = conv(wgb_ref, cg_ref, ug_ref)
    val = conv(wvb_ref, cv_ref, uv_ref)
    o_ref[...] = (_silu(gate) * val).astype(o_ref.dtype)


def _ffn_up(h, w_up, layer, conv_w, seq, d_ff, d_ff_pad):
    t, d = h.shape
    rows = min(MM_ROWS, seq)
    assert seq % rows == 0 and d_ff_pad % FFN_COLS == 0
    n_cols = d_ff_pad // FFN_COLS
    return pl.pallas_call(
        functools.partial(_ffn_up_kernel, seq_steps=seq // rows, d_ff=d_ff),
        grid=(n_cols, t // rows),
        in_specs=[pl.BlockSpec((rows, d), lambda j, i: (i, 0)),
                  pl.BlockSpec((None, d, FFN_COLS), lambda j, i: (layer, 0, j)),
                  pl.BlockSpec((None, pl.Element(d), pl.Element(FFN_COLS, (0, d_ff_pad - d_ff))),
                               lambda j, i: (layer, 0, pl.multiple_of(d_ff + j * FFN_COLS, LANES))),
                  pl.BlockSpec((CONV_WIDTH, FFN_COLS), lambda j, i: (0, j)),
                  pl.BlockSpec((CONV_WIDTH, FFN_COLS), lambda j, i: (0, j + n_cols))],
        out_specs=pl.BlockSpec((rows, FFN_COLS), lambda j, i: (i, j)),
        out_shape=jax.ShapeDtypeStruct((t, d_ff), BF16),
        scratch_shapes=[pltpu.VMEM((rows + SUBLANES, FFN_COLS), F32),
                        pltpu.VMEM((rows + SUBLANES, FFN_COLS), F32),
                        pltpu.VMEM((d, FFN_COLS), BF16),
                        pltpu.VMEM((d, FFN_COLS), BF16)],
        compiler_params=_params("parallel", "arbitrary"),
        name="ffn_up_conv",
    )(h, w_up, w_up, conv_w, conv_w)


def _pad_cols(w, width):
    return jnp.pad(w, ((0, 0), (0, width - w.shape[1])))


def kernel(x, attn_norm_w, ffn_norm_w, final_norm_w, even_w_in, hgrn_lb_logits, hgrn_norm_w, ret_norm_w,
           even_w_out, odd_w_in, odd_b_f, odd_w_out, ffn_w_up, ffn_conv_w, ffn_w_down):
    bsz, seq, d_model = x.shape
    depth = attn_norm_w.shape[0]
    tokens = bsz * seq
    d_hgrn = hgrn_norm_w.shape[1]
    d_fox = odd_w_out.shape[1]
    n_fox = odd_b_f.shape[1]
    d_ff = ffn_w_down.shape[1]
    d_ff_pad = FFN_COLS * math.ceil(d_ff / FFN_COLS)

    xf = x.reshape(tokens, d_model)
    h = _rmsnorm(xf, attn_norm_w[0], BF16)
    for layer in range(depth):
        j = layer // 2
        if layer % 2 == 0:
            proj = _matmul(h, even_w_in, j, F32, even_w_in.shape[2], rows=MM_ROWS, cols=IN_COLS,
                           name="even_in").reshape(bsz, seq, -1)
            o_a = _hgrn(proj, hgrn_lb_logits, hgrn_norm_w[j], j)
            o_b = _retention(proj, ret_norm_w[j], first_group=4 * d_hgrn // ret_norm_w.shape[1])
            mixed = [o_a.reshape(tokens, -1), o_b.reshape(tokens, -1)]
            w_out = even_w_out[j]
        else:
            proj = _matmul(h, odd_w_in, j, BF16, 4 * d_fox, rows=MM_ROWS, cols=IN_COLS,
                           name="fox_in").reshape(bsz, seq, -1)
            c = _forget_scan(h.reshape(bsz, seq, d_model), odd_w_in, j, 4 * d_fox,
                             _pad_cols(odd_b_f[j][None, :], LANES))
            mixed = [_fox(proj, c, n_fox).reshape(tokens, -1)]
            w_out = odd_w_out[j]
        xf, h = _matmul_residual_norm(mixed, w_out.astype(BF16), xf, ffn_norm_w[layer], BF16, rows=OUT_ROWS,
                                      name="mixer_out_norm")
        conv_w = ffn_conv_w[layer]
        conv_w = jnp.concatenate([_pad_cols(conv_w[:, :d_ff], d_ff_pad), _pad_cols(conv_w[:, d_ff:], d_ff_pad)],
                                 axis=1)
        act = _ffn_up(h, ffn_w_up, layer, conv_w, seq, d_ff, d_ff_pad)
        w_down = ffn_w_down[layer].astype(BF16)
        last = layer == depth - 1
        xf, h = _matmul_residual_norm([act], w_down, xf, final_norm_w if last else attn_norm_w[layer + 1],
                                      x.dtype if last else BF16, rows=DOWN_ROWS, keep_sum=not last,
                                      name="ffn_down_norm")
    return h.reshape(bsz, seq, d_model)
```
